```python
import math
import jax, jax.numpy as jnp
from jax import lax
import numpy as np

D_MODEL = 2048
BATCH = 1
SEQ = 8192
DEPTH = 4

HEAD_DIM = 128
MOBA_HEADS = 6
DIFF_HEADS = 6
SGU_GROUPS = 4
SGU_GROUP_DIM = HEAD_DIM
MOBA_WIDTH = MOBA_HEADS * HEAD_DIM
DIFF_WIDTH = DIFF_HEADS * HEAD_DIM
SGU_WIDTH = SGU_GROUPS * SGU_GROUP_DIM
DIFF_QK_DIM = HEAD_DIM // 2
MOBA_BLOCK = 256
MOBA_TOPK = 3
MOBA_Q_CHUNK = 64
ATTN_Q_BLOCK = 128
SGU_CHUNK = 128
D_FF = 4 * D_MODEL
ROPE_THETA = 10000.0
EPS = 1e-6
IN_WIDTH = 3 * MOBA_WIDTH + 3 * DIFF_WIDTH + 2 * SGU_WIDTH
IN_SPLITS = [MOBA_WIDTH, 2 * MOBA_WIDTH, 3 * MOBA_WIDTH,
             3 * MOBA_WIDTH + DIFF_WIDTH, 3 * MOBA_WIDTH + 2 * DIFF_WIDTH,
             3 * MOBA_WIDTH + 3 * DIFF_WIDTH, 3 * MOBA_WIDTH + 3 * DIFF_WIDTH + SGU_WIDTH]

kernel_name = 'hybrid_moba_diffattn_gmlp_trunk'


def rms_norm(x, g):
    xf = x.astype(jnp.float32)
    y = xf * lax.rsqrt(jnp.mean(xf * xf, axis=-1, keepdims=True) + EPS)
    return (y * g.astype(jnp.float32)).astype(x.dtype)


def rope_tables(seq, dim):
    inv = 1.0 / (ROPE_THETA ** (jnp.arange(0, dim, 2, dtype=jnp.float32) / dim))
    ang = jnp.arange(seq, dtype=jnp.float32)[:, None] * inv[None, :]
    return jnp.cos(ang), jnp.sin(ang)


def apply_rope(x, cos, sin):
    half = x.shape[-1] // 2
    bshape = (1, x.shape[1]) + (1,) * (x.ndim - 3) + (half,)
    c = cos.reshape(bshape)
    s = sin.reshape(bshape)
    xf = x.astype(jnp.float32)
    x1, x2 = xf[..., :half], xf[..., half:]
    return jnp.concatenate([x1 * c - x2 * s, x2 * c + x1 * s], axis=-1).astype(x.dtype)


def moba_attention(q, k, v, cos, sin):
    B, S, H, Dh = q.shape
    q = apply_rope(q, cos, sin)
    k = apply_rope(k, cos, sin)
    n_blk = -(-S // MOBA_BLOCK)
    s_pad = n_blk * MOBA_BLOCK
    pad = ((0, 0), (0, 0), (0, s_pad - S), (0, 0))
    q = jnp.pad(q.transpose(0, 2, 1, 3), pad)
    k = jnp.pad(k.transpose(0, 2, 1, 3), pad)
    v = jnp.pad(v.transpose(0, 2, 1, 3), pad)
    kb = k.reshape(B, H, n_blk, MOBA_BLOCK, Dh)
    vb = v.reshape(B, H, n_blk, MOBA_BLOCK, Dh)
    k_mean = jnp.mean(kb.astype(jnp.float32), axis=3)
    scale = Dh ** -0.5
    k_sel = min(MOBA_TOPK, n_blk)
    n_chunks = s_pad // MOBA_Q_CHUNK
    qc = q.reshape(B, H, n_chunks, MOBA_Q_CHUNK, Dh).transpose(2, 0, 1, 3, 4)
    bi = jnp.arange(B)[:, None, None, None]
    hi = jnp.arange(H)[None, :, None, None]
    blk_ids = jnp.arange(n_blk)

    def chunk_fn(args):
        q_c, c = args
        q_start = c * MOBA_Q_CHUNK
        own = q_start // MOBA_BLOCK
        gate = jnp.einsum('bhqd,bhnd->bhqn', q_c.astype(jnp.float32), k_mean)
        gate = jnp.where(blk_ids[None, None, None, :] < own, gate, -jnp.inf)
        _, sel = lax.top_k(gate, k_sel)
        sel_valid = jnp.arange(k_sel) < own
        k_g = kb[bi, hi, sel]
        v_g = vb[bi, hi, sel]
        s_sel = jnp.einsum('bhqd,bhqkld->bhqkl', q_c, k_g).astype(jnp.float32) * scale
        s_sel = jnp.where(sel_valid[None, None, None, :, None], s_sel, -jnp.inf)
        s_sel = s_sel.reshape(B, H, MOBA_Q_CHUNK, k_sel * MOBA_BLOCK)
        k_own = lax.dynamic_slice_in_dim(kb, own, 1, axis=2)[:, :, 0]
        v_own = lax.dynamic_slice_in_dim(vb, own, 1, axis=2)[:, :, 0]
        s_own = jnp.einsum('bhqd,bhld->bhql', q_c, k_own).astype(jnp.float32) * scale
        q_pos = q_start + jnp.arange(MOBA_Q_CHUNK)
        k_pos = own * MOBA_BLOCK + jnp.arange(MOBA_BLOCK)
        s_own = jnp.where(k_pos[None, :] <= q_pos[:, None], s_own, -jnp.inf)
        p = jax.nn.softmax(jnp.concatenate([s_sel, s_own], axis=-1), axis=-1).astype(v.dtype)
        p_sel = p[..., :k_sel * MOBA_BLOCK].reshape(B, H, MOBA_Q_CHUNK, k_sel, MOBA_BLOCK)
        p_own = p[..., k_sel * MOBA_BLOCK:]
        return (jnp.einsum('bhqkl,bhqkld->bhqd', p_sel, v_g)
                + jnp.einsum('bhql,bhld->bhqd', p_own, v_own))

    out = lax.map(chunk_fn, (qc, jnp.arange(n_chunks)))
    out = out.transpose(1, 2, 0, 3, 4).reshape(B, H, s_pad, Dh)[:, :, :S]
    return out.transpose(0, 2, 1, 3)


def diff_attention(q, k, v, lam_params, subln_g, lambda_init, cos, sin):
    B, S, H, _, dq = q.shape
    q = apply_rope(q, cos, sin)
    k = apply_rope(k, cos, sin)
    lp = lam_params.astype(jnp.float32)
    lam = jnp.exp(jnp.sum(lp[0] * lp[1])) - jnp.exp(jnp.sum(lp[2] * lp[3])) + lambda_init
    scale = dq ** -0.5
    nqb = S // ATTN_Q_BLOCK
    qb = q.reshape(B, nqb, ATTN_Q_BLOCK, H, 2, dq).transpose(1, 0, 2, 3, 4, 5)
    k_pos = jnp.arange(S)

    def block_fn(args):
        q_blk, i = args
        s = jnp.einsum('bqhcd,bkhcd->bhcqk', q_blk, k).astype(jnp.float32) * scale
        q_pos = i * ATTN_Q_BLOCK + jnp.arange(ATTN_Q_BLOCK)
        s = jnp.where(k_pos[None, :] <= q_pos[:, None], s, -jnp.inf)
        p = jax.nn.softmax(s, axis=-1)
        a = p[:, :, 0] - lam * p[:, :, 1]
        return jnp.einsum('bhqk,bkhd->bqhd', a.astype(v.dtype), v)

    o = lax.map(block_fn, (qb, jnp.arange(nqb)))
    o = o.transpose(1, 0, 2, 3, 4).reshape(B, S, H, v.shape[-1])
    return rms_norm(o, subln_g) * (1.0 - lambda_init)


def spatial_gating(u, v, ln_g, ln_b, w_s, b_s):
    B, S, G, C = v.shape
    vf = v.astype(jnp.float32)
    mu = jnp.mean(vf, axis=-1, keepdims=True)
    var = jnp.mean(jnp.square(vf - mu), axis=-1, keepdims=True)
    vn = ((vf - mu) * lax.rsqrt(var + EPS) * ln_g.astype(jnp.float32) + ln_b.astype(jnp.float32)).astype(v.dtype)
    nc = S // SGU_CHUNK
    vn = vn.reshape(B, nc, SGU_CHUNK, G, C)
    tri = jnp.tril(jnp.ones((SGU_CHUNK, SGU_CHUNK), dtype=bool))
    w = jnp.where(tri[None], w_s, jnp.zeros_like(w_s))
    mixed = jnp.einsum('gts,bnsgc->bntgc', w, vn) + b_s.T[None, None, :, :, None]
    return u * mixed.reshape(B, S, G, C)


def setup_inputs(seed: int = 0) -> dict:
    key = jax.random.key(seed)
    ks = jax.random.split(key, 14)
    f32 = jnp.float32

    def nrm(k, shape, scale):
        return jax.random.normal(k, shape, f32) * scale

    return {
        'x': nrm(ks[0], (BATCH, SEQ, D_MODEL), 1.0),
        'attn_norm_g': 1.0 + nrm(ks[1], (DEPTH, D_MODEL), 0.02),
        'w_in': nrm(ks[2], (DEPTH, D_MODEL, IN_WIDTH), D_MODEL ** -0.5),
        'diff_lambda': nrm(ks[3], (DEPTH, 4, DIFF_QK_DIM), 0.1),
        'diff_subln_g': 1.0 + nrm(ks[4], (DEPTH, HEAD_DIM), 0.02),
        'sgu_ln_g': 1.0 + nrm(ks[5], (DEPTH, SGU_GROUPS, SGU_GROUP_DIM), 0.02),
        'sgu_ln_b': nrm(ks[6], (DEPTH, SGU_GROUPS, SGU_GROUP_DIM), 0.02),
        'sgu_w': nrm(ks[7], (DEPTH, SGU_GROUPS, SGU_CHUNK, SGU_CHUNK), SGU_CHUNK ** -0.5),
        'sgu_b': 1.0 + nrm(ks[8], (DEPTH, SGU_GROUPS, SGU_CHUNK), 0.1),
        'w_out': nrm(ks[9], (DEPTH, D_MODEL, D_MODEL), D_MODEL ** -0.5),
        'mlp_norm_g': 1.0 + nrm(ks[10], (DEPTH, D_MODEL), 0.02),
        'w_mlp_in': nrm(ks[11], (DEPTH, D_MODEL, D_FF), D_MODEL ** -0.5),
        'w_mlp_out': nrm(ks[12], (DEPTH, D_FF, D_MODEL), D_FF ** -0.5),
        'final_norm_g': 1.0 + nrm(ks[13], (D_MODEL,), 0.02),
    }


def reference(x, attn_norm_g, w_in, diff_lambda, diff_subln_g, sgu_ln_g, sgu_ln_b,
              sgu_w, sgu_b, w_out, mlp_norm_g, w_mlp_in, w_mlp_out, final_norm_g):
    B, S, _ = x.shape
    cos_m, sin_m = rope_tables(S, HEAD_DIM)
    cos_d, sin_d = rope_tables(S, DIFF_QK_DIM)
    for l in range(DEPTH):
        lambda_init = 0.8 - 0.6 * math.exp(-0.3 * l)
        h = rms_norm(x, attn_norm_g[l])
        proj = h @ w_in[l]
        mq, mk, mv, dq, dk, dv, su, sv = jnp.split(proj, IN_SPLITS, axis=-1)
        moba_o = moba_attention(mq.reshape(B, S, MOBA_HEADS, HEAD_DIM),
                                mk.reshape(B, S, MOBA_HEADS, HEAD_DIM),
                                mv.reshape(B, S, MOBA_HEADS, HEAD_DIM), cos_m, sin_m)
        diff_o = diff_attention(dq.reshape(B, S, DIFF_HEADS, 2, DIFF_QK_DIM),
                                dk.reshape(B, S, DIFF_HEADS, 2, DIFF_QK_DIM),
                                dv.reshape(B, S, DIFF_HEADS, HEAD_DIM),
                                diff_lambda[l], diff_subln_g[l], lambda_init, cos_d, sin_d)
        sgu_o = spatial_gating(jax.nn.gelu(su).reshape(B, S, SGU_GROUPS, SGU_GROUP_DIM),
                               jax.nn.gelu(sv).reshape(B, S, SGU_GROUPS, SGU_GROUP_DIM),
                               sgu_ln_g[l], sgu_ln_b[l], sgu_w[l], sgu_b[l])
        mix = jnp.concatenate([moba_o.reshape(B, S, MOBA_WIDTH),
                               diff_o.reshape(B, S, DIFF_WIDTH),
                               sgu_o.reshape(B, S, SGU_WIDTH)], axis=-1)
        x = x + mix @ w_out[l]
        h = rms_norm(x, mlp_norm_g[l])
        x = x + jnp.square(jax.nn.relu(h @ w_mlp_in[l])) @ w_mlp_out[l]
    return rms_norm(x, final_norm_g)
```

```python
import functools
import math

import jax
import jax.numpy as jnp
from jax import lax
from jax.experimental import pallas as pl
from jax.experimental.pallas import tpu as pltpu

D_MODEL = 2048
SEQ = 8192
DEPTH = 4
HEAD_DIM = 128
MOBA_HEADS = 6
DIFF_HEADS = 6
SGU_GROUPS = 4
MOBA_WIDTH = MOBA_HEADS * HEAD_DIM
DIFF_WIDTH = DIFF_HEADS * HEAD_DIM
SGU_WIDTH = SGU_GROUPS * HEAD_DIM
DIFF_QK_DIM = HEAD_DIM // 2
MOBA_BLOCK = 256
MOBA_TOPK = 3
SGU_CHUNK = 128
D_FF = 4 * D_MODEL
ROPE_THETA = 10000.0
EPS = 1e-6
QKV_WIDTH = 3 * MOBA_WIDTH + 3 * DIFF_WIDTH
N_BLOCKS = SEQ // MOBA_BLOCK

VMEM_LIMIT_BYTES = 56 * 1024 * 1024

PROJ_TM = 512
PROJ_TN = 768
SGU_TM = 512
ATTN_TQ = MOBA_BLOCK
ATTN_TK = MOBA_BLOCK
OUT_TM = 512
MLP_TM = 512
MLP_TF = 1024

NEG_INF = float("-inf")


def _cparams(sem):
    return pltpu.CompilerParams(dimension_semantics=sem, vmem_limit_bytes=VMEM_LIMIT_BYTES)


def _rms_norm_rows(x, g):
    return x * lax.rsqrt(jnp.mean(x * x, axis=-1, keepdims=True) + EPS) * g


def _gelu_tanh(x):
    c = math.sqrt(2.0 / math.pi)
    return 0.5 * x * (1.0 + jnp.tanh(c * (x + 0.044715 * (x * x * x))))


def _qkv_proj_kernel(x_ref, g_ref, w_ref, cm_ref, sm_ref, cd_ref, sdl_ref, sdh_ref,
                     o_ref, kmean_ref, h_ref):
    j = pl.program_id(1)

    @pl.when(j == 0)
    def _():
        h_ref[...] = _rms_norm_rows(x_ref[...], g_ref[...]).astype(jnp.bfloat16)

    acc = jnp.dot(h_ref[...], w_ref[...], preferred_element_type=jnp.float32)

    def rope_moba(xh):
        return xh * cm_ref[...] + pltpu.roll(xh, HEAD_DIM // 2, 1) * sm_ref[...]

    def rope_diff(xh):
        return (xh * cd_ref[...]
                + pltpu.roll(xh, HEAD_DIM - DIFF_QK_DIM // 2, 1) * sdl_ref[...]
                + pltpu.roll(xh, DIFF_QK_DIM // 2, 1) * sdh_ref[...])

    def heads(fn, with_mean):
        for h in range(PROJ_TN // HEAD_DIM):
            sl = slice(h * HEAD_DIM, (h + 1) * HEAD_DIM)
            r = fn(acc[:, sl])
            o_ref[:, sl] = r.astype(o_ref.dtype)
            if with_mean:
                for b in range(PROJ_TM // MOBA_BLOCK):
                    blk = r[b * MOBA_BLOCK:(b + 1) * MOBA_BLOCK]
                    kmean_ref[0, b:b + 1, sl] = jnp.mean(blk, axis=0, keepdims=True)

    @pl.when(j == 0)
    def _():
        heads(rope_moba, False)

    @pl.when(j == 1)
    def _():
        heads(rope_moba, True)

    @pl.when((j == 2) | (j == 5))
    def _():
        o_ref[...] = acc.astype(o_ref.dtype)

    @pl.when((j == 3) | (j == 4))
    def _():
        heads(rope_diff, False)


def _qkv_proj(x, g, w_qkv, cm, sm, cd, sdl, sdh):
    tm, tn = PROJ_TM, PROJ_TN
    grid = (SEQ // tm, QKV_WIDTH // tn)
    row_tab = pl.BlockSpec((tm, HEAD_DIM), lambda i, j: (i, 0))
    return pl.pallas_call(
        _qkv_proj_kernel,
        grid=grid,
        in_specs=[
            pl.BlockSpec((tm, D_MODEL), lambda i, j: (i, 0)),
            pl.BlockSpec((1, D_MODEL), lambda i, j: (0, 0)),
            pl.BlockSpec((D_MODEL, tn), lambda i, j: (0, j)),
            row_tab, row_tab, row_tab, row_tab, row_tab,
        ],
        out_specs=[
            pl.BlockSpec((tm, tn), lambda i, j: (i, j)),
            pl.BlockSpec((1, tm // MOBA_BLOCK, MOBA_WIDTH), lambda i, j: (i, 0, 0)),
        ],
        out_shape=[
            jax.ShapeDtypeStruct((SEQ, QKV_WIDTH), jnp.bfloat16),
            jax.ShapeDtypeStruct((SEQ // tm, tm // MOBA_BLOCK, MOBA_WIDTH), jnp.float32),
        ],
        scratch_shapes=[pltpu.VMEM((tm, D_MODEL), jnp.bfloat16)],
        compiler_params=_cparams(("arbitrary", "arbitrary")),
        name="qkv_proj",
    )(x, g, w_qkv, cm, sm, cd, sdl, sdh)


def _sgu_kernel(x_ref, g_ref, w_ref, lng_ref, lnb_ref, ws_ref, bs_ref, o_ref):
    h = _rms_norm_rows(x_ref[...], g_ref[...]).astype(jnp.bfloat16)
    uv = jnp.dot(h, w_ref[...], preferred_element_type=jnp.float32)
    row = lax.broadcasted_iota(jnp.int32, (SGU_CHUNK, SGU_CHUNK), 0)
    col = lax.broadcasted_iota(jnp.int32, (SGU_CHUNK, SGU_CHUNK), 1)
    for g in range(SGU_GROUPS):
        sl = slice(g * HEAD_DIM, (g + 1) * HEAD_DIM)
        u = _gelu_tanh(uv[:, sl])
        v = _gelu_tanh(uv[:, SGU_WIDTH + g * HEAD_DIM:SGU_WIDTH + (g + 1) * HEAD_DIM])
        mu = jnp.mean(v, axis=-1, keepdims=True)
        vc = v - mu
        var = jnp.mean(vc * vc, axis=-1, keepdims=True)
        vn = (vc * lax.rsqrt(var + EPS) * lng_ref[g:g + 1, :] + lnb_ref[g:g + 1, :]).astype(jnp.bfloat16)
        w = jnp.where(row >= col, ws_ref[g], 0.0).astype(jnp.bfloat16)
        bias = bs_ref[g]
        for c in range(SGU_TM // SGU_CHUNK):
            rs = slice(c * SGU_CHUNK, (c + 1) * SGU_CHUNK)
            mixed = jnp.dot(w, vn[rs], preferred_element_type=jnp.float32) + bias
            o_ref[rs, sl] = (u[rs] * mixed).astype(o_ref.dtype)


def _sgu(x, g, w_sgu, ln_g, ln_b, w_s, b_s):
    tm = SGU_TM
    full = lambda shape: pl.BlockSpec(shape, lambda i: (0,) * len(shape))
    return pl.pallas_call(
        _sgu_kernel,
        grid=(SEQ // tm,),
        in_specs=[
            pl.BlockSpec((tm, D_MODEL), lambda i: (i, 0)),
            full((1, D_MODEL)),
            full((D_MODEL, 2 * SGU_WIDTH)),
            full((SGU_GROUPS, HEAD_DIM)),
            full((SGU_GROUPS, HEAD_DIM)),
            full((SGU_GROUPS, SGU_CHUNK, SGU_CHUNK)),
            full((SGU_GROUPS, SGU_CHUNK, 1)),
        ],
        out_specs=pl.BlockSpec((tm, SGU_WIDTH), lambda i: (i, 0)),
        out_shape=jax.ShapeDtypeStruct((SEQ, SGU_WIDTH), jnp.bfloat16),
        compiler_params=_cparams(("arbitrary",)),
        name="sgu",
    )(x, g, w_sgu, ln_g, ln_b, w_s, b_s)


def _flash_update(s, v, m_ref, l_ref, acc_ref):
    m_prev = m_ref[...]
    m_new = jnp.maximum(m_prev, jnp.max(s, axis=-1, keepdims=True))
    alpha = jnp.exp(m_prev - m_new)
    p = jnp.exp(s - m_new)
    l_ref[...] = alpha * l_ref[...] + jnp.sum(p, axis=-1, keepdims=True)
    acc_ref[...] = alpha * acc_ref[...] + jnp.dot(p.astype(jnp.bfloat16), v,
                                                  preferred_element_type=jnp.float32)
    m_ref[...] = m_new


def _scores(q, k):
    return lax.dot_general(q, k, (((1,), (1,)), ((), ())), preferred_element_type=jnp.float32)


def _moba_kernel(q_ref, k_ref, v_ref, kmean_ref, o_ref, m_ref, l_ref, acc_ref, sel_ref):
    qi = pl.program_id(1)
    tq, tk = ATTN_TQ, ATTN_TK
    q = q_ref[...]
    scale = HEAD_DIM ** -0.5

    gate = lax.dot_general(q.astype(jnp.float32), kmean_ref[...], (((1,), (1,)), ((), ())),
                           preferred_element_type=jnp.float32,
                           precision=lax.Precision.HIGHEST)
    blk = lax.broadcasted_iota(jnp.int32, (tq, N_BLOCKS), 1)
    gate = jnp.where(blk < qi, gate, NEG_INF)
    sel = jnp.zeros((tq, N_BLOCKS), jnp.float32)
    for _ in range(MOBA_TOPK):
        mx = jnp.max(gate, axis=-1, keepdims=True)
        cand = jnp.where((gate == mx) & (gate > NEG_INF), blk, N_BLOCKS)
        idx = jnp.min(cand, axis=-1, keepdims=True)
        pick = blk == idx
        sel = jnp.where(pick, 1.0, sel)
        gate = jnp.where(pick, NEG_INF, gate)
    sel_ref[...] = sel

    m_ref[...] = jnp.full(m_ref.shape, NEG_INF, jnp.float32)
    l_ref[...] = jnp.zeros(l_ref.shape, jnp.float32)
    acc_ref[...] = jnp.zeros(acc_ref.shape, jnp.float32)

    own = pl.multiple_of(qi * tk, tk)
    r = lax.broadcasted_iota(jnp.int32, (tq, tk), 0)
    c = lax.broadcasted_iota(jnp.int32, (tq, tk), 1)
    s = _scores(q, k_ref[pl.ds(own, tk), :]) * scale
    s = jnp.where(c <= r, s, NEG_INF)
    _flash_update(s, v_ref[pl.ds(own, tk), :], m_ref, l_ref, acc_ref)

    def body(j, carry):
        start = pl.multiple_of(j * tk, tk)
        s = _scores(q, k_ref[pl.ds(start, tk), :]) * scale
        picked = jnp.sum(jnp.where(blk == j, sel_ref[...], 0.0), axis=-1, keepdims=True)
        s = jnp.where(picked > 0.0, s, NEG_INF)
        _flash_update(s, v_ref[pl.ds(start, tk), :], m_ref, l_ref, acc_ref)
        return carry

    lax.fori_loop(0, qi, body, 0)
    o_ref[...] = (acc_ref[...] / l_ref[...]).astype(o_ref.dtype)


def _moba_attention(qkv, kmean):
    tq = ATTN_TQ
    return pl.pallas_call(
        _moba_kernel,
        grid=(MOBA_HEADS, SEQ // tq),
        in_specs=[
            pl.BlockSpec((tq, HEAD_DIM), lambda h, i: (i, h)),
            pl.BlockSpec((SEQ, HEAD_DIM), lambda h, i: (0, MOBA_HEADS + h)),
            pl.BlockSpec((SEQ, HEAD_DIM), lambda h, i: (0, 2 * MOBA_HEADS + h)),
            pl.BlockSpec((N_BLOCKS, HEAD_DIM), lambda h, i: (0, h)),
        ],
        out_specs=pl.BlockSpec((tq, HEAD_DIM), lambda h, i: (i, h)),
        out_shape=jax.ShapeDtypeStruct((SEQ, MOBA_WIDTH), jnp.bfloat16),
        scratch_shapes=[
            pltpu.VMEM((tq, 1), jnp.float32),
            pltpu.VMEM((tq, 1), jnp.float32),
            pltpu.VMEM((tq, HEAD_DIM), jnp.float32),
            pltpu.VMEM((tq, N_BLOCKS), jnp.float32),
        ],
        compiler_params=_cparams(("arbitrary", "arbitrary")),
        name="moba_attention",
    )(qkv, qkv, qkv, kmean)


def _diff_kernel(lam_init, q_ref, k_ref, v_ref, lp_ref, g_ref, o_ref, m_ref, l_ref, acc_ref):
    qi = pl.program_id(1)
    tq, tk = ATTN_TQ, ATTN_TK
    q = q_ref[...] * (DIFF_QK_DIM ** -0.5)
    lane = lax.broadcasted_iota(jnp.int32, (tq, HEAD_DIM), 1)
    zero = jnp.zeros_like(q)
    q2 = jnp.concatenate([jnp.where(lane < DIFF_QK_DIM, q, zero),
                          jnp.where(lane >= DIFF_QK_DIM, q, zero)], axis=0)

    m_ref[...] = jnp.full(m_ref.shape, NEG_INF, jnp.float32)
    l_ref[...] = jnp.zeros(l_ref.shape, jnp.float32)
    acc_ref[...] = jnp.zeros(acc_ref.shape, jnp.float32)

    def body(j, carry):
        start = pl.multiple_of(j * tk, tk)
        s = _scores(q2, k_ref[pl.ds(start, tk), :])
        _flash_update(s, v_ref[pl.ds(start, tk), :], m_ref, l_ref, acc_ref)
        return carry

    lax.fori_loop(0, qi, body, 0)

    own = pl.multiple_of(qi * tk, tk)
    r = lax.broadcasted_iota(jnp.int32, (2 * tq, tk), 0)
    c = lax.broadcasted_iota(jnp.int32, (2 * tq, tk), 1)
    r = jnp.where(r >= tq, r - tq, r)
    s = _scores(q2, k_ref[pl.ds(own, tk), :])
    s = jnp.where(c <= r, s, NEG_INF)
    _flash_update(s, v_ref[pl.ds(own, tk), :], m_ref, l_ref, acc_ref)

    lp = lp_ref[...]
    lam = (jnp.exp(jnp.sum(lp[0:1] * lp[1:2], axis=-1, keepdims=True))
           - jnp.exp(jnp.sum(lp[2:3] * lp[3:4], axis=-1, keepdims=True)) + lam_init)
    o = acc_ref[...] / l_ref[...]
    o = o[:tq] - lam * o[tq:]
    o = _rms_norm_rows(o, g_ref[...]) * (1.0 - lam_init)
    o_ref[...] = o.astype(o_ref.dtype)


def _diff_attention(qkv, lam_params, subln_g, lam_init):
    tq = ATTN_TQ
    base = 3 * MOBA_HEADS
    return pl.pallas_call(
        functools.partial(_diff_kernel, lam_init),
        grid=(DIFF_HEADS, SEQ // tq),
        in_specs=[
            pl.BlockSpec((tq, HEAD_DIM), lambda h, i: (i, base + h)),
            pl.BlockSpec((SEQ, HEAD_DIM), lambda h, i: (0, base + DIFF_HEADS + h)),
            pl.BlockSpec((SEQ, HEAD_DIM), lambda h, i: (0, base + 2 * DIFF_HEADS + h)),
            pl.BlockSpec((4, DIFF_QK_DIM), lambda h, i: (0, 0)),
            pl.BlockSpec((1, HEAD_DIM), lambda h, i: (0, 0)),
        ],
        out_specs=pl.BlockSpec((tq, HEAD_DIM), lambda h, i: (i, h)),
        out_shape=jax.ShapeDtypeStruct((SEQ, DIFF_WIDTH), jnp.bfloat16),
        scratch_shapes=[
            pltpu.VMEM((2 * tq, 1), jnp.float32),
            pltpu.VMEM((2 * tq, 1), jnp.float32),
            pltpu.VMEM((2 * tq, HEAD_DIM), jnp.float32),
        ],
        compiler_params=_cparams(("arbitrary", "arbitrary")),
        name="diff_attention",
    )(qkv, qkv, qkv, lam_params, subln_g)


def _out_proj_kernel(x_ref, a_ref, b_ref, c_ref, w_ref, o_ref):
    acc = jnp.dot(a_ref[...], w_ref[0:MOBA_WIDTH, :], preferred_element_type=jnp.float32)
    acc += jnp.dot(b_ref[...], w_ref[MOBA_WIDTH:MOBA_WIDTH + DIFF_WIDTH, :],
                   preferred_element_type=jnp.float32)
    acc += jnp.dot(c_ref[...], w_ref[MOBA_WIDTH + DIFF_WIDTH:, :], preferred_element_type=jnp.float32)
    o_ref[...] = x_ref[...] + acc


def _out_proj(x, moba_o, diff_o, sgu_o, w_out):
    tm = OUT_TM
    return pl.pallas_call(
        _out_proj_kernel,
        grid=(SEQ // tm,),
        in_specs=[
            pl.BlockSpec((tm, D_MODEL), lambda i: (i, 0)),
            pl.BlockSpec((tm, MOBA_WIDTH), lambda i: (i, 0)),
            pl.BlockSpec((tm, DIFF_WIDTH), lambda i: (i, 0)),
            pl.BlockSpec((tm, SGU_WIDTH), lambda i: (i, 0)),
            pl.BlockSpec((D_MODEL, D_MODEL), lambda i: (0, 0)),
        ],
        out_specs=pl.BlockSpec((tm, D_MODEL), lambda i: (i, 0)),
        out_shape=jax.ShapeDtypeStruct((SEQ, D_MODEL), jnp.float32),
        compiler_params=_cparams(("arbitrary",)),
        name="out_proj",
    )(x, moba_o, diff_o, sgu_o, w_out)


def _mlp_kernel(final, x_ref, g_ref, w1_ref, w2_ref, gf_ref, o_ref, h_ref, acc_ref):
    j = pl.program_id(1)

    @pl.when(j == 0)
    def _():
        h_ref[...] = _rms_norm_rows(x_ref[...], g_ref[...]).astype(jnp.bfloat16)
        acc_ref[...] = jnp.zeros(acc_ref.shape, jnp.float32)

    a = jnp.dot(h_ref[...], w1_ref[...], preferred_element_type=jnp.float32)
    a = jnp.square(jnp.maximum(a, 0.0)).astype(jnp.bfloat16)
    acc_ref[...] += jnp.dot(a, w2_ref[...], preferred_element_type=jnp.float32)

    @pl.when(j == pl.num_programs(1) - 1)
    def _():
        y = x_ref[...] + acc_ref[...]
        if final:
            y = _rms_norm_rows(y, gf_ref[...])
        o_ref[...] = y


def _mlp(x, g, w1, w2, g_final, final):
    tm, tf = MLP_TM, MLP_TF
    return pl.pallas_call(
        functools.partial(_mlp_kernel, final),
        grid=(SEQ // tm, D_FF // tf),
        in_specs=[
            pl.BlockSpec((tm, D_MODEL), lambda i, j: (i, 0)),
            pl.BlockSpec((1, D_MODEL), lambda i, j: (0, 0)),
            pl.BlockSpec((D_MODEL, tf), lambda i, j: (0, j)),
            pl.BlockSpec((tf, D_MODEL), lambda i, j: (j, 0)),
            pl.BlockSpec((1, D_MODEL), lambda i, j: (0, 0)),
        ],
        out_specs=pl.BlockSpec((tm, D_MODEL), lambda i, j: (i, 0)),
        out_shape=jax.ShapeDtypeStruct((SEQ, D_MODEL), jnp.float32),
        scratch_shapes=[
            pltpu.VMEM((tm, D_MODEL), jnp.bfloat16),
            pltpu.VMEM((tm, D_MODEL), jnp.float32),
        ],
        compiler_params=_cparams(("arbitrary", "arbitrary")),
        name="mlp",
    )(x, g, w1, w2, g_final)


def _rope_tables():
    pos = jnp.arange(SEQ, dtype=jnp.float32)[:, None]

    def tab(dim):
        inv = 1.0 / (ROPE_THETA ** (jnp.arange(0, dim, 2, dtype=jnp.float32) / dim))
        ang = pos * inv[None, :]
        return jnp.cos(ang), jnp.sin(ang)

    cm, sm = tab(HEAD_DIM)
    cd, sd = tab(DIFF_QK_DIM)
    zd = jnp.zeros_like(sd)
    return (jnp.concatenate([cm, cm], axis=1), jnp.concatenate([-sm, sm], axis=1),
            jnp.concatenate([cd, cd, cd, cd], axis=1),
            jnp.concatenate([-sd, zd, -sd, zd], axis=1),
            jnp.concatenate([zd, sd, zd, sd], axis=1))


def kernel(x, attn_norm_g, w_in, diff_lambda, diff_subln_g, sgu_ln_g, sgu_ln_b, sgu_w, sgu_b,
           w_out, mlp_norm_g, w_mlp_in, w_mlp_out, final_norm_g):
    assert x.shape == (1, SEQ, D_MODEL)
    bf = jnp.bfloat16
    xs = x.reshape(SEQ, D_MODEL)
    cm, sm, cd, sdl, sdh = _rope_tables()
    gf = final_norm_g.reshape(1, D_MODEL)
    for l in range(DEPTH):
        lam_init = 0.8 - 0.6 * math.exp(-0.3 * l)
        g_attn = attn_norm_g[l].reshape(1, D_MODEL)
        w_qkv = w_in[l, :, :QKV_WIDTH].astype(bf)
        w_sgu = w_in[l, :, QKV_WIDTH:].astype(bf)
        qkv, kmean = _qkv_proj(xs, g_attn, w_qkv, cm, sm, cd, sdl, sdh)
        kmean = kmean.reshape(N_BLOCKS, MOBA_WIDTH)
        sgu_o = _sgu(xs, g_attn, w_sgu, sgu_ln_g[l], sgu_ln_b[l], sgu_w[l],
                     sgu_b[l].reshape(SGU_GROUPS, SGU_CHUNK, 1))
        moba_o = _moba_attention(qkv, kmean)
        diff_o = _diff_attention(qkv, diff_lambda[l], diff_subln_g[l].reshape(1, HEAD_DIM), lam_init)
        xs = _out_proj(xs, moba_o, diff_o, sgu_o, w_out[l].astype(bf))
        xs = _mlp(xs, mlp_norm_g[l].reshape(1, D_MODEL), w_mlp_in[l].astype(bf),
                  w_mlp_out[l].astype(bf), gf, l == DEPTH - 1)
    return xs.reshape(1, SEQ, D_MODEL)
```

```python
import functools
import math

import jax
import jax.numpy as jnp
from jax import lax
from jax.experimental import pallas as pl
from jax.experimental.pallas import tpu as pltpu

D_MODEL = 2048
SEQ = 8192
DEPTH = 4
HEAD_DIM = 128
MOBA_HEADS = 6
DIFF_HEADS = 6
SGU_GROUPS = 4
MOBA_WIDTH = MOBA_HEADS * HEAD_DIM
DIFF_WIDTH = DIFF_HEADS * HEAD_DIM
SGU_WIDTH = SGU_GROUPS * HEAD_DIM
DIFF_QK_DIM = HEAD_DIM // 2
MOBA_BLOCK = 256
MOBA_TOPK = 3
SGU_CHUNK = 128
D_FF = 4 * D_MODEL
ROPE_THETA = 10000.0
EPS = 1e-6
QKV_WIDTH = 3 * MOBA_WIDTH + 3 * DIFF_WIDTH
N_BLOCKS = SEQ // MOBA_BLOCK

VMEM_LIMIT_BYTES = 56 * 1024 * 1024

PROJ_TM = 512
PROJ_TN = 768
SGU_TM = 512
ATTN_TQ = 2 * MOBA_BLOCK
ATTN_TK = 2 * MOBA_BLOCK
ATTN_ROW_CHUNK = 256
OUT_TM = 512
MLP_TM = 512
MLP_TF = 1024

NEG_INF = float("-inf")
NEG_BIG = -1e30
LOG2E = 1.4426950408889634


def _cparams(sem):
    return pltpu.CompilerParams(dimension_semantics=sem, vmem_limit_bytes=VMEM_LIMIT_BYTES)


def _rms_norm_rows(x, g):
    return x * lax.rsqrt(jnp.mean(x * x, axis=-1, keepdims=True) + EPS) * g


def _gelu_tanh(x):
    c = math.sqrt(2.0 / math.pi)
    return 0.5 * x * (1.0 + jnp.tanh(c * (x + 0.044715 * (x * x * x))))


def _qkv_proj_kernel(x_ref, g_ref, w_ref, cm_ref, sm_ref, cd_ref, sdl_ref, sdh_ref,
                     o_ref, kmean_ref, h_ref):
    j = pl.program_id(1)

    @pl.when(j == 0)
    def _():
        h_ref[...] = _rms_norm_rows(x_ref[...], g_ref[...]).astype(jnp.bfloat16)

    acc = jnp.dot(h_ref[...], w_ref[...], preferred_element_type=jnp.float32)

    def rope_moba(xh):
        return xh * cm_ref[...] + pltpu.roll(xh, HEAD_DIM // 2, 1) * sm_ref[...]

    def rope_diff(xh):
        return (xh * cd_ref[...]
                + pltpu.roll(xh, HEAD_DIM - DIFF_QK_DIM // 2, 1) * sdl_ref[...]
                + pltpu.roll(xh, DIFF_QK_DIM // 2, 1) * sdh_ref[...])

    def heads(fn, with_mean):
        for h in range(PROJ_TN // HEAD_DIM):
            sl = slice(h * HEAD_DIM, (h + 1) * HEAD_DIM)
            r = fn(acc[:, sl])
            o_ref[:, sl] = r.astype(o_ref.dtype)
            if with_mean:
                for b in range(PROJ_TM // MOBA_BLOCK):
                    blk = r[b * MOBA_BLOCK:(b + 1) * MOBA_BLOCK]
                    kmean_ref[0, b:b + 1, sl] = jnp.mean(blk, axis=0, keepdims=True)

    @pl.when(j == 0)
    def _():
        heads(rope_moba, False)

    @pl.when(j == 1)
    def _():
        heads(rope_moba, True)

    @pl.when((j == 2) | (j == 5))
    def _():
        o_ref[...] = acc.astype(o_ref.dtype)

    @pl.when((j == 3) | (j == 4))
    def _():
        heads(rope_diff, False)


def _qkv_proj(x, g, w_qkv, cm, sm, cd, sdl, sdh):
    tm, tn = PROJ_TM, PROJ_TN
    grid = (SEQ // tm, QKV_WIDTH // tn)
    row_tab = pl.BlockSpec((tm, HEAD_DIM), lambda i, j: (i, 0))
    return pl.pallas_call(
        _qkv_proj_kernel,
        grid=grid,
        in_specs=[
            pl.BlockSpec((tm, D_MODEL), lambda i, j: (i, 0)),
            pl.BlockSpec((1, D_MODEL), lambda i, j: (0, 0)),
            pl.BlockSpec((D_MODEL, tn), lambda i, j: (0, j)),
            row_tab, row_tab, row_tab, row_tab, row_tab,
        ],
        out_specs=[
            pl.BlockSpec((tm, tn), lambda i, j: (i, j)),
            pl.BlockSpec((1, tm // MOBA_BLOCK, MOBA_WIDTH), lambda i, j: (i, 0, 0)),
        ],
        out_shape=[
            jax.ShapeDtypeStruct((SEQ, QKV_WIDTH), jnp.bfloat16),
            jax.ShapeDtypeStruct((SEQ // tm, tm // MOBA_BLOCK, MOBA_WIDTH), jnp.float32),
        ],
        scratch_shapes=[pltpu.VMEM((tm, D_MODEL), jnp.bfloat16)],
        compiler_params=_cparams(("arbitrary", "arbitrary")),
        name="qkv_proj",
    )(x, g, w_qkv, cm, sm, cd, sdl, sdh)


def _sgu_kernel(x_ref, g_ref, w_ref, lng_ref, lnb_ref, ws_ref, bs_ref, o_ref):
    h = _rms_norm_rows(x_ref[...], g_ref[...]).astype(jnp.bfloat16)
    uv = jnp.dot(h, w_ref[...], preferred_element_type=jnp.float32)
    row = lax.broadcasted_iota(jnp.int32, (SGU_CHUNK, SGU_CHUNK), 0)
    col = lax.broadcasted_iota(jnp.int32, (SGU_CHUNK, SGU_CHUNK), 1)
    for g in range(SGU_GROUPS):
        sl = slice(g * HEAD_DIM, (g + 1) * HEAD_DIM)
        u = _gelu_tanh(uv[:, sl])
        v = _gelu_tanh(uv[:, SGU_WIDTH + g * HEAD_DIM:SGU_WIDTH + (g + 1) * HEAD_DIM])
        mu = jnp.mean(v, axis=-1, keepdims=True)
        vc = v - mu
        var = jnp.mean(vc * vc, axis=-1, keepdims=True)
        vn = (vc * lax.rsqrt(var + EPS) * lng_ref[g:g + 1, :] + lnb_ref[g:g + 1, :]).astype(jnp.bfloat16)
        w = jnp.where(row >= col, ws_ref[g], 0.0).astype(jnp.bfloat16)
        bias = bs_ref[g]
        for c in range(SGU_TM // SGU_CHUNK):
            rs = slice(c * SGU_CHUNK, (c + 1) * SGU_CHUNK)
            mixed = jnp.dot(w, vn[rs], preferred_element_type=jnp.float32) + bias
            o_ref[rs, sl] = (u[rs] * mixed).astype(o_ref.dtype)


def _sgu(x, g, w_sgu, ln_g, ln_b, w_s, b_s):
    tm = SGU_TM
    full = lambda shape: pl.BlockSpec(shape, lambda i: (0,) * len(shape))
    return pl.pallas_call(
        _sgu_kernel,
        grid=(SEQ // tm,),
        in_specs=[
            pl.BlockSpec((tm, D_MODEL), lambda i: (i, 0)),
            full((1, D_MODEL)),
            full((D_MODEL, 2 * SGU_WIDTH)),
            full((SGU_GROUPS, HEAD_DIM)),
            full((SGU_GROUPS, HEAD_DIM)),
            full((SGU_GROUPS, SGU_CHUNK, SGU_CHUNK)),
            full((SGU_GROUPS, SGU_CHUNK, 1)),
        ],
        out_specs=pl.BlockSpec((tm, SGU_WIDTH), lambda i: (i, 0)),
        out_shape=jax.ShapeDtypeStruct((SEQ, SGU_WIDTH), jnp.bfloat16),
        compiler_params=_cparams(("arbitrary",)),
        name="sgu",
    )(x, g, w_sgu, ln_g, ln_b, w_s, b_s)


def _scores(q, k):
    return lax.dot_general(q, k, (((1,), (1,)), ((), ())), preferred_element_type=jnp.float32)


def _with_ones(v):
    return jnp.concatenate([v, jnp.ones(v.shape, v.dtype)], axis=1)


def _softmax_pv(s_ref, v_ext, m_ref, acc_ref, c, mask_fn=None):
    tk = s_ref.shape[1]
    for r0 in range(0, s_ref.shape[0], ATTN_ROW_CHUNK):
        rows = slice(r0, r0 + ATTN_ROW_CHUNK)
        s = s_ref[rows, :]
        if mask_fn is not None:
            s = mask_fn(s, r0)
        m_prev = m_ref[rows, :]
        m_new = jnp.maximum(m_prev, jnp.max(s, axis=-1, keepdims=True))
        alpha = jnp.exp2((m_prev - m_new) * c)
        p = jnp.exp2((s - pltpu.repeat(m_new, tk // HEAD_DIM, axis=1)) * c).astype(jnp.bfloat16)
        acc_ref[rows, :] = (pltpu.repeat(alpha, 2, axis=1) * acc_ref[rows, :]
                            + jnp.dot(p, v_ext, preferred_element_type=jnp.float32))
        m_ref[rows, :] = m_new


def _flash_pipeline(n_past, qk_own, qk_past, v_tile, own_mask, sa_ref, sb_ref, m_ref, acc_ref, c):
    m_ref[...] = jnp.full(m_ref.shape, NEG_INF, jnp.float32)
    acc_ref[...] = jnp.zeros(acc_ref.shape, jnp.float32)
    qk_own(sa_ref)
    qk_past(0, sb_ref)
    _softmax_pv(sa_ref, v_tile(n_past), m_ref, acc_ref, c, own_mask)

    def pair(jj, carry):
        j = 2 * jj
        qk_past(j + 1, sa_ref)
        _softmax_pv(sb_ref, v_tile(j), m_ref, acc_ref, c)
        qk_past(j + 2, sb_ref)
        _softmax_pv(sa_ref, v_tile(j + 1), m_ref, acc_ref, c)
        return carry

    lax.fori_loop(0, n_past // 2, pair, 0)

    @pl.when(n_past % 2 == 1)
    def _():
        _softmax_pv(sb_ref, v_tile(n_past - 1), m_ref, acc_ref, c)


def _moba_kernel(q_ref, k_ref, v_ref, kmean_ref, o_ref, qa_ref, sa_ref, sb_ref, m_ref, acc_ref):
    i = pl.program_id(1)
    tq, tk = ATTN_TQ, ATTN_TK
    q = q_ref[...]

    gate = lax.dot_general(q.astype(jnp.float32), kmean_ref[...], (((1,), (1,)), ((), ())),
                           preferred_element_type=jnp.float32,
                           precision=lax.Precision.HIGHEST)
    blk = lax.broadcasted_iota(jnp.int32, (tq, HEAD_DIM), 1)
    row = lax.broadcasted_iota(jnp.int32, (tq, HEAD_DIM), 0)
    own = 2 * i + jnp.where(row >= MOBA_BLOCK, 1, 0)
    gate = jnp.where(blk < own, gate, NEG_INF)
    sel = jnp.zeros((tq, HEAD_DIM), jnp.float32)
    for _ in range(MOBA_TOPK):
        mx = jnp.max(gate, axis=-1, keepdims=True)
        cand = jnp.where((gate == mx) & (gate > NEG_INF), blk, HEAD_DIM)
        idx = jnp.min(cand, axis=-1, keepdims=True)
        pick = blk == idx
        sel = jnp.where(pick, 1.0, sel)
        gate = jnp.where(pick, NEG_INF, gate)
    qa_ref[:, :HEAD_DIM] = q
    qa_ref[:, HEAD_DIM:] = jnp.where(sel > 0.0, 0.0, NEG_BIG).astype(jnp.bfloat16)
    picked_first = jnp.sum(jnp.where(blk == 2 * i, sel, 0.0), axis=-1, keepdims=True) > 0.0

    def own_mask(s, r0):
        r = lax.broadcasted_iota(jnp.int32, s.shape, 0) + r0
        col = lax.broadcasted_iota(jnp.int32, s.shape, 1)
        rr = lax.broadcasted_iota(jnp.int32, (s.shape[0], 1), 0) + r0
        first_col = jnp.where(picked_first[r0:r0 + s.shape[0]] | (rr < MOBA_BLOCK), 0, MOBA_BLOCK)
        return jnp.where(col <= r, jnp.where(col >= first_col, s, NEG_INF), NEG_INF)

    erow = lax.broadcasted_iota(jnp.int32, (tk, HEAD_DIM), 0)
    elane = lax.broadcasted_iota(jnp.int32, (tk, HEAD_DIM), 1)
    ehalf = jnp.where(erow >= MOBA_BLOCK, 1, 0)

    def qk_own(s_ref):
        s_ref[...] = _scores(qa_ref[:, :HEAD_DIM], k_ref[pl.ds(pl.multiple_of(i * tk, tk), tk), :])

    def qk_past(j, s_ref):
        onehot = jnp.where(elane == 2 * j + ehalf, 1.0, 0.0).astype(jnp.bfloat16)
        ka = jnp.concatenate([k_ref[pl.ds(pl.multiple_of(j * tk, tk), tk), :], onehot], axis=1)
        s_ref[...] = _scores(qa_ref[...], ka)

    def v_tile(j):
        return _with_ones(v_ref[pl.ds(pl.multiple_of(j * tk, tk), tk), :])

    _flash_pipeline(i, qk_own, qk_past, v_tile, own_mask, sa_ref, sb_ref, m_ref, acc_ref,
                    (HEAD_DIM ** -0.5) * LOG2E)
    acc = acc_ref[...]
    o_ref[...] = (acc[:, :HEAD_DIM] / acc[:, HEAD_DIM:]).astype(o_ref.dtype)


def _moba_attention(qkv, kmean_pad):
    tq, tk = ATTN_TQ, ATTN_TK
    return pl.pallas_call(
        _moba_kernel,
        grid=(MOBA_HEADS, SEQ // tq),
        in_specs=[
            pl.BlockSpec((tq, HEAD_DIM), lambda h, i: (i, h)),
            pl.BlockSpec((SEQ, HEAD_DIM), lambda h, i: (0, MOBA_HEADS + h)),
            pl.BlockSpec((SEQ, HEAD_DIM), lambda h, i: (0, 2 * MOBA_HEADS + h)),
            pl.BlockSpec((HEAD_DIM, HEAD_DIM), lambda h, i: (0, h)),
        ],
        out_specs=pl.BlockSpec((tq, HEAD_DIM), lambda h, i: (i, h)),
        out_shape=jax.ShapeDtypeStruct((SEQ, MOBA_WIDTH), jnp.bfloat16),
        scratch_shapes=[
            pltpu.VMEM((tq, 2 * HEAD_DIM), jnp.bfloat16),
            pltpu.VMEM((tq, tk), jnp.float32),
            pltpu.VMEM((tq, tk), jnp.float32),
            pltpu.VMEM((tq, HEAD_DIM), jnp.float32),
            pltpu.VMEM((tq, 2 * HEAD_DIM), jnp.float32),
        ],
        compiler_params=_cparams(("arbitrary", "arbitrary")),
        name="moba_attention",
    )(qkv, qkv, qkv, kmean_pad)


def _diff_kernel(lam_init, q_ref, k_ref, v_ref, lp_ref, g_ref, o_ref, q2_ref, sa_ref, sb_ref, m_ref, acc_ref):
    i = pl.program_id(1)
    tq, tk = ATTN_TQ, ATTN_TK
    q = q_ref[...] * (DIFF_QK_DIM ** -0.5)
    lane = lax.broadcasted_iota(jnp.int32, (tq, HEAD_DIM), 1)
    zero = jnp.zeros_like(q)
    q2_ref[:tq, :] = jnp.where(lane < DIFF_QK_DIM, q, zero)
    q2_ref[tq:, :] = jnp.where(lane >= DIFF_QK_DIM, q, zero)

    def causal_mask(s, r0):
        r = lax.broadcasted_iota(jnp.int32, s.shape, 0) + (r0 % tq)
        col = lax.broadcasted_iota(jnp.int32, s.shape, 1)
        return jnp.where(col <= r, s, NEG_INF)

    def qk_past(j, s_ref):
        s_ref[...] = _scores(q2_ref[...], k_ref[pl.ds(pl.multiple_of(j * tk, tk), tk), :])

    def v_tile(j):
        return _with_ones(v_ref[pl.ds(pl.multiple_of(j * tk, tk), tk), :])

    _flash_pipeline(i, functools.partial(qk_past, i), qk_past, v_tile, causal_mask,
                    sa_ref, sb_ref, m_ref, acc_ref, LOG2E)

    lp = lp_ref[...]
    lam = (jnp.exp(jnp.sum(lp[0:1] * lp[1:2], axis=-1, keepdims=True))
           - jnp.exp(jnp.sum(lp[2:3] * lp[3:4], axis=-1, keepdims=True)) + lam_init)
    acc = acc_ref[...]
    o = acc[:, :HEAD_DIM] / acc[:, HEAD_DIM:]
    o = o[:tq] - lam * o[tq:]
    o = _rms_norm_rows(o, g_ref[...]) * (1.0 - lam_init)
    o_ref[...] = o.astype(o_ref.dtype)


def _diff_attention(qkv, lam_params, subln_g, lam_init):
    tq, tk = ATTN_TQ, ATTN_TK
    base = 3 * MOBA_HEADS
    return pl.pallas_call(
        functools.partial(_diff_kernel, lam_init),
        grid=(DIFF_HEADS, SEQ // tq),
        in_specs=[
            pl.BlockSpec((tq, HEAD_DIM), lambda h, i: (i, base + h)),
            pl.BlockSpec((SEQ, HEAD_DIM), lambda h, i: (0, base + DIFF_HEADS + h)),
            pl.BlockSpec((SEQ, HEAD_DIM), lambda h, i: (0, base + 2 * DIFF_HEADS + h)),
            pl.BlockSpec((4, DIFF_QK_DIM), lambda h, i: (0, 0)),
            pl.BlockSpec((1, HEAD_DIM), lambda h, i: (0, 0)),
        ],
        out_specs=pl.BlockSpec((tq, HEAD_DIM), lambda h, i: (i, h)),
        out_shape=jax.ShapeDtypeStruct((SEQ, DIFF_WIDTH), jnp.bfloat16),
        scratch_shapes=[
            pltpu.VMEM((2 * tq, HEAD_DIM), jnp.bfloat16),
            pltpu.VMEM((2 * tq, tk), jnp.float32),
            pltpu.VMEM((2 * tq, tk), jnp.float32),
            pltpu.VMEM((2 * tq, HEAD_DIM), jnp.float32),
            pltpu.VMEM((2 * tq, 2 * HEAD_DIM), jnp.float32),
        ],
        compiler_params=_cparams(("arbitrary", "arbitrary")),
        name="diff_attention",
    )(qkv, qkv, qkv, lam_params, subln_g)


def _out_proj_kernel(x_ref, a_ref, b_ref, c_ref, w_ref, o_ref):
    acc = jnp.dot(a_ref[...], w_ref[0:MOBA_WIDTH, :], preferred_element_type=jnp.float32)
    acc += jnp.dot(b_ref[...], w_ref[MOBA_WIDTH:MOBA_WIDTH + DIFF_WIDTH, :],
                   preferred_element_type=jnp.float32)
    acc += jnp.dot(c_ref[...], w_ref[MOBA_WIDTH + DIFF_WIDTH:, :], preferred_element_type=jnp.float32)
    o_ref[...] = x_ref[...] + acc


def _out_proj(x, moba_o, diff_o, sgu_o, w_out):
    tm = OUT_TM
    return pl.pallas_call(
        _out_proj_kernel,
        grid=(SEQ // tm,),
        in_specs=[
            pl.BlockSpec((tm, D_MODEL), lambda i: (i, 0)),
            pl.BlockSpec((tm, MOBA_WIDTH), lambda i: (i, 0)),
            pl.BlockSpec((tm, DIFF_WIDTH), lambda i: (i, 0)),
            pl.BlockSpec((tm, SGU_WIDTH), lambda i: (i, 0)),
            pl.BlockSpec((D_MODEL, D_MODEL), lambda i: (0, 0)),
        ],
        out_specs=pl.BlockSpec((tm, D_MODEL), lambda i: (i, 0)),
        out_shape=jax.ShapeDtypeStruct((SEQ, D_MODEL), jnp.float32),
        compiler_params=_cparams(("arbitrary",)),
        name="out_proj",
    )(x, moba_o, diff_o, sgu_o, w_out)


def _mlp_kernel(final, x_ref, g_ref, w1_ref, w2_ref, gf_ref, o_ref, h_ref, acc_ref):
    j = pl.program_id(1)

    @pl.when(j == 0)
    def _():
        h_ref[...] = _rms_norm_rows(x_ref[...], g_ref[...]).astype(jnp.bfloat16)
        acc_ref[...] = jnp.zeros(acc_ref.shape, jnp.float32)

    a = jnp.dot(h_ref[...], w1_ref[...], preferred_element_type=jnp.float32)
    a = jnp.square(jnp.maximum(a, 0.0)).astype(jnp.bfloat16)
    acc_ref[...] += jnp.dot(a, w2_ref[...], preferred_element_type=jnp.float32)

    @pl.when(j == pl.num_programs(1) - 1)
    def _():
        y = x_ref[...] + acc_ref[...]
        if final:
            y = _rms_norm_rows(y, gf_ref[...])
        o_ref[...] = y


def _mlp(x, g, w1, w2, g_final, final):
    tm, tf = MLP_TM, MLP_TF
    return pl.pallas_call(
        functools.partial(_mlp_kernel, final),
        grid=(SEQ // tm, D_FF // tf),
        in_specs=[
            pl.BlockSpec((tm, D_MODEL), lambda i, j: (i, 0)),
            pl.BlockSpec((1, D_MODEL), lambda i, j: (0, 0)),
            pl.BlockSpec((D_MODEL, tf), lambda i, j: (0, j)),
            pl.BlockSpec((tf, D_MODEL), lambda i, j: (j, 0)),
            pl.BlockSpec((1, D_MODEL), lambda i, j: (0, 0)),
        ],
        out_specs=pl.BlockSpec((tm, D_MODEL), lambda i, j: (i, 0)),
        out_shape=jax.ShapeDtypeStruct((SEQ, D_MODEL), jnp.float32),
        scratch_shapes=[
            pltpu.VMEM((tm, D_MODEL), jnp.bfloat16),
            pltpu.VMEM((tm, D_MODEL), jnp.float32),
        ],
        compiler_params=_cparams(("arbitrary", "arbitrary")),
        name="mlp",
    )(x, g, w1, w2, g_final)


def _rope_tables():
    pos = jnp.arange(SEQ, dtype=jnp.float32)[:, None]

    def tab(dim):
        inv = 1.0 / (ROPE_THETA ** (jnp.arange(0, dim, 2, dtype=jnp.float32) / dim))
        ang = pos * inv[None, :]
        return jnp.cos(ang), jnp.sin(ang)

    cm, sm = tab(HEAD_DIM)
    cd, sd = tab(DIFF_QK_DIM)
    zd = jnp.zeros_like(sd)
    return (jnp.concatenate([cm, cm], axis=1), jnp.concatenate([-sm, sm], axis=1),
            jnp.concatenate([cd, cd, cd, cd], axis=1),
            jnp.concatenate([-sd, zd, -sd, zd], axis=1),
            jnp.concatenate([zd, sd, zd, sd], axis=1))


def kernel(x, attn_norm_g, w_in, diff_lambda, diff_subln_g, sgu_ln_g, sgu_ln_b, sgu_w, sgu_b,
           w_out, mlp_norm_g, w_mlp_in, w_mlp_out, final_norm_g):
    assert x.shape == (1, SEQ, D_MODEL)
    bf = jnp.bfloat16
    xs = x.reshape(SEQ, D_MODEL)
    cm, sm, cd, sdl, sdh = _rope_tables()
    gf = final_norm_g.reshape(1, D_MODEL)
    for l in range(DEPTH):
        lam_init = 0.8 - 0.6 * math.exp(-0.3 * l)
        g_attn = attn_norm_g[l].reshape(1, D_MODEL)
        w_qkv = w_in[l, :, :QKV_WIDTH].astype(bf)
        w_sgu = w_in[l, :, QKV_WIDTH:].astype(bf)
        qkv, kmean = _qkv_proj(xs, g_attn, w_qkv, cm, sm, cd, sdl, sdh)
        kmean = jnp.pad(kmean.reshape(N_BLOCKS, MOBA_WIDTH), ((0, HEAD_DIM - N_BLOCKS), (0, 0)))
        sgu_o = _sgu(xs, g_attn, w_sgu, sgu_ln_g[l], sgu_ln_b[l], sgu_w[l],
                     sgu_b[l].reshape(SGU_GROUPS, SGU_CHUNK, 1))
        moba_o = _moba_attention(qkv, kmean)
        diff_o = _diff_attention(qkv, diff_lambda[l], diff_subln_g[l].reshape(1, HEAD_DIM), lam_init)
        xs = _out_proj(xs, moba_o, diff_o, sgu_o, w_out[l].astype(bf))
        xs = _mlp(xs, mlp_norm_g[l].reshape(1, D_MODEL), w_mlp_in[l].astype(bf),
                  w_mlp_out[l].astype(bf), gf, l == DEPTH - 1)
    return xs.reshape(1, SEQ, D_MODEL)
```

```python
import functools
import math
from typing import Callable, NamedTuple

import jax
import jax.numpy as jnp
from jax import lax
from jax.experimental import pallas as pl
from jax.experimental.pallas import tpu as pltpu

D_MODEL = 2048
SEQ = 8192
DEPTH = 4
HEAD_DIM = 128
MOBA_HEADS = 6
DIFF_HEADS = 6
SGU_GROUPS = 4
MOBA_WIDTH = MOBA_HEADS * HEAD_DIM
DIFF_WIDTH = DIFF_HEADS * HEAD_DIM
SGU_WIDTH = SGU_GROUPS * HEAD_DIM
DIFF_QK_DIM = HEAD_DIM // 2
MOBA_BLOCK = 256
MOBA_TOPK = 3
SGU_CHUNK = 128
D_FF = 4 * D_MODEL
ROPE_THETA = 10000.0
EPS = 1e-6
QKV_WIDTH = 3 * MOBA_WIDTH + 3 * DIFF_WIDTH
N_BLOCKS = SEQ // MOBA_BLOCK

VMEM_LIMIT_BYTES = 56 * 1024 * 1024

PROJ_TM = 512
PROJ_TN = 768
SGU_TM = 512
ATTN_TQ = 2 * MOBA_BLOCK
ATTN_TK = 2 * MOBA_BLOCK
ATTN_HEADS_PER_STEP = 2
OUT_TM = 512
MLP_TM = 512
MLP_TF = 1024

NEG_INF = float("-inf")
NEG_BIG = -1e30
LOG2E = 1.4426950408889634


def _cparams(sem):
    return pltpu.CompilerParams(dimension_semantics=sem, vmem_limit_bytes=VMEM_LIMIT_BYTES)


def _rms_norm_rows(x, g):
    return x * lax.rsqrt(jnp.mean(x * x, axis=-1, keepdims=True) + EPS) * g


def _gelu_tanh(x):
    c = math.sqrt(2.0 / math.pi)
    return 0.5 * x * (1.0 + jnp.tanh(c * (x + 0.044715 * (x * x * x))))


def _qkv_proj_kernel(x_ref, g_ref, w_ref, cm_ref, sm_ref, cd_ref, sdl_ref, sdh_ref,
                     o_ref, kmean_ref, h_ref):
    j = pl.program_id(1)

    @pl.when(j == 0)
    def _():
        h_ref[...] = _rms_norm_rows(x_ref[...], g_ref[...]).astype(jnp.bfloat16)

    acc = jnp.dot(h_ref[...], w_ref[...], preferred_element_type=jnp.float32)

    def rope_moba(xh):
        return xh * cm_ref[...] + pltpu.roll(xh, HEAD_DIM // 2, 1) * sm_ref[...]

    def rope_diff(xh):
        return (xh * cd_ref[...]
                + pltpu.roll(xh, HEAD_DIM - DIFF_QK_DIM // 2, 1) * sdl_ref[...]
                + pltpu.roll(xh, DIFF_QK_DIM // 2, 1) * sdh_ref[...])

    def heads(fn, with_mean):
        for h in range(PROJ_TN // HEAD_DIM):
            sl = slice(h * HEAD_DIM, (h + 1) * HEAD_DIM)
            r = fn(acc[:, sl])
            o_ref[:, sl] = r.astype(o_ref.dtype)
            if with_mean:
                for b in range(PROJ_TM // MOBA_BLOCK):
                    blk = r[b * MOBA_BLOCK:(b + 1) * MOBA_BLOCK]
                    kmean_ref[0, b:b + 1, sl] = jnp.mean(blk, axis=0, keepdims=True)

    @pl.when(j == 0)
    def _():
        heads(rope_moba, False)

    @pl.when(j == 1)
    def _():
        heads(rope_moba, True)

    @pl.when((j == 2) | (j == 5))
    def _():
        o_ref[...] = acc.astype(o_ref.dtype)

    @pl.when((j == 3) | (j == 4))
    def _():
        heads(rope_diff, False)


def _qkv_proj(x, g, w_qkv, cm, sm, cd, sdl, sdh):
    tm, tn = PROJ_TM, PROJ_TN
    grid = (SEQ // tm, QKV_WIDTH // tn)
    row_tab = pl.BlockSpec((tm, HEAD_DIM), lambda i, j: (i, 0))
    return pl.pallas_call(
        _qkv_proj_kernel,
        grid=grid,
        in_specs=[
            pl.BlockSpec((tm, D_MODEL), lambda i, j: (i, 0)),
            pl.BlockSpec((1, D_MODEL), lambda i, j: (0, 0)),
            pl.BlockSpec((D_MODEL, tn), lambda i, j: (0, j)),
            row_tab, row_tab, row_tab, row_tab, row_tab,
        ],
        out_specs=[
            pl.BlockSpec((tm, tn), lambda i, j: (i, j)),
            pl.BlockSpec((1, tm // MOBA_BLOCK, MOBA_WIDTH), lambda i, j: (i, 0, 0)),
        ],
        out_shape=[
            jax.ShapeDtypeStruct((SEQ, QKV_WIDTH), jnp.bfloat16),
            jax.ShapeDtypeStruct((SEQ // tm, tm // MOBA_BLOCK, MOBA_WIDTH), jnp.float32),
        ],
        scratch_shapes=[pltpu.VMEM((tm, D_MODEL), jnp.bfloat16)],
        compiler_params=_cparams(("arbitrary", "arbitrary")),
        name="qkv_proj",
    )(x, g, w_qkv, cm, sm, cd, sdl, sdh)


def _sgu_kernel(x_ref, g_ref, w_ref, lng_ref, lnb_ref, ws_ref, bs_ref, o_ref):
    h = _rms_norm_rows(x_ref[...], g_ref[...]).astype(jnp.bfloat16)
    uv = jnp.dot(h, w_ref[...], preferred_element_type=jnp.float32)
    row = lax.broadcasted_iota(jnp.int32, (SGU_CHUNK, SGU_CHUNK), 0)
    col = lax.broadcasted_iota(jnp.int32, (SGU_CHUNK, SGU_CHUNK), 1)
    for g in range(SGU_GROUPS):
        sl = slice(g * HEAD_DIM, (g + 1) * HEAD_DIM)
        u = _gelu_tanh(uv[:, sl])
        v = _gelu_tanh(uv[:, SGU_WIDTH + g * HEAD_DIM:SGU_WIDTH + (g + 1) * HEAD_DIM])
        mu = jnp.mean(v, axis=-1, keepdims=True)
        vc = v - mu
        var = jnp.mean(vc * vc, axis=-1, keepdims=True)
        vn = (vc * lax.rsqrt(var + EPS) * lng_ref[g:g + 1, :] + lnb_ref[g:g + 1, :]).astype(jnp.bfloat16)
        w = jnp.where(row >= col, ws_ref[g], 0.0).astype(jnp.bfloat16)
        bias = bs_ref[g]
        for c in range(SGU_TM // SGU_CHUNK):
            rs = slice(c * SGU_CHUNK, (c + 1) * SGU_CHUNK)
            mixed = jnp.dot(w, vn[rs], preferred_element_type=jnp.float32) + bias
            o_ref[rs, sl] = (u[rs] * mixed).astype(o_ref.dtype)


def _sgu(x, g, w_sgu, ln_g, ln_b, w_s, b_s):
    tm = SGU_TM
    full = lambda shape: pl.BlockSpec(shape, lambda i: (0,) * len(shape))
    return pl.pallas_call(
        _sgu_kernel,
        grid=(SEQ // tm,),
        in_specs=[
            pl.BlockSpec((tm, D_MODEL), lambda i: (i, 0)),
            full((1, D_MODEL)),
            full((D_MODEL, 2 * SGU_WIDTH)),
            full((SGU_GROUPS, HEAD_DIM)),
            full((SGU_GROUPS, HEAD_DIM)),
            full((SGU_GROUPS, SGU_CHUNK, SGU_CHUNK)),
            full((SGU_GROUPS, SGU_CHUNK, 1)),
        ],
        out_specs=pl.BlockSpec((tm, SGU_WIDTH), lambda i: (i, 0)),
        out_shape=jax.ShapeDtypeStruct((SEQ, SGU_WIDTH), jnp.bfloat16),
        compiler_params=_cparams(("arbitrary",)),
        name="sgu",
    )(x, g, w_sgu, ln_g, ln_b, w_s, b_s)


class _Stream(NamedTuple):
    qk_own: Callable
    qk_past: Callable
    v_tile: Callable
    own_mask: Callable
    sa_ref: object
    sb_ref: object
    m_ref: object
    acc_ref: object
    c: float


def _scores(q, k):
    return lax.dot_general(q, k, (((1,), (1,)), ((), ())), preferred_element_type=jnp.float32)


def _with_ones(v):
    return jnp.concatenate([v, jnp.ones(v.shape, v.dtype)], axis=1)


def _lane_tile(x, n):
    return jnp.concatenate([x] * n, axis=1)


def _softmax_pv(st, s_ref, j, mask_fn=None):
    s = s_ref[...]
    if mask_fn is not None:
        s = mask_fn(s)
    m_prev = st.m_ref[...]
    m_new = jnp.maximum(m_prev, jnp.max(s, axis=-1, keepdims=True))
    alpha = jnp.exp2((m_prev - m_new) * st.c)
    p = jnp.exp2((s - _lane_tile(m_new, s.shape[1] // HEAD_DIM)) * st.c).astype(jnp.bfloat16)
    st.acc_ref[...] = (_lane_tile(alpha, 2) * st.acc_ref[...]
                       + jnp.dot(p, st.v_tile(j), preferred_element_type=jnp.float32))
    st.m_ref[...] = m_new


def _flash_pipeline(n_past, streams):
    for st in streams:
        st.m_ref[...] = jnp.full(st.m_ref.shape, NEG_INF, jnp.float32)
        st.acc_ref[...] = jnp.zeros(st.acc_ref.shape, jnp.float32)
        st.qk_own(st.sa_ref)
    for st in streams:
        st.qk_past(0, st.sb_ref)
        _softmax_pv(st, st.sa_ref, n_past, st.own_mask)

    def pair(jj, carry):
        j = 2 * jj
        for st in streams:
            st.qk_past(j + 1, st.sa_ref)
            _softmax_pv(st, st.sb_ref, j)
        for st in streams:
            st.qk_past(j + 2, st.sb_ref)
            _softmax_pv(st, st.sa_ref, j + 1)
        return carry

    lax.fori_loop(0, n_past // 2, pair, 0)

    @pl.when(n_past % 2 == 1)
    def _():
        for st in streams:
            _softmax_pv(st, st.sb_ref, n_past - 1)


def _moba_stream(i, q, k_ref, v_ref, kmean, qa_ref, sa_ref, sb_ref, m_ref, acc_ref):
    tq, tk = ATTN_TQ, ATTN_TK
    gate = lax.dot_general(q.astype(jnp.float32), kmean, (((1,), (1,)), ((), ())),
                           preferred_element_type=jnp.float32,
                           precision=lax.Precision.HIGHEST)
    blk = lax.broadcasted_iota(jnp.int32, (tq, HEAD_DIM), 1)
    row = lax.broadcasted_iota(jnp.int32, (tq, HEAD_DIM), 0)
    own = 2 * i + jnp.where(row >= MOBA_BLOCK, 1, 0)
    gate = jnp.where(blk < own, gate, NEG_INF)
    blk_f = blk.astype(jnp.float32)
    sel = jnp.zeros((tq, HEAD_DIM), jnp.float32)
    for _ in range(MOBA_TOPK):
        mx = jnp.max(gate, axis=-1, keepdims=True)
        cand = jnp.where((gate == mx) & (gate > NEG_INF), blk_f, float(HEAD_DIM))
        pick = blk_f == jnp.min(cand, axis=-1, keepdims=True)
        sel = jnp.where(pick, 1.0, sel)
        gate = jnp.where(pick, NEG_INF, gate)
    qa_ref[:, :HEAD_DIM] = q
    qa_ref[:, HEAD_DIM:] = jnp.where(sel > 0.0, 0.0, NEG_BIG).astype(jnp.bfloat16)
    picked_first = jnp.sum(jnp.where(blk == 2 * i, sel, 0.0), axis=-1, keepdims=True) > 0.0

    def own_mask(s):
        r = lax.broadcasted_iota(jnp.int32, s.shape, 0)
        col = lax.broadcasted_iota(jnp.int32, s.shape, 1)
        rr = lax.broadcasted_iota(jnp.int32, (tq, 1), 0)
        first_col = jnp.where(picked_first | (rr < MOBA_BLOCK), 0, MOBA_BLOCK)
        return jnp.where(col <= r, jnp.where(col >= first_col, s, NEG_INF), NEG_INF)

    def qk_own(s_ref):
        s_ref[...] = _scores(qa_ref[:, :HEAD_DIM], k_ref[pl.ds(pl.multiple_of(i * tk, tk), tk), :])

    def qk_past(j, s_ref):
        erow = lax.broadcasted_iota(jnp.int32, (tk, HEAD_DIM), 0)
        elane = lax.broadcasted_iota(jnp.int32, (tk, HEAD_DIM), 1)
        onehot = jnp.where(elane == 2 * j + jnp.where(erow >= MOBA_BLOCK, 1, 0), 1.0, 0.0).astype(jnp.bfloat16)
        ka = jnp.concatenate([k_ref[pl.ds(pl.multiple_of(j * tk, tk), tk), :], onehot], axis=1)
        s_ref[...] = _scores(qa_ref[...], ka)

    def v_tile(j):
        return _with_ones(v_ref[pl.ds(pl.multiple_of(j * tk, tk), tk), :])

    return _Stream(qk_own, qk_past, v_tile, own_mask, sa_ref, sb_ref, m_ref, acc_ref,
                   (HEAD_DIM ** -0.5) * LOG2E)


def _moba_kernel(q_ref, k_ref, v_ref, kmean_ref, o_ref, qa_ref, sa_ref, sb_ref, m_ref, acc_ref):
    i = pl.program_id(1)
    streams = []
    for h in range(ATTN_HEADS_PER_STEP):
        sl = slice(h * HEAD_DIM, (h + 1) * HEAD_DIM)
        streams.append(_moba_stream(i, q_ref[:, sl], k_ref.at[:, sl], v_ref.at[:, sl], kmean_ref[:, sl],
                                    qa_ref.at[h], sa_ref.at[h], sb_ref.at[h], m_ref.at[h], acc_ref.at[h]))
    _flash_pipeline(i, streams)
    for h in range(ATTN_HEADS_PER_STEP):
        acc = acc_ref[h]
        o_ref[:, h * HEAD_DIM:(h + 1) * HEAD_DIM] = (acc[:, :HEAD_DIM] / acc[:, HEAD_DIM:]).astype(o_ref.dtype)


def _moba_attention(qkv, kmean_pad):
    tq, tk, hp = ATTN_TQ, ATTN_TK, ATTN_HEADS_PER_STEP
    w = hp * HEAD_DIM
    groups = MOBA_HEADS // hp
    return pl.pallas_call(
        _moba_kernel,
        grid=(groups, SEQ // tq),
        in_specs=[
            pl.BlockSpec((tq, w), lambda h, i: (i, h)),
            pl.BlockSpec((SEQ, w), lambda h, i: (0, groups + h)),
            pl.BlockSpec((SEQ, w), lambda h, i: (0, 2 * groups + h)),
            pl.BlockSpec((HEAD_DIM, w), lambda h, i: (0, h)),
        ],
        out_specs=pl.BlockSpec((tq, w), lambda h, i: (i, h)),
        out_shape=jax.ShapeDtypeStruct((SEQ, MOBA_WIDTH), jnp.bfloat16),
        scratch_shapes=[
            pltpu.VMEM((hp, tq, 2 * HEAD_DIM), jnp.bfloat16),
            pltpu.VMEM((hp, tq, tk), jnp.float32),
            pltpu.VMEM((hp, tq, tk), jnp.float32),
            pltpu.VMEM((hp, tq, HEAD_DIM), jnp.float32),
            pltpu.VMEM((hp, tq, 2 * HEAD_DIM), jnp.float32),
        ],
        compiler_params=_cparams(("arbitrary", "arbitrary")),
        name="moba_attention",
    )(qkv, qkv, qkv, kmean_pad)


def _diff_stream(i, q, k_ref, v_ref, q2_ref, sa_ref, sb_ref, m_ref, acc_ref):
    tq, tk = ATTN_TQ, ATTN_TK
    q = q * (DIFF_QK_DIM ** -0.5)
    lane = lax.broadcasted_iota(jnp.int32, (tq, HEAD_DIM), 1)
    zero = jnp.zeros_like(q)
    q2_ref[:tq, :] = jnp.where(lane < DIFF_QK_DIM, q, zero)
    q2_ref[tq:, :] = jnp.where(lane >= DIFF_QK_DIM, q, zero)

    def causal_mask(s):
        r = lax.broadcasted_iota(jnp.int32, s.shape, 0)
        r = jnp.where(r >= tq, r - tq, r)
        col = lax.broadcasted_iota(jnp.int32, s.shape, 1)
        return jnp.where(col <= r, s, NEG_INF)

    def qk_past(j, s_ref):
        s_ref[...] = _scores(q2_ref[...], k_ref[pl.ds(pl.multiple_of(j * tk, tk), tk), :])

    def v_tile(j):
        return _with_ones(v_ref[pl.ds(pl.multiple_of(j * tk, tk), tk), :])

    return _Stream(functools.partial(qk_past, i), qk_past, v_tile, causal_mask,
                   sa_ref, sb_ref, m_ref, acc_ref, LOG2E)


def _diff_kernel(lam_init, q_ref, k_ref, v_ref, lp_ref, g_ref, o_ref, q2_ref, sa_ref, sb_ref, m_ref, acc_ref):
    i = pl.program_id(1)
    tq = ATTN_TQ
    streams = []
    for h in range(ATTN_HEADS_PER_STEP):
        sl = slice(h * HEAD_DIM, (h + 1) * HEAD_DIM)
        streams.append(_diff_stream(i, q_ref[:, sl], k_ref.at[:, sl], v_ref.at[:, sl],
                                    q2_ref.at[h], sa_ref.at[h], sb_ref.at[h], m_ref.at[h], acc_ref.at[h]))
    _flash_pipeline(i, streams)

    lp = lp_ref[...]
    lam = (jnp.exp(jnp.sum(lp[0:1] * lp[1:2], axis=-1, keepdims=True))
           - jnp.exp(jnp.sum(lp[2:3] * lp[3:4], axis=-1, keepdims=True)) + lam_init)
    for h in range(ATTN_HEADS_PER_STEP):
        acc = acc_ref[h]
        o = acc[:, :HEAD_DIM] / acc[:, HEAD_DIM:]
        o = o[:tq] - lam * o[tq:]
        o = _rms_norm_rows(o, g_ref[...]) * (1.0 - lam_init)
        o_ref[:, h * HEAD_DIM:(h + 1) * HEAD_DIM] = o.astype(o_ref.dtype)


def _diff_attention(qkv, lam_params, subln_g, lam_init):
    tq, tk, hp = ATTN_TQ, ATTN_TK, ATTN_HEADS_PER_STEP
    w = hp * HEAD_DIM
    groups = DIFF_HEADS // hp
    base = 3 * MOBA_WIDTH // w
    return pl.pallas_call(
        functools.partial(_diff_kernel, lam_init),
        grid=(groups, SEQ // tq),
        in_specs=[
            pl.BlockSpec((tq, w), lambda h, i: (i, base + h)),
            pl.BlockSpec((SEQ, w), lambda h, i: (0, base + groups + h)),
            pl.BlockSpec((SEQ, w), lambda h, i: (0, base + 2 * groups + h)),
            pl.BlockSpec((4, DIFF_QK_DIM), lambda h, i: (0, 0)),
            pl.BlockSpec((1, HEAD_DIM), lambda h, i: (0, 0)),
        ],
        out_specs=pl.BlockSpec((tq, w), lambda h, i: (i, h)),
        out_shape=jax.ShapeDtypeStruct((SEQ, DIFF_WIDTH), jnp.bfloat16),
        scratch_shapes=[
            pltpu.VMEM((hp, 2 * tq, HEAD_DIM), jnp.bfloat16),
            pltpu.VMEM((hp, 2 * tq, tk), jnp.float32),
            pltpu.VMEM((hp, 2 * tq, tk), jnp.float32),
            pltpu.VMEM((hp, 2 * tq, HEAD_DIM), jnp.float32),
            pltpu.VMEM((hp, 2 * tq, 2 * HEAD_DIM), jnp.float32),
        ],
        compiler_params=_cparams(("arbitrary", "arbitrary")),
        name="diff_attention",
    )(qkv, qkv, qkv, lam_params, subln_g)


def _out_proj_kernel(x_ref, a_ref, b_ref, c_ref, w_ref, o_ref):
    acc = jnp.dot(a_ref[...], w_ref[0:MOBA_WIDTH, :], preferred_element_type=jnp.float32)
    acc += jnp.dot(b_ref[...], w_ref[MOBA_WIDTH:MOBA_WIDTH + DIFF_WIDTH, :],
                   preferred_element_type=jnp.float32)
    acc += jnp.dot(c_ref[...], w_ref[MOBA_WIDTH + DIFF_WIDTH:, :], preferred_element_type=jnp.float32)
    o_ref[...] = x_ref[...] + acc


def _out_proj(x, moba_o, diff_o, sgu_o, w_out):
    tm = OUT_TM
    return pl.pallas_call(
        _out_proj_kernel,
        grid=(SEQ // tm,),
        in_specs=[
            pl.BlockSpec((tm, D_MODEL), lambda i: (i, 0)),
            pl.BlockSpec((tm, MOBA_WIDTH), lambda i: (i, 0)),
            pl.BlockSpec((tm, DIFF_WIDTH), lambda i: (i, 0)),
            pl.BlockSpec((tm, SGU_WIDTH), lambda i: (i, 0)),
            pl.BlockSpec((D_MODEL, D_MODEL), lambda i: (0, 0)),
        ],
        out_specs=pl.BlockSpec((tm, D_MODEL), lambda i: (i, 0)),
        out_shape=jax.ShapeDtypeStruct((SEQ, D_MODEL), jnp.float32),
        compiler_params=_cparams(("arbitrary",)),
        name="out_proj",
    )(x, moba_o, diff_o, sgu_o, w_out)


def _mlp_kernel(final, x_ref, g_ref, w1_ref, w2_ref, gf_ref, o_ref, h_ref, acc_ref):
    j = pl.program_id(1)

    @pl.when(j == 0)
    def _():
        h_ref[...] = _rms_norm_rows(x_ref[...], g_ref[...]).astype(jnp.bfloat16)
        acc_ref[...] = jnp.zeros(acc_ref.shape, jnp.float32)

    a = jnp.dot(h_ref[...], w1_ref[...], preferred_element_type=jnp.float32)
    a = jnp.square(jnp.maximum(a, 0.0)).astype(jnp.bfloat16)
    acc_ref[...] += jnp.dot(a, w2_ref[...], preferred_element_type=jnp.float32)

    @pl.when(j == pl.num_programs(1) - 1)
    def _():
        y = x_ref[...] + acc_ref[...]
        if final:
            y = _rms_norm_rows(y, gf_ref[...])
        o_ref[...] = y


def _mlp(x, g, w1, w2, g_final, final):
    tm, tf = MLP_TM, MLP_TF
    return pl.pallas_call(
        functools.partial(_mlp_kernel, final),
        grid=(SEQ // tm, D_FF // tf),
        in_specs=[
            pl.BlockSpec((tm, D_MODEL), lambda i, j: (i, 0)),
            pl.BlockSpec((1, D_MODEL), lambda i, j: (0, 0)),
            pl.BlockSpec((D_MODEL, tf), lambda i, j: (0, j)),
            pl.BlockSpec((tf, D_MODEL), lambda i, j: (j, 0)),
            pl.BlockSpec((1, D_MODEL), lambda i, j: (0, 0)),
        ],
        out_specs=pl.BlockSpec((tm, D_MODEL), lambda i, j: (i, 0)),
        out_shape=jax.ShapeDtypeStruct((SEQ, D_MODEL), jnp.float32),
        scratch_shapes=[
            pltpu.VMEM((tm, D_MODEL), jnp.bfloat16),
            pltpu.VMEM((tm, D_MODEL), jnp.float32),
        ],
        compiler_params=_cparams(("arbitrary", "arbitrary")),
        name="mlp",
    )(x, g, w1, w2, g_final)


def _rope_tables():
    pos = jnp.arange(SEQ, dtype=jnp.float32)[:, None]

    def tab(dim):
        inv = 1.0 / (ROPE_THETA ** (jnp.arange(0, dim, 2, dtype=jnp.float32) / dim))
        ang = pos * inv[None, :]
        return jnp.cos(ang), jnp.sin(ang)

    cm, sm = tab(HEAD_DIM)
    cd, sd = tab(DIFF_QK_DIM)
    zd = jnp.zeros_like(sd)
    return (jnp.concatenate([cm, cm], axis=1), jnp.concatenate([-sm, sm], axis=1),
            jnp.concatenate([cd, cd, cd, cd], axis=1),
            jnp.concatenate([-sd, zd, -sd, zd], axis=1),
            jnp.concatenate([zd, sd, zd, sd], axis=1))


def kernel(x, attn_norm_g, w_in, diff_lambda, diff_subln_g, sgu_ln_g, sgu_ln_b, sgu_w, sgu_b,
           w_out, mlp_norm_g, w_mlp_in, w_mlp_out, final_norm_g):
    assert x.shape == (1, SEQ, D_MODEL)
    bf = jnp.bfloat16
    xs = x.reshape(SEQ, D_MODEL)
    cm, sm, cd, sdl, sdh = _rope_tables()
    gf = final_norm_g.reshape(1, D_MODEL)
    for l in range(DEPTH):
        lam_init = 0.8 - 0.6 * math.exp(-0.3 * l)
        g_attn = attn_norm_g[l].reshape(1, D_MODEL)
        w_qkv = w_in[l, :, :QKV_WIDTH].astype(bf)
        w_sgu = w_in[l, :, QKV_WIDTH:].astype(bf)
        qkv, kmean = _qkv_proj(xs, g_attn, w_qkv, cm, sm, cd, sdl, sdh)
        kmean = jnp.pad(kmean.reshape(N_BLOCKS, MOBA_WIDTH), ((0, HEAD_DIM - N_BLOCKS), (0, 0)))
        sgu_o = _sgu(xs, g_attn, w_sgu, sgu_ln_g[l], sgu_ln_b[l], sgu_w[l],
                     sgu_b[l].reshape(SGU_GROUPS, SGU_CHUNK, 1))
        moba_o = _moba_attention(qkv, kmean)
        diff_o = _diff_attention(qkv, diff_lambda[l], diff_subln_g[l].reshape(1, HEAD_DIM), lam_init)
        xs = _out_proj(xs, moba_o, diff_o, sgu_o, w_out[l].astype(bf))
        xs = _mlp(xs, mlp_norm_g[l].reshape(1, D_MODEL), w_mlp_in[l].astype(bf),
                  w_mlp_out[l].astype(bf), gf, l == DEPTH - 1)
    return xs.reshape(1, SEQ, D_MODEL)
```

```python
import functools
import math
from typing import Callable, NamedTuple

import jax
import jax.numpy as jnp
from jax import lax
from jax.experimental import pallas as pl
from jax.experimental.pallas import tpu as pltpu

D_MODEL = 2048
SEQ = 8192
DEPTH = 4
HEAD_DIM = 128
MOBA_HEADS = 6
DIFF_HEADS = 6
SGU_GROUPS = 4
MOBA_WIDTH = MOBA_HEADS * HEAD_DIM
DIFF_WIDTH = DIFF_HEADS * HEAD_DIM
SGU_WIDTH = SGU_GROUPS * HEAD_DIM
DIFF_QK_DIM = HEAD_DIM // 2
MOBA_BLOCK = 256
MOBA_TOPK = 3
SGU_CHUNK = 128
D_FF = 4 * D_MODEL
ROPE_THETA = 10000.0
EPS = 1e-6
QKV_WIDTH = 3 * MOBA_WIDTH + 3 * DIFF_WIDTH
N_BLOCKS = SEQ // MOBA_BLOCK

VMEM_LIMIT_BYTES = 56 * 1024 * 1024

PROJ_TM = 1024
PROJ_TN = 768
SGU_TM = 512
ATTN_TQ = 2 * MOBA_BLOCK
ATTN_TK = 2 * MOBA_BLOCK
ATTN_HEADS_PER_STEP = 2
CAST_CHUNKS = 32
OUT_TM = 512
MLP_TM = 512
MLP_TF = 1024

NEG_INF = float("-inf")
NEG_BIG = -1e30
LOG2E = 1.4426950408889634


def _cparams(sem):
    return pltpu.CompilerParams(dimension_semantics=sem, vmem_limit_bytes=VMEM_LIMIT_BYTES)


def _rms_norm_rows(x, g):
    return x * lax.rsqrt(jnp.mean(x * x, axis=-1, keepdims=True) + EPS) * g


def _gelu_tanh(x):
    c = math.sqrt(2.0 / math.pi)
    return 0.5 * x * (1.0 + jnp.tanh(c * (x + 0.044715 * (x * x * x))))


def _qkv_proj_kernel(x_ref, g_ref, w_ref, cm_ref, sm_ref, cd_ref, sdl_ref, sdh_ref,
                     o_ref, kmean_ref, h_ref):
    j = pl.program_id(1)

    @pl.when(j == 0)
    def _():
        h_ref[...] = _rms_norm_rows(x_ref[...], g_ref[...]).astype(jnp.bfloat16)

    acc = jnp.dot(h_ref[...], w_ref[...], preferred_element_type=jnp.float32)

    def rope_moba(xh):
        return xh * cm_ref[...] + pltpu.roll(xh, HEAD_DIM // 2, 1) * sm_ref[...]

    def rope_diff(xh):
        return (xh * cd_ref[...]
                + pltpu.roll(xh, HEAD_DIM - DIFF_QK_DIM // 2, 1) * sdl_ref[...]
                + pltpu.roll(xh, DIFF_QK_DIM // 2, 1) * sdh_ref[...])

    def heads(fn, with_mean):
        for h in range(PROJ_TN // HEAD_DIM):
            sl = slice(h * HEAD_DIM, (h + 1) * HEAD_DIM)
            r = fn(acc[:, sl])
            o_ref[:, sl] = r.astype(o_ref.dtype)
            if with_mean:
                for b in range(PROJ_TM // MOBA_BLOCK):
                    blk = r[b * MOBA_BLOCK:(b + 1) * MOBA_BLOCK]
                    kmean_ref[0, b:b + 1, sl] = jnp.mean(blk, axis=0, keepdims=True)

    @pl.when(j == 0)
    def _():
        heads(rope_moba, False)

    @pl.when(j == 1)
    def _():
        heads(rope_moba, True)

    @pl.when((j == 2) | (j == 5))
    def _():
        o_ref[...] = acc.astype(o_ref.dtype)

    @pl.when((j == 3) | (j == 4))
    def _():
        heads(rope_diff, False)


def _qkv_proj(x, g, w_qkv, cm, sm, cd, sdl, sdh):
    tm, tn = PROJ_TM, PROJ_TN
    grid = (SEQ // tm, QKV_WIDTH // tn)
    row_tab = pl.BlockSpec((tm, HEAD_DIM), lambda i, j: (i, 0))
    return pl.pallas_call(
        _qkv_proj_kernel,
        grid=grid,
        in_specs=[
            pl.BlockSpec((tm, D_MODEL), lambda i, j: (i, 0)),
            pl.BlockSpec((1, D_MODEL), lambda i, j: (0, 0)),
            pl.BlockSpec((D_MODEL, tn), lambda i, j: (0, j)),
            row_tab, row_tab, row_tab, row_tab, row_tab,
        ],
        out_specs=[
            pl.BlockSpec((tm, tn), lambda i, j: (i, j)),
            pl.BlockSpec((1, tm // MOBA_BLOCK, MOBA_WIDTH), lambda i, j: (i, 0, 0)),
        ],
        out_shape=[
            jax.ShapeDtypeStruct((SEQ, QKV_WIDTH), jnp.bfloat16),
            jax.ShapeDtypeStruct((SEQ // tm, tm // MOBA_BLOCK, MOBA_WIDTH), jnp.float32),
        ],
        scratch_shapes=[pltpu.VMEM((tm, D_MODEL), jnp.bfloat16)],
        compiler_params=_cparams(("arbitrary", "arbitrary")),
        name="qkv_proj",
    )(x, g, w_qkv, cm, sm, cd, sdl, sdh)


def _sgu_kernel(x_ref, g_ref, wu_ref, wv_ref, lng_ref, lnb_ref, ws_ref, bs_ref, o_ref):
    h = _rms_norm_rows(x_ref[...], g_ref[...]).astype(jnp.bfloat16)
    us = jnp.dot(h, wu_ref[...], preferred_element_type=jnp.float32)
    vs = jnp.dot(h, wv_ref[...], preferred_element_type=jnp.float32)
    row = lax.broadcasted_iota(jnp.int32, (SGU_CHUNK, SGU_CHUNK), 0)
    col = lax.broadcasted_iota(jnp.int32, (SGU_CHUNK, SGU_CHUNK), 1)
    for g in range(SGU_GROUPS):
        sl = slice(g * HEAD_DIM, (g + 1) * HEAD_DIM)
        u = _gelu_tanh(us[:, sl])
        v = _gelu_tanh(vs[:, sl])
        mu = jnp.mean(v, axis=-1, keepdims=True)
        vc = v - mu
        var = jnp.mean(vc * vc, axis=-1, keepdims=True)
        vn = (vc * lax.rsqrt(var + EPS) * lng_ref[g:g + 1, :] + lnb_ref[g:g + 1, :]).astype(jnp.bfloat16)
        w = jnp.where(row >= col, ws_ref[g], 0.0).astype(jnp.bfloat16)
        bias = bs_ref[g]
        for c in range(SGU_TM // SGU_CHUNK):
            rs = slice(c * SGU_CHUNK, (c + 1) * SGU_CHUNK)
            mixed = jnp.dot(w, vn[rs], preferred_element_type=jnp.float32) + bias
            o_ref[rs, sl] = (u[rs] * mixed).astype(o_ref.dtype)


def _sgu(x, g, w_in, ln_g, ln_b, w_s, b_s):
    tm = SGU_TM
    u_block = QKV_WIDTH // SGU_WIDTH
    full = lambda shape: pl.BlockSpec(shape, lambda i: (0,) * len(shape))
    return pl.pallas_call(
        _sgu_kernel,
        grid=(SEQ // tm,),
        in_specs=[
            pl.BlockSpec((tm, D_MODEL), lambda i: (i, 0)),
            full((1, D_MODEL)),
            pl.BlockSpec((D_MODEL, SGU_WIDTH), lambda i: (0, u_block)),
            pl.BlockSpec((D_MODEL, SGU_WIDTH), lambda i: (0, u_block + 1)),
            full((SGU_GROUPS, HEAD_DIM)),
            full((SGU_GROUPS, HEAD_DIM)),
            full((SGU_GROUPS, SGU_CHUNK, SGU_CHUNK)),
            full((SGU_GROUPS, SGU_CHUNK, 1)),
        ],
        out_specs=pl.BlockSpec((tm, SGU_WIDTH), lambda i: (i, 0)),
        out_shape=jax.ShapeDtypeStruct((SEQ, SGU_WIDTH), jnp.bfloat16),
        compiler_params=_cparams(("arbitrary",)),
        name="sgu",
    )(x, g, w_in, w_in, ln_g, ln_b, w_s, b_s)


class _Stream(NamedTuple):
    qk_own: Callable
    qk_past: Callable
    v_tile: Callable
    own_mask: Callable
    sa_ref: object
    sb_ref: object
    m_ref: object
    acc_ref: object
    c: float


def _scores(q, k):
    return lax.dot_general(q, k, (((1,), (1,)), ((), ())), preferred_element_type=jnp.float32)


def _with_ones(v):
    return jnp.concatenate([v, jnp.ones(v.shape, v.dtype)], axis=1)


def _lane_tile(x, n):
    return jnp.concatenate([x] * n, axis=1)


def _softmax_pv(st, s_ref, j, mask_fn=None):
    s = s_ref[...]
    if mask_fn is not None:
        s = mask_fn(s)
    m_prev = st.m_ref[...]
    m_new = jnp.maximum(m_prev, jnp.max(s, axis=-1, keepdims=True))
    alpha = jnp.exp2((m_prev - m_new) * st.c)
    p = jnp.exp2((s - _lane_tile(m_new, s.shape[1] // HEAD_DIM)) * st.c).astype(jnp.bfloat16)
    st.acc_ref[...] = (_lane_tile(alpha, 2) * st.acc_ref[...]
                       + jnp.dot(p, st.v_tile(j), preferred_element_type=jnp.float32))
    st.m_ref[...] = m_new


def _flash_pipeline(n_past, streams):
    for st in streams:
        st.m_ref[...] = jnp.full(st.m_ref.shape, NEG_INF, jnp.float32)
        st.acc_ref[...] = jnp.zeros(st.acc_ref.shape, jnp.float32)
        st.qk_own(st.sa_ref)
    for st in streams:
        st.qk_past(0, st.sb_ref)
        _softmax_pv(st, st.sa_ref, n_past, st.own_mask)

    def pair(jj, carry):
        j = 2 * jj
        for st in streams:
            st.qk_past(j + 1, st.sa_ref)
            _softmax_pv(st, st.sb_ref, j)
        for st in streams:
            st.qk_past(j + 2, st.sb_ref)
            _softmax_pv(st, st.sa_ref, j + 1)
        return carry

    lax.fori_loop(0, n_past // 2, pair, 0)

    @pl.when(n_past % 2 == 1)
    def _():
        for st in streams:
            _softmax_pv(st, st.sb_ref, n_past - 1)


def _cast_specs(w, layer):
    rows, cols = w.shape[1], w.shape[2]
    chunk = rows // CAST_CHUNKS
    assert chunk * CAST_CHUNKS == rows and chunk % 16 == 0
    steps_per_group = SEQ // ATTN_TQ

    def chunk_index(h, i):
        return jnp.minimum(h * steps_per_group + i, CAST_CHUNKS - 1)

    return (pl.BlockSpec((None, chunk, cols), lambda h, i: (layer, chunk_index(h, i), 0)),
            pl.BlockSpec((chunk, cols), lambda h, i: (chunk_index(h, i), 0)),
            jax.ShapeDtypeStruct((rows, cols), jnp.bfloat16))


def _cast_chunks(src_refs, dst_refs):
    for src, dst in zip(src_refs, dst_refs, strict=True):
        dst[...] = src[...].astype(dst.dtype)


def _moba_stream(i, q, k_ref, v_ref, kmean, qa_ref, sa_ref, sb_ref, m_ref, acc_ref):
    tq, tk = ATTN_TQ, ATTN_TK
    gate = lax.dot_general(q.astype(jnp.float32), kmean, (((1,), (1,)), ((), ())),
                           preferred_element_type=jnp.float32,
                           precision=lax.Precision.HIGHEST)
    blk = lax.broadcasted_iota(jnp.int32, (tq, HEAD_DIM), 1)
    row = lax.broadcasted_iota(jnp.int32, (tq, HEAD_DIM), 0)
    own = 2 * i + jnp.where(row >= MOBA_BLOCK, 1, 0)
    gate = jnp.where(blk < own, gate, NEG_INF)
    blk_f = blk.astype(jnp.float32)
    sel = jnp.zeros((tq, HEAD_DIM), jnp.float32)
    for _ in range(MOBA_TOPK):
        mx = jnp.max(gate, axis=-1, keepdims=True)
        cand = jnp.where((gate == mx) & (gate > NEG_INF), blk_f, float(HEAD_DIM))
        pick = blk_f == jnp.min(cand, axis=-1, keepdims=True)
        sel = jnp.where(pick, 1.0, sel)
        gate = jnp.where(pick, NEG_INF, gate)
    qa_ref[:, :HEAD_DIM] = q
    qa_ref[:, HEAD_DIM:] = jnp.where(sel > 0.0, 0.0, NEG_BIG).astype(jnp.bfloat16)
    picked_first = jnp.sum(jnp.where(blk == 2 * i, sel, 0.0), axis=-1, keepdims=True) > 0.0

    def own_mask(s):
        r = lax.broadcasted_iota(jnp.int32, s.shape, 0)
        col = lax.broadcasted_iota(jnp.int32, s.shape, 1)
        rr = lax.broadcasted_iota(jnp.int32, (tq, 1), 0)
        first_col = jnp.where(picked_first | (rr < MOBA_BLOCK), 0, MOBA_BLOCK)
        return jnp.where(col <= r, jnp.where(col >= first_col, s, NEG_INF), NEG_INF)

    def qk_own(s_ref):
        s_ref[...] = _scores(qa_ref[:, :HEAD_DIM], k_ref[pl.ds(pl.multiple_of(i * tk, tk), tk), :])

    def qk_past(j, s_ref):
        erow = lax.broadcasted_iota(jnp.int32, (tk, HEAD_DIM), 0)
        elane = lax.broadcasted_iota(jnp.int32, (tk, HEAD_DIM), 1)
        onehot = jnp.where(elane == 2 * j + jnp.where(erow >= MOBA_BLOCK, 1, 0), 1.0, 0.0).astype(jnp.bfloat16)
        ka = jnp.concatenate([k_ref[pl.ds(pl.multiple_of(j * tk, tk), tk), :], onehot], axis=1)
        s_ref[...] = _scores(qa_ref[...], ka)

    def v_tile(j):
        return _with_ones(v_ref[pl.ds(pl.multiple_of(j * tk, tk), tk), :])

    return _Stream(qk_own, qk_past, v_tile, own_mask, sa_ref, sb_ref, m_ref, acc_ref,
                   (HEAD_DIM ** -0.5) * LOG2E)


def _moba_kernel(n_cast, q_ref, k_ref, v_ref, kmean_ref, *refs):
    cast_src, (o_ref, *cast_dst) = refs[:n_cast], refs[n_cast:2 * n_cast + 1]
    qa_ref, sa_ref, sb_ref, m_ref, acc_ref = refs[2 * n_cast + 1:]
    _cast_chunks(cast_src, cast_dst)
    i = pl.program_id(1)
    streams = []
    for h in range(ATTN_HEADS_PER_STEP):
        sl = slice(h * HEAD_DIM, (h + 1) * HEAD_DIM)
        streams.append(_moba_stream(i, q_ref[:, sl], k_ref.at[:, sl], v_ref.at[:, sl], kmean_ref[:, sl],
                                    qa_ref.at[h], sa_ref.at[h], sb_ref.at[h], m_ref.at[h], acc_ref.at[h]))
    _flash_pipeline(i, streams)
    for h in range(ATTN_HEADS_PER_STEP):
        acc = acc_ref[h]
        o_ref[:, h * HEAD_DIM:(h + 1) * HEAD_DIM] = (acc[:, :HEAD_DIM] / acc[:, HEAD_DIM:]).astype(o_ref.dtype)


def _moba_attention(qkv, kmean_pad, casts):
    tq, tk, hp = ATTN_TQ, ATTN_TK, ATTN_HEADS_PER_STEP
    w = hp * HEAD_DIM
    groups = MOBA_HEADS // hp
    assert groups * (SEQ // tq) >= CAST_CHUNKS
    cast_in, cast_out, cast_shape = zip(*[_cast_specs(cw, cl) for cw, cl in casts])
    return pl.pallas_call(
        functools.partial(_moba_kernel, len(casts)),
        grid=(groups, SEQ // tq),
        in_specs=[
            pl.BlockSpec((tq, w), lambda h, i: (i, h)),
            pl.BlockSpec((SEQ, w), lambda h, i: (0, groups + h)),
            pl.BlockSpec((SEQ, w), lambda h, i: (0, 2 * groups + h)),
            pl.BlockSpec((HEAD_DIM, w), lambda h, i: (0, h)),
            *cast_in,
        ],
        out_specs=[pl.BlockSpec((tq, w), lambda h, i: (i, h)), *cast_out],
        out_shape=[jax.ShapeDtypeStruct((SEQ, MOBA_WIDTH), jnp.bfloat16), *cast_shape],
        scratch_shapes=[
            pltpu.VMEM((hp, tq, 2 * HEAD_DIM), jnp.bfloat16),
            pltpu.VMEM((hp, tq, tk), jnp.float32),
            pltpu.VMEM((hp, tq, tk), jnp.float32),
            pltpu.VMEM((hp, tq, HEAD_DIM), jnp.float32),
            pltpu.VMEM((hp, tq, 2 * HEAD_DIM), jnp.float32),
        ],
        compiler_params=_cparams(("arbitrary", "arbitrary")),
        name="moba_attention",
    )(qkv, qkv, qkv, kmean_pad, *[cw for cw, _ in casts])


def _diff_stream(i, q, k_ref, v_ref, q2_ref, sa_ref, sb_ref, m_ref, acc_ref):
    tq, tk = ATTN_TQ, ATTN_TK
    q = q * (DIFF_QK_DIM ** -0.5)
    lane = lax.broadcasted_iota(jnp.int32, (tq, HEAD_DIM), 1)
    zero = jnp.zeros_like(q)
    q2_ref[:tq, :] = jnp.where(lane < DIFF_QK_DIM, q, zero)
    q2_ref[tq:, :] = jnp.where(lane >= DIFF_QK_DIM, q, zero)

    def causal_mask(s):
        r = lax.broadcasted_iota(jnp.int32, s.shape, 0)
        r = jnp.where(r >= tq, r - tq, r)
        col = lax.broadcasted_iota(jnp.int32, s.shape, 1)
        return jnp.where(col <= r, s, NEG_INF)

    def qk_past(j, s_ref):
        s_ref[...] = _scores(q2_ref[...], k_ref[pl.ds(pl.multiple_of(j * tk, tk), tk), :])

    def v_tile(j):
        return _with_ones(v_ref[pl.ds(pl.multiple_of(j * tk, tk), tk), :])

    return _Stream(functools.partial(qk_past, i), qk_past, v_tile, causal_mask,
                   sa_ref, sb_ref, m_ref, acc_ref, LOG2E)


def _diff_kernel(lam_init, n_cast, q_ref, k_ref, v_ref, lp_ref, g_ref, *refs):
    cast_src, (o_ref, *cast_dst) = refs[:n_cast], refs[n_cast:2 * n_cast + 1]
    q2_ref, sa_ref, sb_ref, m_ref, acc_ref = refs[2 * n_cast + 1:]
    _cast_chunks(cast_src, cast_dst)
    i = pl.program_id(1)
    tq = ATTN_TQ
    streams = []
    for h in range(ATTN_HEADS_PER_STEP):
        sl = slice(h * HEAD_DIM, (h + 1) * HEAD_DIM)
        streams.append(_diff_stream(i, q_ref[:, sl], k_ref.at[:, sl], v_ref.at[:, sl],
                                    q2_ref.at[h], sa_ref.at[h], sb_ref.at[h], m_ref.at[h], acc_ref.at[h]))
    _flash_pipeline(i, streams)

    lp = lp_ref[...]
    lam = (jnp.exp(jnp.sum(lp[0:1] * lp[1:2], axis=-1, keepdims=True))
           - jnp.exp(jnp.sum(lp[2:3] * lp[3:4], axis=-1, keepdims=True)) + lam_init)
    for h in range(ATTN_HEADS_PER_STEP):
        acc = acc_ref[h]
        o = acc[:, :HEAD_DIM] / acc[:, HEAD_DIM:]
        o = o[:tq] - lam * o[tq:]
        o = _rms_norm_rows(o, g_ref[...]) * (1.0 - lam_init)
        o_ref[:, h * HEAD_DIM:(h + 1) * HEAD_DIM] = o.astype(o_ref.dtype)


def _diff_attention(qkv, lam_params, subln_g, lam_init, casts):
    tq, tk, hp = ATTN_TQ, ATTN_TK, ATTN_HEADS_PER_STEP
    w = hp * HEAD_DIM
    groups = DIFF_HEADS // hp
    base = 3 * MOBA_WIDTH // w
    assert groups * (SEQ // tq) >= CAST_CHUNKS
    cast_in, cast_out, cast_shape = zip(*[_cast_specs(cw, cl) for cw, cl in casts])
    return pl.pallas_call(
        functools.partial(_diff_kernel, lam_init, len(casts)),
        grid=(groups, SEQ // tq),
        in_specs=[
            pl.BlockSpec((tq, w), lambda h, i: (i, base + h)),
            pl.BlockSpec((SEQ, w), lambda h, i: (0, base + groups + h)),
            pl.BlockSpec((SEQ, w), lambda h, i: (0, base + 2 * groups + h)),
            pl.BlockSpec((4, DIFF_QK_DIM), lambda h, i: (0, 0)),
            pl.BlockSpec((1, HEAD_DIM), lambda h, i: (0, 0)),
            *cast_in,
        ],
        out_specs=[pl.BlockSpec((tq, w), lambda h, i: (i, h)), *cast_out],
        out_shape=[jax.ShapeDtypeStruct((SEQ, DIFF_WIDTH), jnp.bfloat16), *cast_shape],
        scratch_shapes=[
            pltpu.VMEM((hp, 2 * tq, HEAD_DIM), jnp.bfloat16),
            pltpu.VMEM((hp, 2 * tq, tk), jnp.float32),
            pltpu.VMEM((hp, 2 * tq, tk), jnp.float32),
            pltpu.VMEM((hp, 2 * tq, HEAD_DIM), jnp.float32),
            pltpu.VMEM((hp, 2 * tq, 2 * HEAD_DIM), jnp.float32),
        ],
        compiler_params=_cparams(("arbitrary", "arbitrary")),
        name="diff_attention",
    )(qkv, qkv, qkv, lam_params, subln_g, *[cw for cw, _ in casts])


def _out_proj_kernel(x_ref, a_ref, b_ref, c_ref, w_ref, o_ref):
    acc = jnp.dot(a_ref[...], w_ref[0:MOBA_WIDTH, :], preferred_element_type=jnp.float32)
    acc += jnp.dot(b_ref[...], w_ref[MOBA_WIDTH:MOBA_WIDTH + DIFF_WIDTH, :],
                   preferred_element_type=jnp.float32)
    acc += jnp.dot(c_ref[...], w_ref[MOBA_WIDTH + DIFF_WIDTH:, :], preferred_element_type=jnp.float32)
    o_ref[...] = x_ref[...] + acc


def _out_proj(x, moba_o, diff_o, sgu_o, w_out):
    tm = OUT_TM
    return pl.pallas_call(
        _out_proj_kernel,
        grid=(SEQ // tm,),
        in_specs=[
            pl.BlockSpec((tm, D_MODEL), lambda i: (i, 0)),
            pl.BlockSpec((tm, MOBA_WIDTH), lambda i: (i, 0)),
            pl.BlockSpec((tm, DIFF_WIDTH), lambda i: (i, 0)),
            pl.BlockSpec((tm, SGU_WIDTH), lambda i: (i, 0)),
            pl.BlockSpec((D_MODEL, D_MODEL), lambda i: (0, 0)),
        ],
        out_specs=pl.BlockSpec((tm, D_MODEL), lambda i: (i, 0)),
        out_shape=jax.ShapeDtypeStruct((SEQ, D_MODEL), jnp.float32),
        compiler_params=_cparams(("arbitrary",)),
        name="out_proj",
    )(x, moba_o, diff_o, sgu_o, w_out)


def _mlp_kernel(final, x_ref, g_ref, w1_ref, w2_ref, gf_ref, o_ref, h_ref, acc_ref):
    j = pl.program_id(1)

    @pl.when(j == 0)
    def _():
        h_ref[...] = _rms_norm_rows(x_ref[...], g_ref[...]).astype(jnp.bfloat16)
        acc_ref[...] = jnp.zeros(acc_ref.shape, jnp.float32)

    a = jnp.dot(h_ref[...], w1_ref[...], preferred_element_type=jnp.float32)
    a = jnp.square(jnp.maximum(a, 0.0)).astype(jnp.bfloat16)
    acc_ref[...] += jnp.dot(a, w2_ref[...], preferred_element_type=jnp.float32)

    @pl.when(j == pl.num_programs(1) - 1)
    def _():
        y = x_ref[...] + acc_ref[...]
        if final:
            y = _rms_norm_rows(y, gf_ref[...])
        o_ref[...] = y


def _mlp(x, g, w1, w2, g_final, final):
    tm, tf = MLP_TM, MLP_TF
    return pl.pallas_call(
        functools.partial(_mlp_kernel, final),
        grid=(SEQ // tm, D_FF // tf),
        in_specs=[
            pl.BlockSpec((tm, D_MODEL), lambda i, j: (i, 0)),
            pl.BlockSpec((1, D_MODEL), lambda i, j: (0, 0)),
            pl.BlockSpec((D_MODEL, tf), lambda i, j: (0, j)),
            pl.BlockSpec((tf, D_MODEL), lambda i, j: (j, 0)),
            pl.BlockSpec((1, D_MODEL), lambda i, j: (0, 0)),
        ],
        out_specs=pl.BlockSpec((tm, D_MODEL), lambda i, j: (i, 0)),
        out_shape=jax.ShapeDtypeStruct((SEQ, D_MODEL), jnp.float32),
        scratch_shapes=[
            pltpu.VMEM((tm, D_MODEL), jnp.bfloat16),
            pltpu.VMEM((tm, D_MODEL), jnp.float32),
        ],
        compiler_params=_cparams(("arbitrary", "arbitrary")),
        name="mlp",
    )(x, g, w1, w2, g_final)


def _rope_tables():
    pos = jnp.arange(SEQ, dtype=jnp.float32)[:, None]

    def tab(dim):
        inv = 1.0 / (ROPE_THETA ** (jnp.arange(0, dim, 2, dtype=jnp.float32) / dim))
        ang = pos * inv[None, :]
        return jnp.cos(ang), jnp.sin(ang)

    cm, sm = tab(HEAD_DIM)
    cd, sd = tab(DIFF_QK_DIM)
    zd = jnp.zeros_like(sd)
    return (jnp.concatenate([cm, cm], axis=1), jnp.concatenate([-sm, sm], axis=1),
            jnp.concatenate([cd, cd, cd, cd], axis=1),
            jnp.concatenate([-sd, zd, -sd, zd], axis=1),
            jnp.concatenate([zd, sd, zd, sd], axis=1))


def kernel(x, attn_norm_g, w_in, diff_lambda, diff_subln_g, sgu_ln_g, sgu_ln_b, sgu_w, sgu_b,
           w_out, mlp_norm_g, w_mlp_in, w_mlp_out, final_norm_g):
    assert x.shape == (1, SEQ, D_MODEL)
    bf = jnp.bfloat16
    xs = x.reshape(SEQ, D_MODEL)
    cm, sm, cd, sdl, sdh = _rope_tables()
    gf = final_norm_g.reshape(1, D_MODEL)
    w_in_l = w_in[0].astype(bf)
    for l in range(DEPTH):
        lam_init = 0.8 - 0.6 * math.exp(-0.3 * l)
        g_attn = attn_norm_g[l].reshape(1, D_MODEL)
        qkv, kmean = _qkv_proj(xs, g_attn, w_in_l, cm, sm, cd, sdl, sdh)
        kmean = jnp.pad(kmean.reshape(N_BLOCKS, MOBA_WIDTH), ((0, HEAD_DIM - N_BLOCKS), (0, 0)))
        sgu_o = _sgu(xs, g_attn, w_in_l, sgu_ln_g[l], sgu_ln_b[l], sgu_w[l],
                     sgu_b[l].reshape(SGU_GROUPS, SGU_CHUNK, 1))
        moba_o, w1_l, w_out_l = _moba_attention(qkv, kmean, [(w_mlp_in, l), (w_out, l)])
        diff_casts = [(w_mlp_out, l)] + ([(w_in, l + 1)] if l + 1 < DEPTH else [])
        diff_o, w2_l, *w_in_next = _diff_attention(qkv, diff_lambda[l], diff_subln_g[l].reshape(1, HEAD_DIM),
                                                   lam_init, diff_casts)
        xs = _out_proj(xs, moba_o, diff_o, sgu_o, w_out_l)
        xs = _mlp(xs, mlp_norm_g[l].reshape(1, D_MODEL), w1_l, w2_l, gf, l == DEPTH - 1)
        if w_in_next:
            w_in_l = w_in_next[0]
    return xs.reshape(1, SEQ, D_MODEL)
```

```python
import functools
import math
from typing import Callable, NamedTuple

import jax
import jax.numpy as jnp
from jax import lax
from jax.experimental import pallas as pl
from jax.experimental.pallas import tpu as pltpu

D_MODEL = 2048
SEQ = 8192
DEPTH = 4
HEAD_DIM = 128
MOBA_HEADS = 6
DIFF_HEADS = 6
SGU_GROUPS = 4
MOBA_WIDTH = MOBA_HEADS * HEAD_DIM
DIFF_WIDTH = DIFF_HEADS * HEAD_DIM
SGU_WIDTH = SGU_GROUPS * HEAD_DIM
DIFF_QK_DIM = HEAD_DIM // 2
MOBA_BLOCK = 256
MOBA_TOPK = 3
SGU_CHUNK = 128
D_FF = 4 * D_MODEL
ROPE_THETA = 10000.0
EPS = 1e-6
QKV_WIDTH = 3 * MOBA_WIDTH + 3 * DIFF_WIDTH
N_BLOCKS = SEQ // MOBA_BLOCK

VMEM_LIMIT_BYTES = 56 * 1024 * 1024

PROJ_TM = 512
PROJ_TN = 768
ATTN_TQ = 2 * MOBA_BLOCK
ATTN_TK = 2 * MOBA_BLOCK
ATTN_HEADS_PER_STEP = 2
CAST_CHUNKS = 32
OUT_TM = 512
MLP_TM = 512
MLP_TF = 1024

NEG_INF = float("-inf")
NEG_BIG = -1e30
LOG2E = 1.4426950408889634


def _cparams(sem):
    return pltpu.CompilerParams(dimension_semantics=sem, vmem_limit_bytes=VMEM_LIMIT_BYTES)


def _rms_norm_rows(x, g):
    return x * lax.rsqrt(jnp.mean(x * x, axis=-1, keepdims=True) + EPS) * g


def _gelu_tanh(x):
    c = math.sqrt(2.0 / math.pi)
    return 0.5 * x * (1.0 + jnp.tanh(c * (x + 0.044715 * (x * x * x))))


def _in_proj_kernel(x_ref, g_ref, w_ref, cm_ref, sm_ref, cd_ref, sdl_ref, sdh_ref,
                    lng_ref, lnb_ref, ws_ref, bs_ref, o_ref, kmean_ref, sgu_ref):
    h = _rms_norm_rows(x_ref[...], g_ref[...]).astype(jnp.bfloat16)

    def rope_moba(xh):
        return xh * cm_ref[...] + pltpu.roll(xh, HEAD_DIM // 2, 1) * sm_ref[...]

    def rope_diff(xh):
        return (xh * cd_ref[...]
                + pltpu.roll(xh, HEAD_DIM - DIFF_QK_DIM // 2, 1) * sdl_ref[...]
                + pltpu.roll(xh, DIFF_QK_DIM // 2, 1) * sdh_ref[...])

    epilogues = (rope_moba, rope_moba, None, rope_diff, rope_diff, None)
    for grp, fn in enumerate(epilogues):
        acc = jnp.dot(h, w_ref[:, grp * PROJ_TN:(grp + 1) * PROJ_TN], preferred_element_type=jnp.float32)
        for hd in range(PROJ_TN // HEAD_DIM):
            sl = slice(hd * HEAD_DIM, (hd + 1) * HEAD_DIM)
            r = acc[:, sl] if fn is None else fn(acc[:, sl])
            o_ref[:, grp * PROJ_TN + hd * HEAD_DIM:grp * PROJ_TN + (hd + 1) * HEAD_DIM] = r.astype(o_ref.dtype)
            if grp == 1:
                for b in range(PROJ_TM // MOBA_BLOCK):
                    blk = r[b * MOBA_BLOCK:(b + 1) * MOBA_BLOCK]
                    kmean_ref[0, b:b + 1, sl] = jnp.mean(blk, axis=0, keepdims=True)

    us = jnp.dot(h, w_ref[:, QKV_WIDTH:QKV_WIDTH + SGU_WIDTH], preferred_element_type=jnp.float32)
    vs = jnp.dot(h, w_ref[:, QKV_WIDTH + SGU_WIDTH:], preferred_element_type=jnp.float32)
    row = lax.broadcasted_iota(jnp.int32, (SGU_CHUNK, SGU_CHUNK), 0)
    col = lax.broadcasted_iota(jnp.int32, (SGU_CHUNK, SGU_CHUNK), 1)
    for g in range(SGU_GROUPS):
        sl = slice(g * HEAD_DIM, (g + 1) * HEAD_DIM)
        u = _gelu_tanh(us[:, sl])
        v = _gelu_tanh(vs[:, sl])
        mu = jnp.mean(v, axis=-1, keepdims=True)
        vc = v - mu
        var = jnp.mean(vc * vc, axis=-1, keepdims=True)
        vn = (vc * lax.rsqrt(var + EPS) * lng_ref[g:g + 1, :] + lnb_ref[g:g + 1, :]).astype(jnp.bfloat16)
        w = jnp.where(row >= col, ws_ref[g], 0.0).astype(jnp.bfloat16)
        bias = bs_ref[g]
        for c in range(PROJ_TM // SGU_CHUNK):
            rs = slice(c * SGU_CHUNK, (c + 1) * SGU_CHUNK)
            mixed = jnp.dot(w, vn[rs], preferred_element_type=jnp.float32) + bias
            sgu_ref[rs, sl] = (u[rs] * mixed).astype(sgu_ref.dtype)


def _in_proj(x, g, w_in, cm, sm, cd, sdl, sdh, ln_g, ln_b, w_s, b_s):
    tm = PROJ_TM
    row_tab = pl.BlockSpec((tm, HEAD_DIM), lambda i: (i, 0))
    full = lambda shape: pl.BlockSpec(shape, lambda i: (0,) * len(shape))
    return pl.pallas_call(
        _in_proj_kernel,
        grid=(SEQ // tm,),
        in_specs=[
            pl.BlockSpec((tm, D_MODEL), lambda i: (i, 0)),
            pl.BlockSpec((1, D_MODEL), lambda i: (0, 0)),
            pl.BlockSpec((D_MODEL, QKV_WIDTH + 2 * SGU_WIDTH), lambda i: (0, 0), pipeline_mode=pl.Buffered(1)),
            row_tab, row_tab, row_tab, row_tab, row_tab,
            full((SGU_GROUPS, HEAD_DIM)),
            full((SGU_GROUPS, HEAD_DIM)),
            full((SGU_GROUPS, SGU_CHUNK, SGU_CHUNK)),
            full((SGU_GROUPS, SGU_CHUNK, 1)),
        ],
        out_specs=[
            pl.BlockSpec((tm, QKV_WIDTH), lambda i: (i, 0)),
            pl.BlockSpec((1, tm // MOBA_BLOCK, MOBA_WIDTH), lambda i: (i, 0, 0)),
            pl.BlockSpec((tm, SGU_WIDTH), lambda i: (i, 0)),
        ],
        out_shape=[
            jax.ShapeDtypeStruct((SEQ, QKV_WIDTH), jnp.bfloat16),
            jax.ShapeDtypeStruct((SEQ // tm, tm // MOBA_BLOCK, MOBA_WIDTH), jnp.float32),
            jax.ShapeDtypeStruct((SEQ, SGU_WIDTH), jnp.bfloat16),
        ],
        compiler_params=_cparams(("arbitrary",)),
        name="in_proj",
    )(x, g, w_in, cm, sm, cd, sdl, sdh, ln_g, ln_b, w_s, b_s)


class _Stream(NamedTuple):
    qk_own: Callable
    qk_past: Callable
    v_tile: Callable
    own_mask: Callable
    sa_ref: object
    sb_ref: object
    m_ref: object
    acc_ref: object
    c: float


def _scores(q, k):
    return lax.dot_general(q, k, (((1,), (1,)), ((), ())), preferred_element_type=jnp.float32)


def _with_ones(v):
    return jnp.concatenate([v, jnp.ones(v.shape, v.dtype)], axis=1)


def _lane_tile(x, n):
    return jnp.concatenate([x] * n, axis=1)


def _softmax_pv(st, s_ref, j, mask_fn=None):
    s = s_ref[...]
    if mask_fn is not None:
        s = mask_fn(s)
    m_prev = st.m_ref[...]
    m_new = jnp.maximum(m_prev, jnp.max(s, axis=-1, keepdims=True))
    alpha = jnp.exp2((m_prev - m_new) * st.c)
    p = jnp.exp2((s - _lane_tile(m_new, s.shape[1] // HEAD_DIM)) * st.c).astype(jnp.bfloat16)
    st.acc_ref[...] = (_lane_tile(alpha, 2) * st.acc_ref[...]
                       + jnp.dot(p, st.v_tile(j), preferred_element_type=jnp.float32))
    st.m_ref[...] = m_new


def _flash_pipeline(n_past, streams):
    for st in streams:
        st.m_ref[...] = jnp.full(st.m_ref.shape, NEG_INF, jnp.float32)
        st.acc_ref[...] = jnp.zeros(st.acc_ref.shape, jnp.float32)
        st.qk_own(st.sa_ref)
    for st in streams:
        st.qk_past(0, st.sb_ref)
        _softmax_pv(st, st.sa_ref, n_past, st.own_mask)

    def pair(jj, carry):
        j = 2 * jj
        for st in streams:
            st.qk_past(j + 1, st.sa_ref)
            _softmax_pv(st, st.sb_ref, j)
        for st in streams:
            st.qk_past(j + 2, st.sb_ref)
            _softmax_pv(st, st.sa_ref, j + 1)
        return carry

    lax.fori_loop(0, n_past // 2, pair, 0)

    @pl.when(n_past % 2 == 1)
    def _():
        for st in streams:
            _softmax_pv(st, st.sb_ref, n_past - 1)


def _cast_specs(w, layer):
    rows, cols = w.shape[1], w.shape[2]
    chunk = rows // CAST_CHUNKS
    assert chunk * CAST_CHUNKS == rows and chunk % 16 == 0
    steps_per_group = SEQ // ATTN_TQ

    def chunk_index(h, i):
        return jnp.minimum(h * steps_per_group + i, CAST_CHUNKS - 1)

    return (pl.BlockSpec((None, chunk, cols), lambda h, i: (layer, chunk_index(h, i), 0)),
            pl.BlockSpec((chunk, cols), lambda h, i: (chunk_index(h, i), 0)),
            jax.ShapeDtypeStruct((rows, cols), jnp.bfloat16))


def _cast_chunks(src_refs, dst_refs):
    for src, dst in zip(src_refs, dst_refs, strict=True):
        dst[...] = src[...].astype(dst.dtype)


def _moba_stream(i, q, k_ref, v_ref, kmean, qa_ref, sa_ref, sb_ref, m_ref, acc_ref):
    tq, tk = ATTN_TQ, ATTN_TK
    gate = lax.dot_general(q.astype(jnp.float32), kmean, (((1,), (1,)), ((), ())),
                           preferred_element_type=jnp.float32,
                           precision=lax.Precision.HIGHEST)
    blk = lax.broadcasted_iota(jnp.int32, (tq, HEAD_DIM), 1)
    row = lax.broadcasted_iota(jnp.int32, (tq, HEAD_DIM), 0)
    own = 2 * i + jnp.where(row >= MOBA_BLOCK, 1, 0)
    gate = jnp.where(blk < own, gate, NEG_INF)
    blk_f = blk.astype(jnp.float32)
    sel = jnp.zeros((tq, HEAD_DIM), jnp.float32)
    for _ in range(MOBA_TOPK):
        mx = jnp.max(gate, axis=-1, keepdims=True)
        cand = jnp.where((gate == mx) & (gate > NEG_INF), blk_f, float(HEAD_DIM))
        pick = blk_f == jnp.min(cand, axis=-1, keepdims=True)
        sel = jnp.where(pick, 1.0, sel)
        gate = jnp.where(pick, NEG_INF, gate)
    qa_ref[:, :HEAD_DIM] = q
    qa_ref[:, HEAD_DIM:] = jnp.where(sel > 0.0, 0.0, NEG_BIG).astype(jnp.bfloat16)
    picked_first = jnp.sum(jnp.where(blk == 2 * i, sel, 0.0), axis=-1, keepdims=True) > 0.0

    def own_mask(s):
        r = lax.broadcasted_iota(jnp.int32, s.shape, 0)
        col = lax.broadcasted_iota(jnp.int32, s.shape, 1)
        rr = lax.broadcasted_iota(jnp.int32, (tq, 1), 0)
        first_col = jnp.where(picked_first | (rr < MOBA_BLOCK), 0, MOBA_BLOCK)
        return jnp.where(col <= r, jnp.where(col >= first_col, s, NEG_INF), NEG_INF)

    def qk_own(s_ref):
        s_ref[...] = _scores(qa_ref[:, :HEAD_DIM], k_ref[pl.ds(pl.multiple_of(i * tk, tk), tk), :])

    def qk_past(j, s_ref):
        erow = lax.broadcasted_iota(jnp.int32, (tk, HEAD_DIM), 0)
        elane = lax.broadcasted_iota(jnp.int32, (tk, HEAD_DIM), 1)
        onehot = jnp.where(elane == 2 * j + jnp.where(erow >= MOBA_BLOCK, 1, 0), 1.0, 0.0).astype(jnp.bfloat16)
        ka = jnp.concatenate([k_ref[pl.ds(pl.multiple_of(j * tk, tk), tk), :], onehot], axis=1)
        s_ref[...] = _scores(qa_ref[...], ka)

    def v_tile(j):
        return _with_ones(v_ref[pl.ds(pl.multiple_of(j * tk, tk), tk), :])

    return _Stream(qk_own, qk_past, v_tile, own_mask, sa_ref, sb_ref, m_ref, acc_ref,
                   (HEAD_DIM ** -0.5) * LOG2E)


def _moba_kernel(n_cast, q_ref, k_ref, v_ref, kmean_ref, *refs):
    cast_src, (o_ref, *cast_dst) = refs[:n_cast], refs[n_cast:2 * n_cast + 1]
    qa_ref, sa_ref, sb_ref, m_ref, acc_ref = refs[2 * n_cast + 1:]
    _cast_chunks(cast_src, cast_dst)
    i = pl.program_id(1)
    streams = []
    for h in range(ATTN_HEADS_PER_STEP):
        sl = slice(h * HEAD_DIM, (h + 1) * HEAD_DIM)
        streams.append(_moba_stream(i, q_ref[:, sl], k_ref.at[:, sl], v_ref.at[:, sl], kmean_ref[:, sl],
                                    qa_ref.at[h], sa_ref.at[h], sb_ref.at[h], m_ref.at[h], acc_ref.at[h]))
    _flash_pipeline(i, streams)
    for h in range(ATTN_HEADS_PER_STEP):
        acc = acc_ref[h]
        o_ref[:, h * HEAD_DIM:(h + 1) * HEAD_DIM] = (acc[:, :HEAD_DIM] / acc[:, HEAD_DIM:]).astype(o_ref.dtype)


def _moba_attention(qkv, kmean_pad, casts):
    tq, tk, hp = ATTN_TQ, ATTN_TK, ATTN_HEADS_PER_STEP
    w = hp * HEAD_DIM
    groups = MOBA_HEADS // hp
    assert groups * (SEQ // tq) >= CAST_CHUNKS
    cast_in, cast_out, cast_shape = zip(*[_cast_specs(cw, cl) for cw, cl in casts])
    return pl.pallas_call(
        functools.partial(_moba_kernel, len(casts)),
        grid=(groups, SEQ // tq),
        in_specs=[
            pl.BlockSpec((tq, w), lambda h, i: (i, h)),
            pl.BlockSpec((SEQ, w), lambda h, i: (0, groups + h)),
            pl.BlockSpec((SEQ, w), lambda h, i: (0, 2 * groups + h)),
            pl.BlockSpec((HEAD_DIM, w), lambda h, i: (0, h)),
            *cast_in,
        ],
        out_specs=[pl.BlockSpec((tq, w), lambda h, i: (i, h)), *cast_out],
        out_shape=[jax.ShapeDtypeStruct((SEQ, MOBA_WIDTH), jnp.bfloat16), *cast_shape],
        scratch_shapes=[
            pltpu.VMEM((hp, tq, 2 * HEAD_DIM), jnp.bfloat16),
            pltpu.VMEM((hp, tq, tk), jnp.float32),
            pltpu.VMEM((hp, tq, tk), jnp.float32),
            pltpu.VMEM((hp, tq, HEAD_DIM), jnp.float32),
            pltpu.VMEM((hp, tq, 2 * HEAD_DIM), jnp.float32),
        ],
        compiler_params=_cparams(("arbitrary", "arbitrary")),
        name="moba_attention",
    )(qkv, qkv, qkv, kmean_pad, *[cw for cw, _ in casts])


def _diff_stream(i, q, k_ref, v_ref, q2_ref, sa_ref, sb_ref, m_ref, acc_ref):
    tq, tk = ATTN_TQ, ATTN_TK
    q = q * (DIFF_QK_DIM ** -0.5)
    lane = lax.broadcasted_iota(jnp.int32, (tq, HEAD_DIM), 1)
    zero = jnp.zeros_like(q)
    q2_ref[:tq, :] = jnp.where(lane < DIFF_QK_DIM, q, zero)
    q2_ref[tq:, :] = jnp.where(lane >= DIFF_QK_DIM, q, zero)

    def causal_mask(s):
        r = lax.broadcasted_iota(jnp.int32, s.shape, 0)
        r = jnp.where(r >= tq, r - tq, r)
        col = lax.broadcasted_iota(jnp.int32, s.shape, 1)
        return jnp.where(col <= r, s, NEG_INF)

    def qk_past(j, s_ref):
        s_ref[...] = _scores(q2_ref[...], k_ref[pl.ds(pl.multiple_of(j * tk, tk), tk), :])

    def v_tile(j):
        return _with_ones(v_ref[pl.ds(pl.multiple_of(j * tk, tk), tk), :])

    return _Stream(functools.partial(qk_past, i), qk_past, v_tile, causal_mask,
                   sa_ref, sb_ref, m_ref, acc_ref, LOG2E)


def _diff_kernel(lam_init, n_cast, q_ref, k_ref, v_ref, lp_ref, g_ref, *refs):
    cast_src, (o_ref, *cast_dst) = refs[:n_cast], refs[n_cast:2 * n_cast + 1]
    q2_ref, sa_ref, sb_ref, m_ref, acc_ref = refs[2 * n_cast + 1:]
    _cast_chunks(cast_src, cast_dst)
    i = pl.program_id(1)
    tq = ATTN_TQ
    streams = []
    for h in range(ATTN_HEADS_PER_STEP):
        sl = slice(h * HEAD_DIM, (h + 1) * HEAD_DIM)
        streams.append(_diff_stream(i, q_ref[:, sl], k_ref.at[:, sl], v_ref.at[:, sl],
                                    q2_ref.at[h], sa_ref.at[h], sb_ref.at[h], m_ref.at[h], acc_ref.at[h]))
    _flash_pipeline(i, streams)

    lp = lp_ref[...]
    lam = (jnp.exp(jnp.sum(lp[0:1] * lp[1:2], axis=-1, keepdims=True))
           - jnp.exp(jnp.sum(lp[2:3] * lp[3:4], axis=-1, keepdims=True)) + lam_init)
    for h in range(ATTN_HEADS_PER_STEP):
        acc = acc_ref[h]
        o = acc[:, :HEAD_DIM] / acc[:, HEAD_DIM:]
        o = o[:tq] - lam * o[tq:]
        o = _rms_norm_rows(o, g_ref[...]) * (1.0 - lam_init)
        o_ref[:, h * HEAD_DIM:(h + 1) * HEAD_DIM] = o.astype(o_ref.dtype)


def _diff_attention(qkv, lam_params, subln_g, lam_init, casts):
    tq, tk, hp = ATTN_TQ, ATTN_TK, ATTN_HEADS_PER_STEP
    w = hp * HEAD_DIM
    groups = DIFF_HEADS // hp
    base = 3 * MOBA_WIDTH // w
    assert groups * (SEQ // tq) >= CAST_CHUNKS
    cast_in, cast_out, cast_shape = zip(*[_cast_specs(cw, cl) for cw, cl in casts])
    return pl.pallas_call(
        functools.partial(_diff_kernel, lam_init, len(casts)),
        grid=(groups, SEQ // tq),
        in_specs=[
            pl.BlockSpec((tq, w), lambda h, i: (i, base + h)),
            pl.BlockSpec((SEQ, w), lambda h, i: (0, base + groups + h)),
            pl.BlockSpec((SEQ, w), lambda h, i: (0, base + 2 * groups + h)),
            pl.BlockSpec((4, DIFF_QK_DIM), lambda h, i: (0, 0)),
            pl.BlockSpec((1, HEAD_DIM), lambda h, i: (0, 0)),
            *cast_in,
        ],
        out_specs=[pl.BlockSpec((tq, w), lambda h, i: (i, h)), *cast_out],
        out_shape=[jax.ShapeDtypeStruct((SEQ, DIFF_WIDTH), jnp.bfloat16), *cast_shape],
        scratch_shapes=[
            pltpu.VMEM((hp, 2 * tq, HEAD_DIM), jnp.bfloat16),
            pltpu.VMEM((hp, 2 * tq, tk), jnp.float32),
            pltpu.VMEM((hp, 2 * tq, tk), jnp.float32),
            pltpu.VMEM((hp, 2 * tq, HEAD_DIM), jnp.float32),
            pltpu.VMEM((hp, 2 * tq, 2 * HEAD_DIM), jnp.float32),
        ],
        compiler_params=_cparams(("arbitrary", "arbitrary")),
        name="diff_attention",
    )(qkv, qkv, qkv, lam_params, subln_g, *[cw for cw, _ in casts])


def _out_proj_kernel(x_ref, a_ref, b_ref, c_ref, w_ref, o_ref):
    acc = jnp.dot(a_ref[...], w_ref[0:MOBA_WIDTH, :], preferred_element_type=jnp.float32)
    acc += jnp.dot(b_ref[...], w_ref[MOBA_WIDTH:MOBA_WIDTH + DIFF_WIDTH, :],
                   preferred_element_type=jnp.float32)
    acc += jnp.dot(c_ref[...], w_ref[MOBA_WIDTH + DIFF_WIDTH:, :], preferred_element_type=jnp.float32)
    o_ref[...] = x_ref[...] + acc


def _out_proj(x, moba_o, diff_o, sgu_o, w_out):
    tm = OUT_TM
    return pl.pallas_call(
        _out_proj_kernel,
        grid=(SEQ // tm,),
        in_specs=[
            pl.BlockSpec((tm, D_MODEL), lambda i: (i, 0)),
            pl.BlockSpec((tm, MOBA_WIDTH), lambda i: (i, 0)),
            pl.BlockSpec((tm, DIFF_WIDTH), lambda i: (i, 0)),
            pl.BlockSpec((tm, SGU_WIDTH), lambda i: (i, 0)),
            pl.BlockSpec((D_MODEL, D_MODEL), lambda i: (0, 0)),
        ],
        out_specs=pl.BlockSpec((tm, D_MODEL), lambda i: (i, 0)),
        out_shape=jax.ShapeDtypeStruct((SEQ, D_MODEL), jnp.float32),
        compiler_params=_cparams(("arbitrary",)),
        name="out_proj",
    )(x, moba_o, diff_o, sgu_o, w_out)


def _mlp_kernel(final, x_ref, g_ref, w1_ref, w2_ref, gf_ref, o_ref, h_ref, acc_ref):
    j = pl.program_id(1)

    @pl.when(j == 0)
    def _():
        h_ref[...] = _rms_norm_rows(x_ref[...], g_ref[...]).astype(jnp.bfloat16)
        acc_ref[...] = jnp.zeros(acc_ref.shape, jnp.float32)

    a = jnp.dot(h_ref[...], w1_ref[...], preferred_element_type=jnp.float32)
    a = jnp.square(jnp.maximum(a, 0.0)).astype(jnp.bfloat16)
    acc_ref[...] += jnp.dot(a, w2_ref[...], preferred_element_type=jnp.float32)

    @pl.when(j == pl.num_programs(1) - 1)
    def _():
        y = x_ref[...] + acc_ref[...]
        if final:
            y = _rms_norm_rows(y, gf_ref[...])
        o_ref[...] = y


def _mlp(x, g, w1, w2, g_final, final):
    tm, tf = MLP_TM, MLP_TF
    return pl.pallas_call(
        functools.partial(_mlp_kernel, final),
        grid=(SEQ // tm, D_FF // tf),
        in_specs=[
            pl.BlockSpec((tm, D_MODEL), lambda i, j: (i, 0)),
            pl.BlockSpec((1, D_MODEL), lambda i, j: (0, 0)),
            pl.BlockSpec((D_MODEL, tf), lambda i, j: (0, j)),
            pl.BlockSpec((tf, D_MODEL), lambda i, j: (j, 0)),
            pl.BlockSpec((1, D_MODEL), lambda i, j: (0, 0)),
        ],
        out_specs=pl.BlockSpec((tm, D_MODEL), lambda i, j: (i, 0)),
        out_shape=jax.ShapeDtypeStruct((SEQ, D_MODEL), jnp.float32),
        scratch_shapes=[
            pltpu.VMEM((tm, D_MODEL), jnp.bfloat16),
            pltpu.VMEM((tm, D_MODEL), jnp.float32),
        ],
        compiler_params=_cparams(("arbitrary", "arbitrary")),
        name="mlp",
    )(x, g, w1, w2, g_final)


def _rope_tables():
    pos = jnp.arange(SEQ, dtype=jnp.float32)[:, None]

    def tab(dim):
        inv = 1.0 / (ROPE_THETA ** (jnp.arange(0, dim, 2, dtype=jnp.float32) / dim))
        ang = pos * inv[None, :]
        return jnp.cos(ang), jnp.sin(ang)

    cm, sm = tab(HEAD_DIM)
    cd, sd = tab(DIFF_QK_DIM)
    zd = jnp.zeros_like(sd)
    return (jnp.concatenate([cm, cm], axis=1), jnp.concatenate([-sm, sm], axis=1),
            jnp.concatenate([cd, cd, cd, cd], axis=1),
            jnp.concatenate([-sd, zd, -sd, zd], axis=1),
            jnp.concatenate([zd, sd, zd, sd], axis=1))


def kernel(x, attn_norm_g, w_in, diff_lambda, diff_subln_g, sgu_ln_g, sgu_ln_b, sgu_w, sgu_b,
           w_out, mlp_norm_g, w_mlp_in, w_mlp_out, final_norm_g):
    assert x.shape == (1, SEQ, D_MODEL)
    bf = jnp.bfloat16
    xs = x.reshape(SEQ, D_MODEL)
    cm, sm, cd, sdl, sdh = _rope_tables()
    gf = final_norm_g.reshape(1, D_MODEL)
    w_in_l = w_in[0].astype(bf)
    for l in range(DEPTH):
        lam_init = 0.8 - 0.6 * math.exp(-0.3 * l)
        g_attn = attn_norm_g[l].reshape(1, D_MODEL)
        qkv, kmean, sgu_o = _in_proj(xs, g_attn, w_in_l, cm, sm, cd, sdl, sdh, sgu_ln_g[l], sgu_ln_b[l],
                                     sgu_w[l], sgu_b[l].reshape(SGU_GROUPS, SGU_CHUNK, 1))
        kmean = jnp.pad(kmean.reshape(N_BLOCKS, MOBA_WIDTH), ((0, HEAD_DIM - N_BLOCKS), (0, 0)))
        moba_o, w1_l, w_out_l = _moba_attention(qkv, kmean, [(w_mlp_in, l), (w_out, l)])
        diff_casts = [(w_mlp_out, l)] + ([(w_in, l + 1)] if l + 1 < DEPTH else [])
        diff_o, w2_l, *w_in_next = _diff_attention(qkv, diff_lambda[l], diff_subln_g[l].reshape(1, HEAD_DIM),
                                                   lam_init, diff_casts)
        xs = _out_proj(xs, moba_o, diff_o, sgu_o, w_out_l)
        xs = _mlp(xs, mlp_norm_g[l].reshape(1, D_MODEL), w1_l, w2_l, gf, l == DEPTH - 1)
        if w_in_next:
            w_in_l = w_in_next[0]
    return xs.reshape(1, SEQ, D_MODEL)
```

```python
import functools
import math
from typing import Callable, NamedTuple

import jax
import jax.numpy as jnp
from jax import lax
from jax.experimental import pallas as pl
from jax.experimental.pallas import tpu as pltpu

D_MODEL = 2048
SEQ = 8192
DEPTH = 4
HEAD_DIM = 128
MOBA_HEADS = 6
DIFF_HEADS = 6
SGU_GROUPS = 4
MOBA_WIDTH = MOBA_HEADS * HEAD_DIM
DIFF_WIDTH = DIFF_HEADS * HEAD_DIM
SGU_WIDTH = SGU_GROUPS * HEAD_DIM
DIFF_QK_DIM = HEAD_DIM // 2
MOBA_BLOCK = 256
MOBA_TOPK = 3
SGU_CHUNK = 128
D_FF = 4 * D_MODEL
ROPE_THETA = 10000.0
EPS = 1e-6
QKV_WIDTH = 3 * MOBA_WIDTH + 3 * DIFF_WIDTH
N_BLOCKS = SEQ // MOBA_BLOCK

VMEM_LIMIT_BYTES = 56 * 1024 * 1024

PROJ_TM = 512
PROJ_TN = 768
ATTN_TQ = 2 * MOBA_BLOCK
ATTN_TK = 2 * MOBA_BLOCK
ATTN_HEADS_PER_STEP = 2
CAST_CHUNKS = 32
OUT_TM = 512
MLP_TM = 512
MLP_TF = 1024

NEG_INF = float("-inf")
NEG_BIG = -1e30
LOG2E = 1.4426950408889634


def _cparams(sem):
    return pltpu.CompilerParams(dimension_semantics=sem, vmem_limit_bytes=VMEM_LIMIT_BYTES)


def _rms_norm_rows(x, g):
    return x * lax.rsqrt(jnp.mean(x * x, axis=-1, keepdims=True) + EPS) * g


def _gelu_tanh(x):
    c = math.sqrt(2.0 / math.pi)
    return 0.5 * x * (1.0 + jnp.tanh(c * (x + 0.044715 * (x * x * x))))


def _in_proj_kernel(x_ref, g_ref, w_ref, cm_ref, sm_ref, cd_ref, sdl_ref, sdh_ref,
                    lng_ref, lnb_ref, ws_ref, bs_ref, o_ref, kmean_ref, sgu_ref):
    h = _rms_norm_rows(x_ref[...], g_ref[...]).astype(jnp.bfloat16)

    def rope_moba(xh):
        return xh * cm_ref[...] + pltpu.roll(xh, HEAD_DIM // 2, 1) * sm_ref[...]

    def rope_diff(xh):
        return (xh * cd_ref[...]
                + pltpu.roll(xh, HEAD_DIM - DIFF_QK_DIM // 2, 1) * sdl_ref[...]
                + pltpu.roll(xh, DIFF_QK_DIM // 2, 1) * sdh_ref[...])

    epilogues = (rope_moba, rope_moba, None, rope_diff, rope_diff, None)
    for grp, fn in enumerate(epilogues):
        acc = jnp.dot(h, w_ref[:, grp * PROJ_TN:(grp + 1) * PROJ_TN], preferred_element_type=jnp.float32)
        for hd in range(PROJ_TN // HEAD_DIM):
            sl = slice(hd * HEAD_DIM, (hd + 1) * HEAD_DIM)
            r = acc[:, sl] if fn is None else fn(acc[:, sl])
            o_ref[:, grp * PROJ_TN + hd * HEAD_DIM:grp * PROJ_TN + (hd + 1) * HEAD_DIM] = r.astype(o_ref.dtype)
            if grp == 1:
                for b in range(PROJ_TM // MOBA_BLOCK):
                    blk = r[b * MOBA_BLOCK:(b + 1) * MOBA_BLOCK]
                    kmean_ref[0, b:b + 1, sl] = jnp.mean(blk, axis=0, keepdims=True)

    us = jnp.dot(h, w_ref[:, QKV_WIDTH:QKV_WIDTH + SGU_WIDTH], preferred_element_type=jnp.float32)
    vs = jnp.dot(h, w_ref[:, QKV_WIDTH + SGU_WIDTH:], preferred_element_type=jnp.float32)
    row = lax.broadcasted_iota(jnp.int32, (SGU_CHUNK, SGU_CHUNK), 0)
    col = lax.broadcasted_iota(jnp.int32, (SGU_CHUNK, SGU_CHUNK), 1)
    for g in range(SGU_GROUPS):
        sl = slice(g * HEAD_DIM, (g + 1) * HEAD_DIM)
        u = _gelu_tanh(us[:, sl])
        v = _gelu_tanh(vs[:, sl])
        mu = jnp.mean(v, axis=-1, keepdims=True)
        vc = v - mu
        var = jnp.mean(vc * vc, axis=-1, keepdims=True)
        vn = (vc * lax.rsqrt(var + EPS) * lng_ref[g:g + 1, :] + lnb_ref[g:g + 1, :]).astype(jnp.bfloat16)
        w = jnp.where(row >= col, ws_ref[g], 0.0).astype(jnp.bfloat16)
        bias = bs_ref[g]
        for c in range(PROJ_TM // SGU_CHUNK):
            rs = slice(c * SGU_CHUNK, (c + 1) * SGU_CHUNK)
            mixed = jnp.dot(w, vn[rs], preferred_element_type=jnp.float32) + bias
            sgu_ref[rs, sl] = (u[rs] * mixed).astype(sgu_ref.dtype)


def _in_proj(x, g, w_in, cm, sm, cd, sdl, sdh, ln_g, ln_b, w_s, b_s):
    tm = PROJ_TM
    row_tab = pl.BlockSpec((tm, HEAD_DIM), lambda i: (i, 0))
    full = lambda shape: pl.BlockSpec(shape, lambda i: (0,) * len(shape))
    return pl.pallas_call(
        _in_proj_kernel,
        grid=(SEQ // tm,),
        in_specs=[
            pl.BlockSpec((tm, D_MODEL), lambda i: (i, 0)),
            pl.BlockSpec((1, D_MODEL), lambda i: (0, 0)),
            pl.BlockSpec((D_MODEL, QKV_WIDTH + 2 * SGU_WIDTH), lambda i: (0, 0), pipeline_mode=pl.Buffered(1)),
            row_tab, row_tab, row_tab, row_tab, row_tab,
            full((SGU_GROUPS, HEAD_DIM)),
            full((SGU_GROUPS, HEAD_DIM)),
            full((SGU_GROUPS, SGU_CHUNK, SGU_CHUNK)),
            full((SGU_GROUPS, SGU_CHUNK, 1)),
        ],
        out_specs=[
            pl.BlockSpec((tm, QKV_WIDTH), lambda i: (i, 0)),
            pl.BlockSpec((1, tm // MOBA_BLOCK, MOBA_WIDTH), lambda i: (i, 0, 0)),
            pl.BlockSpec((tm, SGU_WIDTH), lambda i: (i, 0)),
        ],
        out_shape=[
            jax.ShapeDtypeStruct((SEQ, QKV_WIDTH), jnp.bfloat16),
            jax.ShapeDtypeStruct((SEQ // tm, tm // MOBA_BLOCK, MOBA_WIDTH), jnp.float32),
            jax.ShapeDtypeStruct((SEQ, SGU_WIDTH), jnp.bfloat16),
        ],
        compiler_params=_cparams(("arbitrary",)),
        name="in_proj",
    )(x, g, w_in, cm, sm, cd, sdl, sdh, ln_g, ln_b, w_s, b_s)


class _Stream(NamedTuple):
    qk_own: Callable
    qk_past: Callable
    v_tile: Callable
    own_mask: Callable
    sa_ref: object
    sb_ref: object
    m_ref: object
    acc_ref: object
    c: float


def _scores(q, k):
    return lax.dot_general(q, k, (((1,), (1,)), ((), ())), preferred_element_type=jnp.float32)


def _with_ones(v):
    return jnp.concatenate([v, jnp.ones(v.shape, v.dtype)], axis=1)


def _lane_tile(x, n):
    return jnp.concatenate([x] * n, axis=1)


def _softmax_pv(st, s_ref, j, mask_fn=None):
    s = s_ref[...]
    n_rep = s.shape[1] // HEAD_DIM
    if mask_fn is not None:
        s = mask_fn(s)
        m_new = jnp.max(s, axis=-1, keepdims=True)
        p = jnp.exp2((s - m_new) * st.c).astype(jnp.bfloat16)
        st.acc_ref[...] = jnp.dot(p, st.v_tile(j), preferred_element_type=jnp.float32)
        st.m_ref[...] = jnp.broadcast_to(m_new, st.m_ref.shape)
        return
    m_prev = st.m_ref[...]
    m_new = jnp.maximum(m_prev, jnp.max(s, axis=-1, keepdims=True))
    alpha = jnp.exp2((m_prev - m_new) * st.c)
    p = jnp.exp2((s - _lane_tile(m_new, n_rep)) * st.c).astype(jnp.bfloat16)
    st.acc_ref[...] = (_lane_tile(alpha, 2) * st.acc_ref[...]
                       + jnp.dot(p, st.v_tile(j), preferred_element_type=jnp.float32))
    st.m_ref[...] = m_new


def _flash_pipeline(n_past, streams):
    for st in streams:
        st.qk_own(st.sa_ref)
    for st in streams:
        st.qk_past(0, st.sb_ref)
        _softmax_pv(st, st.sa_ref, n_past, st.own_mask)

    def pair(jj, carry):
        j = 2 * jj
        for st in streams:
            st.qk_past(j + 1, st.sa_ref)
            _softmax_pv(st, st.sb_ref, j)
        for st in streams:
            st.qk_past(j + 2, st.sb_ref)
            _softmax_pv(st, st.sa_ref, j + 1)
        return carry

    lax.fori_loop(0, n_past // 2, pair, 0)

    @pl.when(n_past % 2 == 1)
    def _():
        for st in streams:
            _softmax_pv(st, st.sb_ref, n_past - 1)


def _cast_specs(w, layer):
    rows, cols = w.shape[1], w.shape[2]
    chunk = rows // CAST_CHUNKS
    assert chunk * CAST_CHUNKS == rows and chunk % 16 == 0
    steps_per_group = SEQ // ATTN_TQ

    def chunk_index(h, i):
        return jnp.minimum(h * steps_per_group + i, CAST_CHUNKS - 1)

    return (pl.BlockSpec((None, chunk, cols), lambda h, i: (layer, chunk_index(h, i), 0)),
            pl.BlockSpec((chunk, cols), lambda h, i: (chunk_index(h, i), 0)),
            jax.ShapeDtypeStruct((rows, cols), jnp.bfloat16))


def _cast_chunks(src_refs, dst_refs):
    for src, dst in zip(src_refs, dst_refs, strict=True):
        dst[...] = src[...].astype(dst.dtype)


def _moba_stream(i, q, k_ref, v_ref, kmean, qa_ref, sa_ref, sb_ref, m_ref, acc_ref):
    tq, tk = ATTN_TQ, ATTN_TK
    k_hi = kmean.astype(jnp.bfloat16)
    rest = kmean - k_hi.astype(jnp.float32)
    k_mid = rest.astype(jnp.bfloat16)
    k_lo = (rest - k_mid.astype(jnp.float32)).astype(jnp.bfloat16)
    gate = _scores(k_hi, q) + _scores(k_mid, q) + _scores(k_lo, q)
    blk = lax.broadcasted_iota(jnp.int32, (N_BLOCKS, tq), 0)
    own = 2 * i + jnp.where(lax.broadcasted_iota(jnp.int32, (N_BLOCKS, tq), 1) >= MOBA_BLOCK, 1, 0)
    gate = jnp.where(blk < own, gate, NEG_INF)
    blk_f = blk.astype(jnp.float32)
    sel = jnp.zeros((N_BLOCKS, tq), jnp.float32)
    for _ in range(MOBA_TOPK):
        mx = jnp.max(gate, axis=0, keepdims=True)
        cand = jnp.where((gate == mx) & (gate > NEG_INF), blk_f, float(N_BLOCKS))
        pick = blk_f == jnp.min(cand, axis=0, keepdims=True)
        sel = jnp.where(pick, 1.0, sel)
        gate = jnp.where(pick, NEG_INF, gate)
    sel = jnp.concatenate([sel, jnp.zeros((HEAD_DIM - N_BLOCKS, tq), jnp.float32)], axis=0).T
    qa_ref[:, :HEAD_DIM] = q
    qa_ref[:, HEAD_DIM:] = jnp.where(sel > 0.0, 0.0, NEG_BIG).astype(jnp.bfloat16)
    lane = lax.broadcasted_iota(jnp.int32, (tq, HEAD_DIM), 1)
    picked_first = jnp.sum(jnp.where(lane == 2 * i, sel, 0.0), axis=-1, keepdims=True) > 0.0

    def own_mask(s):
        r = lax.broadcasted_iota(jnp.int32, s.shape, 0)
        col = lax.broadcasted_iota(jnp.int32, s.shape, 1)
        rr = lax.broadcasted_iota(jnp.int32, (tq, 1), 0)
        first_col = jnp.where(picked_first | (rr < MOBA_BLOCK), 0, MOBA_BLOCK)
        return jnp.where(col <= r, jnp.where(col >= first_col, s, NEG_INF), NEG_INF)

    def qk_own(s_ref):
        s_ref[...] = _scores(qa_ref[:, :HEAD_DIM], k_ref[pl.ds(pl.multiple_of(i * tk, tk), tk), :])

    def qk_past(j, s_ref):
        erow = lax.broadcasted_iota(jnp.int32, (tk, HEAD_DIM), 0)
        elane = lax.broadcasted_iota(jnp.int32, (tk, HEAD_DIM), 1)
        onehot = jnp.where(elane == 2 * j + jnp.where(erow >= MOBA_BLOCK, 1, 0), 1.0, 0.0).astype(jnp.bfloat16)
        ka = jnp.concatenate([k_ref[pl.ds(pl.multiple_of(j * tk, tk), tk), :], onehot], axis=1)
        s_ref[...] = _scores(qa_ref[...], ka)

    def v_tile(j):
        return _with_ones(v_ref[pl.ds(pl.multiple_of(j * tk, tk), tk), :])

    return _Stream(qk_own, qk_past, v_tile, own_mask, sa_ref, sb_ref, m_ref, acc_ref,
                   (HEAD_DIM ** -0.5) * LOG2E)


def _moba_kernel(n_cast, q_ref, k_ref, v_ref, kmean_ref, *refs):
    cast_src, (o_ref, *cast_dst) = refs[:n_cast], refs[n_cast:2 * n_cast + 1]
    qa_ref, sa_ref, sb_ref, m_ref, acc_ref = refs[2 * n_cast + 1:]
    _cast_chunks(cast_src, cast_dst)
    i = pl.program_id(1)
    streams = []
    for h in range(ATTN_HEADS_PER_STEP):
        sl = slice(h * HEAD_DIM, (h + 1) * HEAD_DIM)
        streams.append(_moba_stream(i, q_ref[:, sl], k_ref.at[:, sl], v_ref.at[:, sl], kmean_ref[:, sl],
                                    qa_ref.at[h], sa_ref.at[h], sb_ref.at[h], m_ref.at[h], acc_ref.at[h]))
    _flash_pipeline(i, streams)
    for h in range(ATTN_HEADS_PER_STEP):
        acc = acc_ref[h]
        o_ref[:, h * HEAD_DIM:(h + 1) * HEAD_DIM] = (acc[:, :HEAD_DIM] / acc[:, HEAD_DIM:]).astype(o_ref.dtype)


def _moba_attention(qkv, kmean, casts):
    tq, tk, hp = ATTN_TQ, ATTN_TK, ATTN_HEADS_PER_STEP
    w = hp * HEAD_DIM
    groups = MOBA_HEADS // hp
    assert groups * (SEQ // tq) >= CAST_CHUNKS
    cast_in, cast_out, cast_shape = zip(*[_cast_specs(cw, cl) for cw, cl in casts])
    return pl.pallas_call(
        functools.partial(_moba_kernel, len(casts)),
        grid=(groups, SEQ // tq),
        in_specs=[
            pl.BlockSpec((tq, w), lambda h, i: (i, h)),
            pl.BlockSpec((SEQ, w), lambda h, i: (0, groups + h)),
            pl.BlockSpec((SEQ, w), lambda h, i: (0, 2 * groups + h)),
            pl.BlockSpec((N_BLOCKS, w), lambda h, i: (0, h)),
            *cast_in,
        ],
        out_specs=[pl.BlockSpec((tq, w), lambda h, i: (i, h)), *cast_out],
        out_shape=[jax.ShapeDtypeStruct((SEQ, MOBA_WIDTH), jnp.bfloat16), *cast_shape],
        scratch_shapes=[
            pltpu.VMEM((hp, tq, 2 * HEAD_DIM), jnp.bfloat16),
            pltpu.VMEM((hp, tq, tk), jnp.float32),
            pltpu.VMEM((hp, tq, tk), jnp.float32),
            pltpu.VMEM((hp, tq, HEAD_DIM), jnp.float32),
            pltpu.VMEM((hp, tq, 2 * HEAD_DIM), jnp.float32),
        ],
        compiler_params=_cparams(("arbitrary", "arbitrary")),
        name="moba_attention",
    )(qkv, qkv, qkv, kmean, *[cw for cw, _ in casts])


def _diff_stream(i, q, k_ref, v_ref, q2_ref, sa_ref, sb_ref, m_ref, acc_ref):
    tq, tk = ATTN_TQ, ATTN_TK
    q = q * (DIFF_QK_DIM ** -0.5)
    lane = lax.broadcasted_iota(jnp.int32, (tq, HEAD_DIM), 1)
    zero = jnp.zeros_like(q)
    q2_ref[:tq, :] = jnp.where(lane < DIFF_QK_DIM, q, zero)
    q2_ref[tq:, :] = jnp.where(lane >= DIFF_QK_DIM, q, zero)

    def causal_mask(s):
        r = lax.broadcasted_iota(jnp.int32, s.shape, 0)
        r = jnp.where(r >= tq, r - tq, r)
        col = lax.broadcasted_iota(jnp.int32, s.shape, 1)
        return jnp.where(col <= r, s, NEG_INF)

    def qk_past(j, s_ref):
        s_ref[...] = _scores(q2_ref[...], k_ref[pl.ds(pl.multiple_of(j * tk, tk), tk), :])

    def v_tile(j):
        return _with_ones(v_ref[pl.ds(pl.multiple_of(j * tk, tk), tk), :])

    return _Stream(functools.partial(qk_past, i), qk_past, v_tile, causal_mask,
                   sa_ref, sb_ref, m_ref, acc_ref, LOG2E)


def _diff_kernel(lam_init, n_cast, q_ref, k_ref, v_ref, lp_ref, g_ref, *refs):
    cast_src, (o_ref, *cast_dst) = refs[:n_cast], refs[n_cast:2 * n_cast + 1]
    q2_ref, sa_ref, sb_ref, m_ref, acc_ref = refs[2 * n_cast + 1:]
    _cast_chunks(cast_src, cast_dst)
    i = pl.program_id(1)
    tq = ATTN_TQ
    streams = []
    for h in range(ATTN_HEADS_PER_STEP):
        sl = slice(h * HEAD_DIM, (h + 1) * HEAD_DIM)
        streams.append(_diff_stream(i, q_ref[:, sl], k_ref.at[:, sl], v_ref.at[:, sl],
                                    q2_ref.at[h], sa_ref.at[h], sb_ref.at[h], m_ref.at[h], acc_ref.at[h]))
    _flash_pipeline(i, streams)

    lp = lp_ref[...]
    lam = (jnp.exp(jnp.sum(lp[0:1] * lp[1:2], axis=-1, keepdims=True))
           - jnp.exp(jnp.sum(lp[2:3] * lp[3:4], axis=-1, keepdims=True)) + lam_init)
    for h in range(ATTN_HEADS_PER_STEP):
        acc = acc_ref[h]
        o = acc[:, :HEAD_DIM] / acc[:, HEAD_DIM:]
        o = o[:tq] - lam * o[tq:]
        o = _rms_norm_rows(o, g_ref[...]) * (1.0 - lam_init)
        o_ref[:, h * HEAD_DIM:(h + 1) * HEAD_DIM] = o.astype(o_ref.dtype)


def _diff_attention(qkv, lam_params, subln_g, lam_init, casts):
    tq, tk, hp = ATTN_TQ, ATTN_TK, ATTN_HEADS_PER_STEP
    w = hp * HEAD_DIM
    groups = DIFF_HEADS // hp
    base = 3 * MOBA_WIDTH // w
    assert groups * (SEQ // tq) >= CAST_CHUNKS
    cast_in, cast_out, cast_shape = zip(*[_cast_specs(cw, cl) for cw, cl in casts])
    return pl.pallas_call(
        functools.partial(_diff_kernel, lam_init, len(casts)),
        grid=(groups, SEQ // tq),
        in_specs=[
            pl.BlockSpec((tq, w), lambda h, i: (i, base + h)),
            pl.BlockSpec((SEQ, w), lambda h, i: (0, base + groups + h)),
            pl.BlockSpec((SEQ, w), lambda h, i: (0, base + 2 * groups + h)),
            pl.BlockSpec((4, DIFF_QK_DIM), lambda h, i: (0, 0)),
            pl.BlockSpec((1, HEAD_DIM), lambda h, i: (0, 0)),
            *cast_in,
        ],
        out_specs=[pl.BlockSpec((tq, w), lambda h, i: (i, h)), *cast_out],
        out_shape=[jax.ShapeDtypeStruct((SEQ, DIFF_WIDTH), jnp.bfloat16), *cast_shape],
        scratch_shapes=[
            pltpu.VMEM((hp, 2 * tq, HEAD_DIM), jnp.bfloat16),
            pltpu.VMEM((hp, 2 * tq, tk), jnp.float32),
            pltpu.VMEM((hp, 2 * tq, tk), jnp.float32),
            pltpu.VMEM((hp, 2 * tq, HEAD_DIM), jnp.float32),
            pltpu.VMEM((hp, 2 * tq, 2 * HEAD_DIM), jnp.float32),
        ],
        compiler_params=_cparams(("arbitrary", "arbitrary")),
        name="diff_attention",
    )(qkv, qkv, qkv, lam_params, subln_g, *[cw for cw, _ in casts])


def _out_proj_kernel(x_ref, a_ref, b_ref, c_ref, w_ref, o_ref):
    acc = jnp.dot(a_ref[...], w_ref[0:MOBA_WIDTH, :], preferred_element_type=jnp.float32)
    acc += jnp.dot(b_ref[...], w_ref[MOBA_WIDTH:MOBA_WIDTH + DIFF_WIDTH, :],
                   preferred_element_type=jnp.float32)
    acc += jnp.dot(c_ref[...], w_ref[MOBA_WIDTH + DIFF_WIDTH:, :], preferred_element_type=jnp.float32)
    o_ref[...] = x_ref[...] + acc


def _out_proj(x, moba_o, diff_o, sgu_o, w_out):
    tm = OUT_TM
    return pl.pallas_call(
        _out_proj_kernel,
        grid=(SEQ // tm,),
        in_specs=[
            pl.BlockSpec((tm, D_MODEL), lambda i: (i, 0)),
            pl.BlockSpec((tm, MOBA_WIDTH), lambda i: (i, 0)),
            pl.BlockSpec((tm, DIFF_WIDTH), lambda i: (i, 0)),
            pl.BlockSpec((tm, SGU_WIDTH), lambda i: (i, 0)),
            pl.BlockSpec((D_MODEL, D_MODEL), lambda i: (0, 0)),
        ],
        out_specs=pl.BlockSpec((tm, D_MODEL), lambda i: (i, 0)),
        out_shape=jax.ShapeDtypeStruct((SEQ, D_MODEL), jnp.float32),
        compiler_params=_cparams(("arbitrary",)),
        name="out_proj",
    )(x, moba_o, diff_o, sgu_o, w_out)


def _mlp_kernel(final, x_ref, g_ref, w1_ref, w2_ref, gf_ref, o_ref, h_ref, acc_ref):
    j = pl.program_id(1)

    @pl.when(j == 0)
    def _():
        h_ref[...] = _rms_norm_rows(x_ref[...], g_ref[...]).astype(jnp.bfloat16)
        acc_ref[...] = jnp.zeros(acc_ref.shape, jnp.float32)

    a = jnp.dot(h_ref[...], w1_ref[...], preferred_element_type=jnp.float32)
    a = jnp.square(jnp.maximum(a, 0.0)).astype(jnp.bfloat16)
    acc_ref[...] += jnp.dot(a, w2_ref[...], preferred_element_type=jnp.float32)

    @pl.when(j == pl.num_programs(1) - 1)
    def _():
        y = x_ref[...] + acc_ref[...]
        if final:
            y = _rms_norm_rows(y, gf_ref[...])
        o_ref[...] = y


def _mlp(x, g, w1, w2, g_final, final):
    tm, tf = MLP_TM, MLP_TF
    return pl.pallas_call(
        functools.partial(_mlp_kernel, final),
        grid=(SEQ // tm, D_FF // tf),
        in_specs=[
            pl.BlockSpec((tm, D_MODEL), lambda i, j: (i, 0)),
            pl.BlockSpec((1, D_MODEL), lambda i, j: (0, 0)),
            pl.BlockSpec((D_MODEL, tf), lambda i, j: (0, j)),
            pl.BlockSpec((tf, D_MODEL), lambda i, j: (j, 0)),
            pl.BlockSpec((1, D_MODEL), lambda i, j: (0, 0)),
        ],
        out_specs=pl.BlockSpec((tm, D_MODEL), lambda i, j: (i, 0)),
        out_shape=jax.ShapeDtypeStruct((SEQ, D_MODEL), jnp.float32),
        scratch_shapes=[
            pltpu.VMEM((tm, D_MODEL), jnp.bfloat16),
            pltpu.VMEM((tm, D_MODEL), jnp.float32),
        ],
        compiler_params=_cparams(("arbitrary", "arbitrary")),
        name="mlp",
    )(x, g, w1, w2, g_final)


def _rope_tables():
    pos = jnp.arange(SEQ, dtype=jnp.float32)[:, None]

    def tab(dim):
        inv = 1.0 / (ROPE_THETA ** (jnp.arange(0, dim, 2, dtype=jnp.float32) / dim))
        ang = pos * inv[None, :]
        return jnp.cos(ang), jnp.sin(ang)

    cm, sm = tab(HEAD_DIM)
    cd, sd = tab(DIFF_QK_DIM)
    zd = jnp.zeros_like(sd)
    return (jnp.concatenate([cm, cm], axis=1), jnp.concatenate([-sm, sm], axis=1),
            jnp.concatenate([cd, cd, cd, cd], axis=1),
            jnp.concatenate([-sd, zd, -sd, zd], axis=1),
            jnp.concatenate([zd, sd, zd, sd], axis=1))


def kernel(x, attn_norm_g, w_in, diff_lambda, diff_subln_g, sgu_ln_g, sgu_ln_b, sgu_w, sgu_b,
           w_out, mlp_norm_g, w_mlp_in, w_mlp_out, final_norm_g):
    assert x.shape == (1, SEQ, D_MODEL)
    bf = jnp.bfloat16
    xs = x.reshape(SEQ, D_MODEL)
    cm, sm, cd, sdl, sdh = _rope_tables()
    gf = final_norm_g.reshape(1, D_MODEL)
    w_in_l = w_in[0].astype(bf)
    for l in range(DEPTH):
        lam_init = 0.8 - 0.6 * math.exp(-0.3 * l)
        g_attn = attn_norm_g[l].reshape(1, D_MODEL)
        qkv, kmean, sgu_o = _in_proj(xs, g_attn, w_in_l, cm, sm, cd, sdl, sdh, sgu_ln_g[l], sgu_ln_b[l],
                                     sgu_w[l], sgu_b[l].reshape(SGU_GROUPS, SGU_CHUNK, 1))
        kmean = kmean.reshape(N_BLOCKS, MOBA_WIDTH)
        moba_o, w1_l, w_out_l = _moba_attention(qkv, kmean, [(w_mlp_in, l), (w_out, l)])
        diff_casts = [(w_mlp_out, l)] + ([(w_in, l + 1)] if l + 1 < DEPTH else [])
        diff_o, w2_l, *w_in_next = _diff_attention(qkv, diff_lambda[l], diff_subln_g[l].reshape(1, HEAD_DIM),
                                                   lam_init, diff_casts)
        xs = _out_proj(xs, moba_o, diff_o, sgu_o, w_out_l)
        xs = _mlp(xs, mlp_norm_g[l].reshape(1, D_MODEL), w1_l, w2_l, gf, l == DEPTH - 1)
        if w_in_next:
            w_in_l = w_in_next[0]
    return xs.reshape(1, SEQ, D_MODEL)
```

```python
import functools
import math
from typing import Callable, NamedTuple

import jax
import jax.numpy as jnp
from jax import lax
from jax.experimental import pallas as pl
from jax.experimental.pallas import tpu as pltpu

D_MODEL = 2048
SEQ = 8192
DEPTH = 4
HEAD_DIM = 128
MOBA_HEADS = 6
DIFF_HEADS = 6
SGU_GROUPS = 4
MOBA_WIDTH = MOBA_HEADS * HEAD_DIM
DIFF_WIDTH = DIFF_HEADS * HEAD_DIM
SGU_WIDTH = SGU_GROUPS * HEAD_DIM
DIFF_QK_DIM = HEAD_DIM // 2
MOBA_BLOCK = 256
MOBA_TOPK = 3
SGU_CHUNK = 128
D_FF = 4 * D_MODEL
ROPE_THETA = 10000.0
EPS = 1e-6
QKV_WIDTH = 3 * MOBA_WIDTH + 3 * DIFF_WIDTH
N_BLOCKS = SEQ // MOBA_BLOCK

VMEM_LIMIT_BYTES = 56 * 1024 * 1024

PROJ_TM = 512
PROJ_TN = 768
ATTN_TQ = 2 * MOBA_BLOCK
ATTN_TK = 2 * MOBA_BLOCK
MOBA_HEADS_PER_STEP = 3
DIFF_HEADS_PER_STEP = 2
CAST_CHUNKS = 32
OUT_TM = 512
MLP_TM = 512
MLP_TF = 1024

NEG_INF = float("-inf")
NEG_BIG = -1e30
LOG2E = 1.4426950408889634


def _cparams(sem):
    return pltpu.CompilerParams(dimension_semantics=sem, vmem_limit_bytes=VMEM_LIMIT_BYTES)


def _rms_norm_rows(x, g):
    return x * lax.rsqrt(jnp.mean(x * x, axis=-1, keepdims=True) + EPS) * g


def _gelu_tanh(x):
    c = math.sqrt(2.0 / math.pi)
    return 0.5 * x * (1.0 + jnp.tanh(c * (x + 0.044715 * (x * x * x))))


def _in_proj_kernel(x_ref, g_ref, w_ref, cm_ref, sm_ref, cd_ref, sdl_ref, sdh_ref,
                    lng_ref, lnb_ref, ws_ref, bs_ref, o_ref, kmean_ref, sgu_ref):
    h = _rms_norm_rows(x_ref[...], g_ref[...]).astype(jnp.bfloat16)

    def rope_moba(xh):
        return xh * cm_ref[...] + pltpu.roll(xh, HEAD_DIM // 2, 1) * sm_ref[...]

    def rope_diff(xh):
        return (xh * cd_ref[...]
                + pltpu.roll(xh, HEAD_DIM - DIFF_QK_DIM // 2, 1) * sdl_ref[...]
                + pltpu.roll(xh, DIFF_QK_DIM // 2, 1) * sdh_ref[...])

    def group_dot(grp):
        return jnp.dot(h, w_ref[:, grp * PROJ_TN:(grp + 1) * PROJ_TN], preferred_element_type=jnp.float32)

    def group_epilogue(grp, acc, fn):
        for hd in range(PROJ_TN // HEAD_DIM):
            sl = slice(hd * HEAD_DIM, (hd + 1) * HEAD_DIM)
            r = acc[:, sl] if fn is None else fn(acc[:, sl])
            o_ref[:, grp * PROJ_TN + hd * HEAD_DIM:grp * PROJ_TN + (hd + 1) * HEAD_DIM] = r.astype(o_ref.dtype)
            if grp == 1:
                for b in range(PROJ_TM // MOBA_BLOCK):
                    blk = r[b * MOBA_BLOCK:(b + 1) * MOBA_BLOCK]
                    kmean_ref[0, b:b + 1, sl] = jnp.mean(blk, axis=0, keepdims=True)

    us = jnp.dot(h, w_ref[:, QKV_WIDTH:QKV_WIDTH + SGU_WIDTH], preferred_element_type=jnp.float32)
    vs = jnp.dot(h, w_ref[:, QKV_WIDTH + SGU_WIDTH:], preferred_element_type=jnp.float32)
    acc_q = group_dot(0)

    row = lax.broadcasted_iota(jnp.int32, (SGU_CHUNK, SGU_CHUNK), 0)
    col = lax.broadcasted_iota(jnp.int32, (SGU_CHUNK, SGU_CHUNK), 1)
    for g in range(SGU_GROUPS):
        sl = slice(g * HEAD_DIM, (g + 1) * HEAD_DIM)
        u = _gelu_tanh(us[:, sl])
        v = _gelu_tanh(vs[:, sl])
        mu = jnp.mean(v, axis=-1, keepdims=True)
        vc = v - mu
        var = jnp.mean(vc * vc, axis=-1, keepdims=True)
        vn = (vc * lax.rsqrt(var + EPS) * lng_ref[g:g + 1, :] + lnb_ref[g:g + 1, :]).astype(jnp.bfloat16)
        w = jnp.where(row >= col, ws_ref[g], 0.0).astype(jnp.bfloat16)
        bias = bs_ref[g]
        for c in range(PROJ_TM // SGU_CHUNK):
            rs = slice(c * SGU_CHUNK, (c + 1) * SGU_CHUNK)
            mixed = jnp.dot(w, vn[rs], preferred_element_type=jnp.float32) + bias
            sgu_ref[rs, sl] = (u[rs] * mixed).astype(sgu_ref.dtype)

    group_epilogue(0, acc_q, rope_moba)
    for grp, fn in ((1, rope_moba), (3, rope_diff), (4, rope_diff), (2, None), (5, None)):
        group_epilogue(grp, group_dot(grp), fn)


def _in_proj(x, g, w_in, cm, sm, cd, sdl, sdh, ln_g, ln_b, w_s, b_s):
    tm = PROJ_TM
    row_tab = pl.BlockSpec((tm, HEAD_DIM), lambda i: (i, 0))
    full = lambda shape: pl.BlockSpec(shape, lambda i: (0,) * len(shape))
    return pl.pallas_call(
        _in_proj_kernel,
        grid=(SEQ // tm,),
        in_specs=[
            pl.BlockSpec((tm, D_MODEL), lambda i: (i, 0)),
            pl.BlockSpec((1, D_MODEL), lambda i: (0, 0)),
            pl.BlockSpec((D_MODEL, QKV_WIDTH + 2 * SGU_WIDTH), lambda i: (0, 0), pipeline_mode=pl.Buffered(1)),
            row_tab, row_tab, row_tab, row_tab, row_tab,
            full((SGU_GROUPS, HEAD_DIM)),
            full((SGU_GROUPS, HEAD_DIM)),
            full((SGU_GROUPS, SGU_CHUNK, SGU_CHUNK)),
            full((SGU_GROUPS, SGU_CHUNK, 1)),
        ],
        out_specs=[
            pl.BlockSpec((tm, QKV_WIDTH), lambda i: (i, 0)),
            pl.BlockSpec((1, tm // MOBA_BLOCK, MOBA_WIDTH), lambda i: (i, 0, 0)),
            pl.BlockSpec((tm, SGU_WIDTH), lambda i: (i, 0)),
        ],
        out_shape=[
            jax.ShapeDtypeStruct((SEQ, QKV_WIDTH), jnp.bfloat16),
            jax.ShapeDtypeStruct((SEQ // tm, tm // MOBA_BLOCK, MOBA_WIDTH), jnp.float32),
            jax.ShapeDtypeStruct((SEQ, SGU_WIDTH), jnp.bfloat16),
        ],
        compiler_params=_cparams(("arbitrary",)),
        name="in_proj",
    )(x, g, w_in, cm, sm, cd, sdl, sdh, ln_g, ln_b, w_s, b_s)


class _Stream(NamedTuple):
    qk_own: Callable
    qk_past: Callable
    v_tile: Callable
    own_mask: Callable
    sa_ref: object
    sb_ref: object
    m_ref: object
    acc_ref: object
    c: float


def _scores(q, k):
    return lax.dot_general(q, k, (((1,), (1,)), ((), ())), preferred_element_type=jnp.float32)


def _with_ones(v):
    return jnp.concatenate([v, jnp.ones(v.shape, v.dtype)], axis=1)


def _lane_tile(x, n):
    return jnp.concatenate([x] * n, axis=1)


def _softmax_pv(st, s_ref, j, mask_fn=None):
    s = s_ref[...]
    n_rep = s.shape[1] // HEAD_DIM
    if mask_fn is not None:
        s = mask_fn(s)
        m_new = jnp.max(s, axis=-1, keepdims=True)
        p = jnp.exp2((s - m_new) * st.c).astype(jnp.bfloat16)
        st.acc_ref[...] = jnp.dot(p, st.v_tile(j), preferred_element_type=jnp.float32)
        st.m_ref[...] = jnp.broadcast_to(m_new, st.m_ref.shape)
        return
    m_prev = st.m_ref[...]
    m_new = jnp.maximum(m_prev, jnp.max(s, axis=-1, keepdims=True))
    alpha = jnp.exp2((m_prev - m_new) * st.c)
    p = jnp.exp2((s - _lane_tile(m_new, n_rep)) * st.c).astype(jnp.bfloat16)
    st.acc_ref[...] = (_lane_tile(alpha, 2) * st.acc_ref[...]
                       + jnp.dot(p, st.v_tile(j), preferred_element_type=jnp.float32))
    st.m_ref[...] = m_new


def _flash_pipeline(n_past, streams):
    for st in streams:
        st.qk_own(st.sa_ref)
    for st in streams:
        st.qk_past(0, st.sb_ref)
        _softmax_pv(st, st.sa_ref, n_past, st.own_mask)

    def pair(jj, carry):
        j = 2 * jj
        for st in streams:
            st.qk_past(j + 1, st.sa_ref)
            _softmax_pv(st, st.sb_ref, j)
        for st in streams:
            st.qk_past(j + 2, st.sb_ref)
            _softmax_pv(st, st.sa_ref, j + 1)
        return carry

    lax.fori_loop(0, n_past // 2, pair, 0)

    @pl.when(n_past % 2 == 1)
    def _():
        for st in streams:
            _softmax_pv(st, st.sb_ref, n_past - 1)


def _cast_specs(w, layer):
    rows, cols = w.shape[1], w.shape[2]
    chunk = rows // CAST_CHUNKS
    assert chunk * CAST_CHUNKS == rows and chunk % 16 == 0
    steps_per_group = SEQ // ATTN_TQ

    def chunk_index(h, i):
        return jnp.minimum(h * steps_per_group + i, CAST_CHUNKS - 1)

    return (pl.BlockSpec((None, chunk, cols), lambda h, i: (layer, chunk_index(h, i), 0)),
            pl.BlockSpec((chunk, cols), lambda h, i: (chunk_index(h, i), 0)),
            jax.ShapeDtypeStruct((rows, cols), jnp.bfloat16))


def _cast_chunks(src_refs, dst_refs):
    for src, dst in zip(src_refs, dst_refs, strict=True):
        dst[...] = src[...].astype(dst.dtype)


def _moba_stream(i, q, k_ref, v_ref, kmean, qa_ref, sa_ref, sb_ref, m_ref, acc_ref):
    tq, tk = ATTN_TQ, ATTN_TK
    k_hi = kmean.astype(jnp.bfloat16)
    rest = kmean - k_hi.astype(jnp.float32)
    k_mid = rest.astype(jnp.bfloat16)
    k_lo = (rest - k_mid.astype(jnp.float32)).astype(jnp.bfloat16)
    gate = _scores(k_hi, q) + _scores(k_mid, q) + _scores(k_lo, q)
    blk = lax.broadcasted_iota(jnp.int32, (N_BLOCKS, tq), 0)
    own = 2 * i + jnp.where(lax.broadcasted_iota(jnp.int32, (N_BLOCKS, tq), 1) >= MOBA_BLOCK, 1, 0)
    gate = jnp.where(blk < own, gate, NEG_INF)
    blk_f = blk.astype(jnp.float32)
    sel = jnp.zeros((N_BLOCKS, tq), jnp.float32)
    for _ in range(MOBA_TOPK):
        mx = jnp.max(gate, axis=0, keepdims=True)
        cand = jnp.where((gate == mx) & (gate > NEG_INF), blk_f, float(N_BLOCKS))
        pick = blk_f == jnp.min(cand, axis=0, keepdims=True)
        sel = jnp.where(pick, 1.0, sel)
        gate = jnp.where(pick, NEG_INF, gate)
    sel = jnp.concatenate([sel, jnp.zeros((HEAD_DIM - N_BLOCKS, tq), jnp.float32)], axis=0).T
    qa_ref[:, :HEAD_DIM] = q
    qa_ref[:, HEAD_DIM:] = jnp.where(sel > 0.0, 0.0, NEG_BIG).astype(jnp.bfloat16)
    lane = lax.broadcasted_iota(jnp.int32, (tq, HEAD_DIM), 1)
    picked_first = jnp.sum(jnp.where(lane == 2 * i, sel, 0.0), axis=-1, keepdims=True) > 0.0

    def own_mask(s):
        r = lax.broadcasted_iota(jnp.int32, s.shape, 0)
        col = lax.broadcasted_iota(jnp.int32, s.shape, 1)
        rr = lax.broadcasted_iota(jnp.int32, (tq, 1), 0)
        first_col = jnp.where(picked_first | (rr < MOBA_BLOCK), 0, MOBA_BLOCK)
        return jnp.where(col <= r, jnp.where(col >= first_col, s, NEG_INF), NEG_INF)

    def qk_own(s_ref):
        s_ref[...] = _scores(qa_ref[:, :HEAD_DIM], k_ref[pl.ds(pl.multiple_of(i * tk, tk), tk), :])

    def qk_past(j, s_ref):
        erow = lax.broadcasted_iota(jnp.int32, (tk, HEAD_DIM), 0)
        elane = lax.broadcasted_iota(jnp.int32, (tk, HEAD_DIM), 1)
        onehot = jnp.where(elane == 2 * j + jnp.where(erow >= MOBA_BLOCK, 1, 0), 1.0, 0.0).astype(jnp.bfloat16)
        ka = jnp.concatenate([k_ref[pl.ds(pl.multiple_of(j * tk, tk), tk), :], onehot], axis=1)
        s_ref[...] = _scores(qa_ref[...], ka)

    def v_tile(j):
        return _with_ones(v_ref[pl.ds(pl.multiple_of(j * tk, tk), tk), :])

    return _Stream(qk_own, qk_past, v_tile, own_mask, sa_ref, sb_ref, m_ref, acc_ref,
                   (HEAD_DIM ** -0.5) * LOG2E)


def _moba_kernel(n_cast, q_ref, k_ref, v_ref, kmean_ref, *refs):
    cast_src, (o_ref, *cast_dst) = refs[:n_cast], refs[n_cast:2 * n_cast + 1]
    qa_ref, sa_ref, sb_ref, m_ref, acc_ref = refs[2 * n_cast + 1:]
    _cast_chunks(cast_src, cast_dst)
    i = pl.program_id(1)
    streams = []
    for h in range(MOBA_HEADS_PER_STEP):
        sl = slice(h * HEAD_DIM, (h + 1) * HEAD_DIM)
        streams.append(_moba_stream(i, q_ref[:, sl], k_ref.at[:, sl], v_ref.at[:, sl], kmean_ref[:, sl],
                                    qa_ref.at[h], sa_ref.at[h], sb_ref.at[h], m_ref.at[h], acc_ref.at[h]))
    _flash_pipeline(i, streams)
    for h in range(MOBA_HEADS_PER_STEP):
        acc = acc_ref[h]
        o_ref[:, h * HEAD_DIM:(h + 1) * HEAD_DIM] = (acc[:, :HEAD_DIM] / acc[:, HEAD_DIM:]).astype(o_ref.dtype)


def _moba_attention(qkv, kmean, casts):
    tq, tk, hp = ATTN_TQ, ATTN_TK, MOBA_HEADS_PER_STEP
    w = hp * HEAD_DIM
    groups = MOBA_HEADS // hp
    assert groups * (SEQ // tq) >= CAST_CHUNKS
    cast_in, cast_out, cast_shape = zip(*[_cast_specs(cw, cl) for cw, cl in casts])
    return pl.pallas_call(
        functools.partial(_moba_kernel, len(casts)),
        grid=(groups, SEQ // tq),
        in_specs=[
            pl.BlockSpec((tq, w), lambda h, i: (i, h)),
            pl.BlockSpec((SEQ, w), lambda h, i: (0, groups + h)),
            pl.BlockSpec((SEQ, w), lambda h, i: (0, 2 * groups + h)),
            pl.BlockSpec((N_BLOCKS, w), lambda h, i: (0, h)),
            *cast_in,
        ],
        out_specs=[pl.BlockSpec((tq, w), lambda h, i: (i, h)), *cast_out],
        out_shape=[jax.ShapeDtypeStruct((SEQ, MOBA_WIDTH), jnp.bfloat16), *cast_shape],
        scratch_shapes=[
            pltpu.VMEM((hp, tq, 2 * HEAD_DIM), jnp.bfloat16),
            pltpu.VMEM((hp, tq, tk), jnp.float32),
            pltpu.VMEM((hp, tq, tk), jnp.float32),
            pltpu.VMEM((hp, tq, HEAD_DIM), jnp.float32),
            pltpu.VMEM((hp, tq, 2 * HEAD_DIM), jnp.float32),
        ],
        compiler_params=_cparams(("arbitrary", "arbitrary")),
        name="moba_attention",
    )(qkv, qkv, qkv, kmean, *[cw for cw, _ in casts])


def _diff_stream(i, q, k_ref, v_ref, q2_ref, sa_ref, sb_ref, m_ref, acc_ref):
    tq, tk = ATTN_TQ, ATTN_TK
    q = q * (DIFF_QK_DIM ** -0.5)
    lane = lax.broadcasted_iota(jnp.int32, (tq, HEAD_DIM), 1)
    zero = jnp.zeros_like(q)
    q2_ref[:tq, :] = jnp.where(lane < DIFF_QK_DIM, q, zero)
    q2_ref[tq:, :] = jnp.where(lane >= DIFF_QK_DIM, q, zero)

    def causal_mask(s):
        r = lax.broadcasted_iota(jnp.int32, s.shape, 0)
        r = jnp.where(r >= tq, r - tq, r)
        col = lax.broadcasted_iota(jnp.int32, s.shape, 1)
        return jnp.where(col <= r, s, NEG_INF)

    def qk_past(j, s_ref):
        s_ref[...] = _scores(q2_ref[...], k_ref[pl.ds(pl.multiple_of(j * tk, tk), tk), :])

    def v_tile(j):
        return _with_ones(v_ref[pl.ds(pl.multiple_of(j * tk, tk), tk), :])

    return _Stream(functools.partial(qk_past, i), qk_past, v_tile, causal_mask,
                   sa_ref, sb_ref, m_ref, acc_ref, LOG2E)


def _diff_kernel(lam_init, n_cast, q_ref, k_ref, v_ref, lp_ref, g_ref, *refs):
    cast_src, (o_ref, *cast_dst) = refs[:n_cast], refs[n_cast:2 * n_cast + 1]
    q2_ref, sa_ref, sb_ref, m_ref, acc_ref = refs[2 * n_cast + 1:]
    _cast_chunks(cast_src, cast_dst)
    i = pl.program_id(1)
    tq = ATTN_TQ
    streams = []
    for h in range(DIFF_HEADS_PER_STEP):
        sl = slice(h * HEAD_DIM, (h + 1) * HEAD_DIM)
        streams.append(_diff_stream(i, q_ref[:, sl], k_ref.at[:, sl], v_ref.at[:, sl],
                                    q2_ref.at[h], sa_ref.at[h], sb_ref.at[h], m_ref.at[h], acc_ref.at[h]))
    _flash_pipeline(i, streams)

    lp = lp_ref[...]
    lam = (jnp.exp(jnp.sum(lp[0:1] * lp[1:2], axis=-1, keepdims=True))
           - jnp.exp(jnp.sum(lp[2:3] * lp[3:4], axis=-1, keepdims=True)) + lam_init)
    for h in range(DIFF_HEADS_PER_STEP):
        acc = acc_ref[h]
        o = acc[:, :HEAD_DIM] / acc[:, HEAD_DIM:]
        o = o[:tq] - lam * o[tq:]
        o = _rms_norm_rows(o, g_ref[...]) * (1.0 - lam_init)
        o_ref[:, h * HEAD_DIM:(h + 1) * HEAD_DIM] = o.astype(o_ref.dtype)


def _diff_attention(qkv, lam_params, subln_g, lam_init, casts):
    tq, tk, hp = ATTN_TQ, ATTN_TK, DIFF_HEADS_PER_STEP
    w = hp * HEAD_DIM
    groups = DIFF_HEADS // hp
    base = 3 * MOBA_WIDTH // w
    assert groups * (SEQ // tq) >= CAST_CHUNKS
    cast_in, cast_out, cast_shape = zip(*[_cast_specs(cw, cl) for cw, cl in casts])
    return pl.pallas_call(
        functools.partial(_diff_kernel, lam_init, len(casts)),
        grid=(groups, SEQ // tq),
        in_specs=[
            pl.BlockSpec((tq, w), lambda h, i: (i, base + h)),
            pl.BlockSpec((SEQ, w), lambda h, i: (0, base + groups + h)),
            pl.BlockSpec((SEQ, w), lambda h, i: (0, base + 2 * groups + h)),
            pl.BlockSpec((4, DIFF_QK_DIM), lambda h, i: (0, 0)),
            pl.BlockSpec((1, HEAD_DIM), lambda h, i: (0, 0)),
            *cast_in,
        ],
        out_specs=[pl.BlockSpec((tq, w), lambda h, i: (i, h)), *cast_out],
        out_shape=[jax.ShapeDtypeStruct((SEQ, DIFF_WIDTH), jnp.bfloat16), *cast_shape],
        scratch_shapes=[
            pltpu.VMEM((hp, 2 * tq, HEAD_DIM), jnp.bfloat16),
            pltpu.VMEM((hp, 2 * tq, tk), jnp.float32),
            pltpu.VMEM((hp, 2 * tq, tk), jnp.float32),
            pltpu.VMEM((hp, 2 * tq, HEAD_DIM), jnp.float32),
            pltpu.VMEM((hp, 2 * tq, 2 * HEAD_DIM), jnp.float32),
        ],
        compiler_params=_cparams(("arbitrary", "arbitrary")),
        name="diff_attention",
    )(qkv, qkv, qkv, lam_params, subln_g, *[cw for cw, _ in casts])


def _out_proj_kernel(x_ref, a_ref, b_ref, c_ref, w_ref, g_ref, o_ref, h_ref):
    half = OUT_TM // 2
    for r0 in range(0, OUT_TM, half):
        rows = slice(r0, r0 + half)
        acc = jnp.dot(a_ref[rows, :], w_ref[0:MOBA_WIDTH, :], preferred_element_type=jnp.float32)
        acc += jnp.dot(b_ref[rows, :], w_ref[MOBA_WIDTH:MOBA_WIDTH + DIFF_WIDTH, :],
                       preferred_element_type=jnp.float32)
        acc += jnp.dot(c_ref[rows, :], w_ref[MOBA_WIDTH + DIFF_WIDTH:, :], preferred_element_type=jnp.float32)
        y = x_ref[rows, :] + acc
        o_ref[rows, :] = y
        h_ref[rows, :] = _rms_norm_rows(y, g_ref[...]).astype(h_ref.dtype)


def _out_proj(x, moba_o, diff_o, sgu_o, w_out, g_mlp):
    tm = OUT_TM
    return pl.pallas_call(
        _out_proj_kernel,
        grid=(SEQ // tm,),
        in_specs=[
            pl.BlockSpec((tm, D_MODEL), lambda i: (i, 0)),
            pl.BlockSpec((tm, MOBA_WIDTH), lambda i: (i, 0)),
            pl.BlockSpec((tm, DIFF_WIDTH), lambda i: (i, 0)),
            pl.BlockSpec((tm, SGU_WIDTH), lambda i: (i, 0)),
            pl.BlockSpec((D_MODEL, D_MODEL), lambda i: (0, 0)),
            pl.BlockSpec((1, D_MODEL), lambda i: (0, 0)),
        ],
        out_specs=[
            pl.BlockSpec((tm, D_MODEL), lambda i: (i, 0)),
            pl.BlockSpec((tm, D_MODEL), lambda i: (i, 0)),
        ],
        out_shape=[
            jax.ShapeDtypeStruct((SEQ, D_MODEL), jnp.float32),
            jax.ShapeDtypeStruct((SEQ, D_MODEL), jnp.bfloat16),
        ],
        compiler_params=_cparams(("arbitrary",)),
        name="out_proj",
    )(x, moba_o, diff_o, sgu_o, w_out, g_mlp)


def _mlp_kernel(final, x_ref, h_ref, w1_ref, w2_ref, gf_ref, o_ref):
    j = pl.program_id(1)

    @pl.when(j == 0)
    def _():
        o_ref[...] = x_ref[...]

    a = jnp.dot(h_ref[...], w1_ref[...], preferred_element_type=jnp.float32)
    a = jnp.square(jnp.maximum(a, 0.0)).astype(jnp.bfloat16)
    o_ref[...] += jnp.dot(a, w2_ref[...], preferred_element_type=jnp.float32)

    if final:
        @pl.when(j == pl.num_programs(1) - 1)
        def _():
            o_ref[...] = _rms_norm_rows(o_ref[...], gf_ref[...])


def _mlp(x, h, w1, w2, g_final, final):
    tm, tf = MLP_TM, MLP_TF
    return pl.pallas_call(
        functools.partial(_mlp_kernel, final),
        grid=(SEQ // tm, D_FF // tf),
        in_specs=[
            pl.BlockSpec((tm, D_MODEL), lambda i, j: (i, 0)),
            pl.BlockSpec((tm, D_MODEL), lambda i, j: (i, 0)),
            pl.BlockSpec((D_MODEL, tf), lambda i, j: (0, j)),
            pl.BlockSpec((tf, D_MODEL), lambda i, j: (j, 0)),
            pl.BlockSpec((1, D_MODEL), lambda i, j: (0, 0)),
        ],
        out_specs=pl.BlockSpec((tm, D_MODEL), lambda i, j: (i, 0)),
        out_shape=jax.ShapeDtypeStruct((SEQ, D_MODEL), jnp.float32),
        compiler_params=_cparams(("arbitrary", "arbitrary")),
        name="mlp",
    )(x, h, w1, w2, g_final)


def _rope_tables():
    pos = jnp.arange(SEQ, dtype=jnp.float32)[:, None]

    def tab(dim):
        inv = 1.0 / (ROPE_THETA ** (jnp.arange(0, dim, 2, dtype=jnp.float32) / dim))
        ang = pos * inv[None, :]
        return jnp.cos(ang), jnp.sin(ang)

    cm, sm = tab(HEAD_DIM)
    cd, sd = tab(DIFF_QK_DIM)
    zd = jnp.zeros_like(sd)
    return (jnp.concatenate([cm, cm], axis=1), jnp.concatenate([-sm, sm], axis=1),
            jnp.concatenate([cd, cd, cd, cd], axis=1),
            jnp.concatenate([-sd, zd, -sd, zd], axis=1),
            jnp.concatenate([zd, sd, zd, sd], axis=1))


def kernel(x, attn_norm_g, w_in, diff_lambda, diff_subln_g, sgu_ln_g, sgu_ln_b, sgu_w, sgu_b,
           w_out, mlp_norm_g, w_mlp_in, w_mlp_out, final_norm_g):
    assert x.shape == (1, SEQ, D_MODEL)
    bf = jnp.bfloat16
    xs = x.reshape(SEQ, D_MODEL)
    cm, sm, cd, sdl, sdh = _rope_tables()
    gf = final_norm_g.reshape(1, D_MODEL)
    w_in_l = w_in[0].astype(bf)
    for l in range(DEPTH):
        lam_init = 0.8 - 0.6 * math.exp(-0.3 * l)
        g_attn = attn_norm_g[l].reshape(1, D_MODEL)
        qkv, kmean, sgu_o = _in_proj(xs, g_attn, w_in_l, cm, sm, cd, sdl, sdh, sgu_ln_g[l], sgu_ln_b[l],
                                     sgu_w[l], sgu_b[l].reshape(SGU_GROUPS, SGU_CHUNK, 1))
        kmean = kmean.reshape(N_BLOCKS, MOBA_WIDTH)
        moba_o, w1_l, w_out_l = _moba_attention(qkv, kmean, [(w_mlp_in, l), (w_out, l)])
        diff_casts = [(w_mlp_out, l)] + ([(w_in, l + 1)] if l + 1 < DEPTH else [])
        diff_o, w2_l, *w_in_next = _diff_attention(qkv, diff_lambda[l], diff_subln_g[l].reshape(1, HEAD_DIM),
                                                   lam_init, diff_casts)
        xs, h_mlp = _out_proj(xs, moba_o, diff_o, sgu_o, w_out_l, mlp_norm_g[l].reshape(1, D_MODEL))
        xs = _mlp(xs, h_mlp, w1_l, w2_l, gf, l == DEPTH - 1)
        if w_in_next:
            w_in_l = w_in_next[0]
    return xs.reshape(1, SEQ, D_MODEL)
```

```python
import functools
import math
from typing import Callable, NamedTuple

import jax
import jax.numpy as jnp
from jax import lax
from jax.experimental import pallas as pl
from jax.experimental.pallas import tpu as pltpu

D_MODEL = 2048
SEQ = 8192
DEPTH = 4
HEAD_DIM = 128
MOBA_HEADS = 6
DIFF_HEADS = 6
SGU_GROUPS = 4
MOBA_WIDTH = MOBA_HEADS * HEAD_DIM
DIFF_WIDTH = DIFF_HEADS * HEAD_DIM
SGU_WIDTH = SGU_GROUPS * HEAD_DIM
DIFF_QK_DIM = HEAD_DIM // 2
MOBA_BLOCK = 256
MOBA_TOPK = 3
SGU_CHUNK = 128
D_FF = 4 * D_MODEL
ROPE_THETA = 10000.0
EPS = 1e-6
QKV_WIDTH = 3 * MOBA_WIDTH + 3 * DIFF_WIDTH
N_BLOCKS = SEQ // MOBA_BLOCK

VMEM_LIMIT_BYTES = 56 * 1024 * 1024

PROJ_TM = 512
PROJ_TN = 768
ATTN_TQ = 2 * MOBA_BLOCK
ATTN_TK = 2 * MOBA_BLOCK
MOBA_HEADS_PER_STEP = 3
DIFF_HEADS_PER_STEP = 3
CAST_CHUNKS = 32
OUT_TM = 512
MLP_TM = 512
MLP_TF = 1024

NEG_INF = float("-inf")
NEG_BIG = -1e30
LOG2E = 1.4426950408889634


def _cparams(sem):
    return pltpu.CompilerParams(dimension_semantics=sem, vmem_limit_bytes=VMEM_LIMIT_BYTES)


def _rms_norm_rows(x, g):
    return x * lax.rsqrt(jnp.mean(x * x, axis=-1, keepdims=True) + EPS) * g


def _gelu_tanh(x):
    c = math.sqrt(2.0 / math.pi)
    return 0.5 * x * (1.0 + jnp.tanh(c * (x + 0.044715 * (x * x * x))))


def _in_proj_kernel(x_ref, g_ref, w_ref, cm_ref, sm_ref, cd_ref, sdl_ref, sdh_ref,
                    lng_ref, lnb_ref, ws_ref, bs_ref, o_ref, kmean_ref, sgu_ref):
    h = _rms_norm_rows(x_ref[...], g_ref[...]).astype(jnp.bfloat16)

    def rope_moba(xh):
        return xh * cm_ref[...] + pltpu.roll(xh, HEAD_DIM // 2, 1) * sm_ref[...]

    def rope_diff(xh):
        return (xh * cd_ref[...]
                + pltpu.roll(xh, HEAD_DIM - DIFF_QK_DIM // 2, 1) * sdl_ref[...]
                + pltpu.roll(xh, DIFF_QK_DIM // 2, 1) * sdh_ref[...])

    def group_dot(grp):
        return jnp.dot(h, w_ref[:, grp * PROJ_TN:(grp + 1) * PROJ_TN], preferred_element_type=jnp.float32)

    def group_epilogue(grp, acc, fn):
        for hd in range(PROJ_TN // HEAD_DIM):
            sl = slice(hd * HEAD_DIM, (hd + 1) * HEAD_DIM)
            r = acc[:, sl] if fn is None else fn(acc[:, sl])
            o_ref[:, grp * PROJ_TN + hd * HEAD_DIM:grp * PROJ_TN + (hd + 1) * HEAD_DIM] = r.astype(o_ref.dtype)
            if grp == 1:
                for b in range(PROJ_TM // MOBA_BLOCK):
                    blk = r[b * MOBA_BLOCK:(b + 1) * MOBA_BLOCK]
                    kmean_ref[0, b:b + 1, sl] = jnp.mean(blk, axis=0, keepdims=True)

    us = jnp.dot(h, w_ref[:, QKV_WIDTH:QKV_WIDTH + SGU_WIDTH], preferred_element_type=jnp.float32)
    vs = jnp.dot(h, w_ref[:, QKV_WIDTH + SGU_WIDTH:], preferred_element_type=jnp.float32)
    acc_q = group_dot(0)

    row = lax.broadcasted_iota(jnp.int32, (SGU_CHUNK, SGU_CHUNK), 0)
    col = lax.broadcasted_iota(jnp.int32, (SGU_CHUNK, SGU_CHUNK), 1)
    for g in range(SGU_GROUPS):
        sl = slice(g * HEAD_DIM, (g + 1) * HEAD_DIM)
        u = _gelu_tanh(us[:, sl])
        v = _gelu_tanh(vs[:, sl])
        mu = jnp.mean(v, axis=-1, keepdims=True)
        vc = v - mu
        var = jnp.mean(vc * vc, axis=-1, keepdims=True)
        vn = (vc * lax.rsqrt(var + EPS) * lng_ref[g:g + 1, :] + lnb_ref[g:g + 1, :]).astype(jnp.bfloat16)
        w = jnp.where(row >= col, ws_ref[g], 0.0).astype(jnp.bfloat16)
        bias = bs_ref[g]
        for c in range(PROJ_TM // SGU_CHUNK):
            rs = slice(c * SGU_CHUNK, (c + 1) * SGU_CHUNK)
            mixed = jnp.dot(w, vn[rs], preferred_element_type=jnp.float32) + bias
            sgu_ref[rs, sl] = (u[rs] * mixed).astype(sgu_ref.dtype)

    group_epilogue(0, acc_q, rope_moba)
    for grp, fn in ((1, rope_moba), (3, rope_diff), (4, rope_diff), (2, None), (5, None)):
        group_epilogue(grp, group_dot(grp), fn)


def _in_proj(x, g, w_in, cm, sm, cd, sdl, sdh, ln_g, ln_b, w_s, b_s):
    tm = PROJ_TM
    row_tab = pl.BlockSpec((tm, HEAD_DIM), lambda i: (i, 0))
    full = lambda shape: pl.BlockSpec(shape, lambda i: (0,) * len(shape))
    return pl.pallas_call(
        _in_proj_kernel,
        grid=(SEQ // tm,),
        in_specs=[
            pl.BlockSpec((tm, D_MODEL), lambda i: (i, 0)),
            pl.BlockSpec((1, D_MODEL), lambda i: (0, 0)),
            pl.BlockSpec((D_MODEL, QKV_WIDTH + 2 * SGU_WIDTH), lambda i: (0, 0), pipeline_mode=pl.Buffered(1)),
            row_tab, row_tab, row_tab, row_tab, row_tab,
            full((SGU_GROUPS, HEAD_DIM)),
            full((SGU_GROUPS, HEAD_DIM)),
            full((SGU_GROUPS, SGU_CHUNK, SGU_CHUNK)),
            full((SGU_GROUPS, SGU_CHUNK, 1)),
        ],
        out_specs=[
            pl.BlockSpec((tm, QKV_WIDTH), lambda i: (i, 0)),
            pl.BlockSpec((1, tm // MOBA_BLOCK, MOBA_WIDTH), lambda i: (i, 0, 0)),
            pl.BlockSpec((tm, SGU_WIDTH), lambda i: (i, 0)),
        ],
        out_shape=[
            jax.ShapeDtypeStruct((SEQ, QKV_WIDTH), jnp.bfloat16),
            jax.ShapeDtypeStruct((SEQ // tm, tm // MOBA_BLOCK, MOBA_WIDTH), jnp.float32),
            jax.ShapeDtypeStruct((SEQ, SGU_WIDTH), jnp.bfloat16),
        ],
        compiler_params=_cparams(("arbitrary",)),
        name="in_proj",
    )(x, g, w_in, cm, sm, cd, sdl, sdh, ln_g, ln_b, w_s, b_s)


class _Stream(NamedTuple):
    qk_own: Callable
    qk_past: Callable
    v_tile: Callable
    own_mask: Callable
    sa_ref: object
    sb_ref: object
    m_ref: object
    acc_ref: object
    c: float


def _scores(q, k):
    return lax.dot_general(q, k, (((1,), (1,)), ((), ())), preferred_element_type=jnp.float32)


def _with_ones(v):
    return jnp.concatenate([v, jnp.ones(v.shape, v.dtype)], axis=1)


def _lane_tile(x, n):
    return jnp.concatenate([x] * n, axis=1)


def _softmax_pv(st, s_ref, j, mask_fn=None):
    s = s_ref[...]
    n_rep = s.shape[1] // HEAD_DIM
    if mask_fn is not None:
        s = mask_fn(s)
        m_new = jnp.max(s, axis=-1, keepdims=True)
        p = jnp.exp2((s - m_new) * st.c).astype(jnp.bfloat16)
        st.acc_ref[...] = jnp.dot(p, st.v_tile(j), preferred_element_type=jnp.float32)
        st.m_ref[...] = jnp.broadcast_to(m_new, st.m_ref.shape)
        return
    m_prev = st.m_ref[...]
    m_new = jnp.maximum(m_prev, jnp.max(s, axis=-1, keepdims=True))
    alpha = jnp.exp2((m_prev - m_new) * st.c)
    p = jnp.exp2((s - _lane_tile(m_new, n_rep)) * st.c).astype(jnp.bfloat16)
    st.acc_ref[...] = (_lane_tile(alpha, 2) * st.acc_ref[...]
                       + jnp.dot(p, st.v_tile(j), preferred_element_type=jnp.float32))
    st.m_ref[...] = m_new


def _flash_pipeline(n_past, streams):
    for st in streams:
        st.qk_own(st.sa_ref)
    for st in streams:
        st.qk_past(0, st.sb_ref)
        _softmax_pv(st, st.sa_ref, n_past, st.own_mask)

    def pair(jj, carry):
        j = 2 * jj
        for st in streams:
            st.qk_past(j + 1, st.sa_ref)
            _softmax_pv(st, st.sb_ref, j)
        for st in streams:
            st.qk_past(j + 2, st.sb_ref)
            _softmax_pv(st, st.sa_ref, j + 1)
        return carry

    lax.fori_loop(0, n_past // 2, pair, 0)

    @pl.when(n_past % 2 == 1)
    def _():
        for st in streams:
            _softmax_pv(st, st.sb_ref, n_past - 1)


def _cast_specs(w, layer, col_tile=None):
    rows, cols = w.shape[1], w.shape[2]
    chunk = rows // CAST_CHUNKS
    assert chunk * CAST_CHUNKS == rows and chunk % 16 == 0
    steps_per_group = SEQ // ATTN_TQ

    def chunk_index(h, i):
        return jnp.minimum(h * steps_per_group + i, CAST_CHUNKS - 1)

    in_spec = pl.BlockSpec((None, chunk, cols), lambda h, i: (layer, chunk_index(h, i), 0))
    if col_tile is None:
        return (in_spec, pl.BlockSpec((chunk, cols), lambda h, i: (chunk_index(h, i), 0)),
                jax.ShapeDtypeStruct((rows, cols), jnp.bfloat16))
    n_tiles = cols // col_tile
    return (in_spec, pl.BlockSpec((n_tiles, chunk, col_tile), lambda h, i: (0, chunk_index(h, i), 0)),
            jax.ShapeDtypeStruct((n_tiles, rows, col_tile), jnp.bfloat16))


def _cast_chunks(src_refs, dst_refs):
    for src, dst in zip(src_refs, dst_refs, strict=True):
        if len(dst.shape) == 2:
            dst[...] = src[...].astype(dst.dtype)
        else:
            ct = dst.shape[2]
            for t in range(dst.shape[0]):
                dst[t] = src[:, t * ct:(t + 1) * ct].astype(dst.dtype)


def _moba_stream(i, q, k_ref, v_ref, kmean, qa_ref, sa_ref, sb_ref, m_ref, acc_ref):
    tq, tk = ATTN_TQ, ATTN_TK
    k_hi = kmean.astype(jnp.bfloat16)
    rest = kmean - k_hi.astype(jnp.float32)
    k_mid = rest.astype(jnp.bfloat16)
    k_lo = (rest - k_mid.astype(jnp.float32)).astype(jnp.bfloat16)
    gate = _scores(k_hi, q) + _scores(k_mid, q) + _scores(k_lo, q)
    blk = lax.broadcasted_iota(jnp.int32, (N_BLOCKS, tq), 0)
    own = 2 * i + jnp.where(lax.broadcasted_iota(jnp.int32, (N_BLOCKS, tq), 1) >= MOBA_BLOCK, 1, 0)
    gate = jnp.where(blk < own, gate, NEG_INF)
    blk_f = blk.astype(jnp.float32)
    sel = jnp.zeros((N_BLOCKS, tq), jnp.float32)
    for _ in range(MOBA_TOPK):
        mx = jnp.max(gate, axis=0, keepdims=True)
        cand = jnp.where((gate == mx) & (gate > NEG_INF), blk_f, float(N_BLOCKS))
        pick = blk_f == jnp.min(cand, axis=0, keepdims=True)
        sel = jnp.where(pick, 1.0, sel)
        gate = jnp.where(pick, NEG_INF, gate)
    sel = jnp.concatenate([sel, jnp.zeros((HEAD_DIM - N_BLOCKS, tq), jnp.float32)], axis=0).T
    qa_ref[:, :HEAD_DIM] = q
    qa_ref[:, HEAD_DIM:] = jnp.where(sel > 0.0, 0.0, NEG_BIG).astype(jnp.bfloat16)
    lane = lax.broadcasted_iota(jnp.int32, (tq, HEAD_DIM), 1)
    picked_first = jnp.sum(jnp.where(lane == 2 * i, sel, 0.0), axis=-1, keepdims=True) > 0.0

    def own_mask(s):
        r = lax.broadcasted_iota(jnp.int32, s.shape, 0)
        col = lax.broadcasted_iota(jnp.int32, s.shape, 1)
        rr = lax.broadcasted_iota(jnp.int32, (tq, 1), 0)
        first_col = jnp.where(picked_first | (rr < MOBA_BLOCK), 0, MOBA_BLOCK)
        return jnp.where(col <= r, jnp.where(col >= first_col, s, NEG_INF), NEG_INF)

    def qk_own(s_ref):
        s_ref[...] = _scores(qa_ref[:, :HEAD_DIM], k_ref[pl.ds(pl.multiple_of(i * tk, tk), tk), :])

    def qk_past(j, s_ref):
        erow = lax.broadcasted_iota(jnp.int32, (tk, HEAD_DIM), 0)
        elane = lax.broadcasted_iota(jnp.int32, (tk, HEAD_DIM), 1)
        onehot = jnp.where(elane == 2 * j + jnp.where(erow >= MOBA_BLOCK, 1, 0), 1.0, 0.0).astype(jnp.bfloat16)
        ka = jnp.concatenate([k_ref[pl.ds(pl.multiple_of(j * tk, tk), tk), :], onehot], axis=1)
        s_ref[...] = _scores(qa_ref[...], ka)

    def v_tile(j):
        return _with_ones(v_ref[pl.ds(pl.multiple_of(j * tk, tk), tk), :])

    return _Stream(qk_own, qk_past, v_tile, own_mask, sa_ref, sb_ref, m_ref, acc_ref,
                   (HEAD_DIM ** -0.5) * LOG2E)


def _moba_kernel(n_cast, q_ref, k_ref, v_ref, kmean_ref, *refs):
    cast_src, (o_ref, *cast_dst) = refs[:n_cast], refs[n_cast:2 * n_cast + 1]
    qa_ref, sa_ref, sb_ref, m_ref, acc_ref = refs[2 * n_cast + 1:]
    _cast_chunks(cast_src, cast_dst)
    i = pl.program_id(1)
    streams = []
    for h in range(MOBA_HEADS_PER_STEP):
        sl = slice(h * HEAD_DIM, (h + 1) * HEAD_DIM)
        streams.append(_moba_stream(i, q_ref[:, sl], k_ref.at[:, sl], v_ref.at[:, sl], kmean_ref[:, sl],
                                    qa_ref.at[h], sa_ref.at[h], sb_ref.at[h], m_ref.at[h], acc_ref.at[h]))
    _flash_pipeline(i, streams)
    for h in range(MOBA_HEADS_PER_STEP):
        acc = acc_ref[h]
        o_ref[:, h * HEAD_DIM:(h + 1) * HEAD_DIM] = (acc[:, :HEAD_DIM] / acc[:, HEAD_DIM:]).astype(o_ref.dtype)


def _moba_attention(qkv, kmean, casts):
    tq, tk, hp = ATTN_TQ, ATTN_TK, MOBA_HEADS_PER_STEP
    w = hp * HEAD_DIM
    groups = MOBA_HEADS // hp
    assert groups * (SEQ // tq) >= CAST_CHUNKS
    cast_in, cast_out, cast_shape = zip(*[_cast_specs(*c) for c in casts])
    return pl.pallas_call(
        functools.partial(_moba_kernel, len(casts)),
        grid=(groups, SEQ // tq),
        in_specs=[
            pl.BlockSpec((tq, w), lambda h, i: (i, h)),
            pl.BlockSpec((SEQ, w), lambda h, i: (0, groups + h)),
            pl.BlockSpec((SEQ, w), lambda h, i: (0, 2 * groups + h)),
            pl.BlockSpec((N_BLOCKS, w), lambda h, i: (0, h)),
            *cast_in,
        ],
        out_specs=[pl.BlockSpec((tq, w), lambda h, i: (i, h)), *cast_out],
        out_shape=[jax.ShapeDtypeStruct((SEQ, MOBA_WIDTH), jnp.bfloat16), *cast_shape],
        scratch_shapes=[
            pltpu.VMEM((hp, tq, 2 * HEAD_DIM), jnp.bfloat16),
            pltpu.VMEM((hp, tq, tk), jnp.float32),
            pltpu.VMEM((hp, tq, tk), jnp.float32),
            pltpu.VMEM((hp, tq, HEAD_DIM), jnp.float32),
            pltpu.VMEM((hp, tq, 2 * HEAD_DIM), jnp.float32),
        ],
        compiler_params=_cparams(("arbitrary", "arbitrary")),
        name="moba_attention",
    )(qkv, qkv, qkv, kmean, *[c[0] for c in casts])


def _diff_stream(i, q, k_ref, v_ref, q2_ref, sa_ref, sb_ref, m_ref, acc_ref):
    tq, tk = ATTN_TQ, ATTN_TK
    q = q * (DIFF_QK_DIM ** -0.5)
    lane = lax.broadcasted_iota(jnp.int32, (tq, HEAD_DIM), 1)
    zero = jnp.zeros_like(q)
    q2_ref[:tq, :] = jnp.where(lane < DIFF_QK_DIM, q, zero)
    q2_ref[tq:, :] = jnp.where(lane >= DIFF_QK_DIM, q, zero)

    def causal_mask(s):
        r = lax.broadcasted_iota(jnp.int32, s.shape, 0)
        r = jnp.where(r >= tq, r - tq, r)
        col = lax.broadcasted_iota(jnp.int32, s.shape, 1)
        return jnp.where(col <= r, s, NEG_INF)

    def qk_past(j, s_ref):
        s_ref[...] = _scores(q2_ref[...], k_ref[pl.ds(pl.multiple_of(j * tk, tk), tk), :])

    def v_tile(j):
        return _with_ones(v_ref[pl.ds(pl.multiple_of(j * tk, tk), tk), :])

    return _Stream(functools.partial(qk_past, i), qk_past, v_tile, causal_mask,
                   sa_ref, sb_ref, m_ref, acc_ref, LOG2E)


def _diff_kernel(lam_init, q_ref, k_ref, v_ref, lp_ref, g_ref, o_ref, q2_ref, sa_ref, sb_ref, m_ref, acc_ref):
    i = pl.program_id(1)
    tq = ATTN_TQ
    streams = []
    for h in range(DIFF_HEADS_PER_STEP):
        sl = slice(h * HEAD_DIM, (h + 1) * HEAD_DIM)
        streams.append(_diff_stream(i, q_ref[:, sl], k_ref.at[:, sl], v_ref.at[:, sl],
                                    q2_ref.at[h], sa_ref.at[h], sb_ref.at[h], m_ref.at[h], acc_ref.at[h]))
    _flash_pipeline(i, streams)

    lp = lp_ref[...]
    lam = (jnp.exp(jnp.sum(lp[0:1] * lp[1:2], axis=-1, keepdims=True))
           - jnp.exp(jnp.sum(lp[2:3] * lp[3:4], axis=-1, keepdims=True)) + lam_init)
    for h in range(DIFF_HEADS_PER_STEP):
        acc = acc_ref[h]
        o = acc[:, :HEAD_DIM] / acc[:, HEAD_DIM:]
        o = o[:tq] - lam * o[tq:]
        o = _rms_norm_rows(o, g_ref[...]) * (1.0 - lam_init)
        o_ref[:, h * HEAD_DIM:(h + 1) * HEAD_DIM] = o.astype(o_ref.dtype)


def _diff_attention(qkv, lam_params, subln_g, lam_init):
    tq, tk, hp = ATTN_TQ, ATTN_TK, DIFF_HEADS_PER_STEP
    w = hp * HEAD_DIM
    groups = DIFF_HEADS // hp
    base = 3 * MOBA_WIDTH // w
    return pl.pallas_call(
        functools.partial(_diff_kernel, lam_init),
        grid=(groups, SEQ // tq),
        in_specs=[
            pl.BlockSpec((tq, w), lambda h, i: (i, base + h)),
            pl.BlockSpec((SEQ, w), lambda h, i: (0, base + groups + h)),
            pl.BlockSpec((SEQ, w), lambda h, i: (0, base + 2 * groups + h)),
            pl.BlockSpec((4, DIFF_QK_DIM), lambda h, i: (0, 0)),
            pl.BlockSpec((1, HEAD_DIM), lambda h, i: (0, 0)),
        ],
        out_specs=pl.BlockSpec((tq, w), lambda h, i: (i, h)),
        out_shape=jax.ShapeDtypeStruct((SEQ, DIFF_WIDTH), jnp.bfloat16),
        scratch_shapes=[
            pltpu.VMEM((hp, 2 * tq, HEAD_DIM), jnp.bfloat16),
            pltpu.VMEM((hp, 2 * tq, tk), jnp.float32),
            pltpu.VMEM((hp, 2 * tq, tk), jnp.float32),
            pltpu.VMEM((hp, 2 * tq, HEAD_DIM), jnp.float32),
            pltpu.VMEM((hp, 2 * tq, 2 * HEAD_DIM), jnp.float32),
        ],
        compiler_params=_cparams(("arbitrary", "arbitrary")),
        name="diff_attention",
    )(qkv, qkv, qkv, lam_params, subln_g)


def _out_proj_kernel(x_ref, a_ref, b_ref, c_ref, w_ref, g_ref, o_ref, h_ref):
    half = OUT_TM // 2
    for r0 in range(0, OUT_TM, half):
        rows = slice(r0, r0 + half)
        acc = jnp.dot(a_ref[rows, :], w_ref[0:MOBA_WIDTH, :], preferred_element_type=jnp.float32)
        acc += jnp.dot(b_ref[rows, :], w_ref[MOBA_WIDTH:MOBA_WIDTH + DIFF_WIDTH, :],
                       preferred_element_type=jnp.float32)
        acc += jnp.dot(c_ref[rows, :], w_ref[MOBA_WIDTH + DIFF_WIDTH:, :], preferred_element_type=jnp.float32)
        y = x_ref[rows, :] + acc
        o_ref[rows, :] = y
        h_ref[rows, :] = _rms_norm_rows(y, g_ref[...]).astype(h_ref.dtype)


def _out_proj(x, moba_o, diff_o, sgu_o, w_out, g_mlp):
    tm = OUT_TM
    return pl.pallas_call(
        _out_proj_kernel,
        grid=(SEQ // tm,),
        in_specs=[
            pl.BlockSpec((tm, D_MODEL), lambda i: (i, 0)),
            pl.BlockSpec((tm, MOBA_WIDTH), lambda i: (i, 0)),
            pl.BlockSpec((tm, DIFF_WIDTH), lambda i: (i, 0)),
            pl.BlockSpec((tm, SGU_WIDTH), lambda i: (i, 0)),
            pl.BlockSpec((D_MODEL, D_MODEL), lambda i: (0, 0)),
            pl.BlockSpec((1, D_MODEL), lambda i: (0, 0)),
        ],
        out_specs=[
            pl.BlockSpec((tm, D_MODEL), lambda i: (i, 0)),
            pl.BlockSpec((tm, D_MODEL), lambda i: (i, 0)),
        ],
        out_shape=[
            jax.ShapeDtypeStruct((SEQ, D_MODEL), jnp.float32),
            jax.ShapeDtypeStruct((SEQ, D_MODEL), jnp.bfloat16),
        ],
        compiler_params=_cparams(("arbitrary",)),
        name="out_proj",
    )(x, moba_o, diff_o, sgu_o, w_out, g_mlp)


def _mlp_kernel(final, x_ref, h_ref, w1_ref, w2_ref, gf_ref, o_ref):
    j = pl.program_id(1)

    @pl.when(j == 0)
    def _():
        o_ref[...] = x_ref[...]

    a = jnp.dot(h_ref[...], w1_ref[...], preferred_element_type=jnp.float32)
    a = jnp.square(jnp.maximum(a, 0.0)).astype(jnp.bfloat16)
    o_ref[...] += jnp.dot(a, w2_ref[...], preferred_element_type=jnp.float32)

    if final:
        @pl.when(j == pl.num_programs(1) - 1)
        def _():
            o_ref[...] = _rms_norm_rows(o_ref[...], gf_ref[...])


def _mlp(x, h, w1, w2, g_final, final):
    tm, tf = MLP_TM, MLP_TF
    return pl.pallas_call(
        functools.partial(_mlp_kernel, final),
        grid=(SEQ // tm, D_FF // tf),
        in_specs=[
            pl.BlockSpec((tm, D_MODEL), lambda i, j: (i, 0)),
            pl.BlockSpec((tm, D_MODEL), lambda i, j: (i, 0)),
            pl.BlockSpec((None, D_MODEL, tf), lambda i, j: (j, 0, 0)),
            pl.BlockSpec((tf, D_MODEL), lambda i, j: (j, 0)),
            pl.BlockSpec((1, D_MODEL), lambda i, j: (0, 0)),
        ],
        out_specs=pl.BlockSpec((tm, D_MODEL), lambda i, j: (i, 0)),
        out_shape=jax.ShapeDtypeStruct((SEQ, D_MODEL), jnp.float32),
        compiler_params=_cparams(("arbitrary", "arbitrary")),
        name="mlp",
    )(x, h, w1, w2, g_final)


def _rope_tables():
    pos = jnp.arange(SEQ, dtype=jnp.float32)[:, None]

    def tab(dim):
        inv = 1.0 / (ROPE_THETA ** (jnp.arange(0, dim, 2, dtype=jnp.float32) / dim))
        ang = pos * inv[None, :]
        return jnp.cos(ang), jnp.sin(ang)

    cm, sm = tab(HEAD_DIM)
    cd, sd = tab(DIFF_QK_DIM)
    zd = jnp.zeros_like(sd)
    return (jnp.concatenate([cm, cm], axis=1), jnp.concatenate([-sm, sm], axis=1),
            jnp.concatenate([cd, cd, cd, cd], axis=1),
            jnp.concatenate([-sd, zd, -sd, zd], axis=1),
            jnp.concatenate([zd, sd, zd, sd], axis=1))


def kernel(x, attn_norm_g, w_in, diff_lambda, diff_subln_g, sgu_ln_g, sgu_ln_b, sgu_w, sgu_b,
           w_out, mlp_norm_g, w_mlp_in, w_mlp_out, final_norm_g):
    assert x.shape == (1, SEQ, D_MODEL)
    bf = jnp.bfloat16
    xs = x.reshape(SEQ, D_MODEL)
    cm, sm, cd, sdl, sdh = _rope_tables()
    gf = final_norm_g.reshape(1, D_MODEL)
    w_in_l = w_in[0].astype(bf)
    for l in range(DEPTH):
        lam_init = 0.8 - 0.6 * math.exp(-0.3 * l)
        g_attn = attn_norm_g[l].reshape(1, D_MODEL)
        qkv, kmean, sgu_o = _in_proj(xs, g_attn, w_in_l, cm, sm, cd, sdl, sdh, sgu_ln_g[l], sgu_ln_b[l],
                                     sgu_w[l], sgu_b[l].reshape(SGU_GROUPS, SGU_CHUNK, 1))
        kmean = kmean.reshape(N_BLOCKS, MOBA_WIDTH)
        casts = [(w_mlp_in, l, MLP_TF), (w_mlp_out, l), (w_out, l)] + ([(w_in, l + 1)] if l + 1 < DEPTH else [])
        moba_o, w1_l, w2_l, w_out_l, *w_in_next = _moba_attention(qkv, kmean, casts)
        diff_o = _diff_attention(qkv, diff_lambda[l], diff_subln_g[l].reshape(1, HEAD_DIM), lam_init)
        xs, h_mlp = _out_proj(xs, moba_o, diff_o, sgu_o, w_out_l, mlp_norm_g[l].reshape(1, D_MODEL))
        xs = _mlp(xs, h_mlp, w1_l, w2_l, gf, l == DEPTH - 1)
        if w_in_next:
            w_in_l = w_in_next[0]
    return xs.reshape(1, SEQ, D_MODEL)
```

```python
import functools
import math
from typing import Callable, NamedTuple

import jax
import jax.numpy as jnp
from jax import lax
from jax.experimental import pallas as pl
from jax.experimental.pallas import tpu as pltpu

D_MODEL = 2048
SEQ = 8192
DEPTH = 4
HEAD_DIM = 128
MOBA_HEADS = 6
DIFF_HEADS = 6
SGU_GROUPS = 4
MOBA_WIDTH = MOBA_HEADS * HEAD_DIM
DIFF_WIDTH = DIFF_HEADS * HEAD_DIM
SGU_WIDTH = SGU_GROUPS * HEAD_DIM
DIFF_QK_DIM = HEAD_DIM // 2
MOBA_BLOCK = 256
MOBA_TOPK = 3
SGU_CHUNK = 128
D_FF = 4 * D_MODEL
ROPE_THETA = 10000.0
EPS = 1e-6
QKV_WIDTH = 3 * MOBA_WIDTH + 3 * DIFF_WIDTH
N_BLOCKS = SEQ // MOBA_BLOCK

VMEM_LIMIT_BYTES = 56 * 1024 * 1024

PROJ_TM = 512
PROJ_TN = 768
ATTN_TQ = 2 * MOBA_BLOCK
ATTN_TK = 2 * MOBA_BLOCK
MOBA_HEADS_PER_STEP = 3
DIFF_HEADS_PER_STEP = 2
CAST_CHUNKS = 32
OUT_TM = 512
MLP_TM = 512
MLP_TF = 2048
MLP_VMEM_LIMIT_BYTES = 58 * 1024 * 1024

NEG_INF = float("-inf")
NEG_BIG = -1e30
LOG2E = 1.4426950408889634


def _cparams(sem, vmem_limit_bytes=VMEM_LIMIT_BYTES):
    return pltpu.CompilerParams(dimension_semantics=sem, vmem_limit_bytes=vmem_limit_bytes)


def _rms_norm_rows(x, g):
    return x * lax.rsqrt(jnp.mean(x * x, axis=-1, keepdims=True) + EPS) * g


def _gelu_tanh(x):
    c = math.sqrt(2.0 / math.pi)
    return 0.5 * x * (1.0 + jnp.tanh(c * (x + 0.044715 * (x * x * x))))


def _in_proj_kernel(x_ref, g_ref, w_ref, cm_ref, sm_ref, cd_ref, sdl_ref, sdh_ref,
                    lng_ref, lnb_ref, ws_ref, bs_ref, o_ref, kmean_ref, sgu_ref):
    h = _rms_norm_rows(x_ref[...], g_ref[...]).astype(jnp.bfloat16)

    def rope_moba(xh):
        return xh * cm_ref[...] + pltpu.roll(xh, HEAD_DIM // 2, 1) * sm_ref[...]

    def rope_diff(xh):
        return (xh * cd_ref[...]
                + pltpu.roll(xh, HEAD_DIM - DIFF_QK_DIM // 2, 1) * sdl_ref[...]
                + pltpu.roll(xh, DIFF_QK_DIM // 2, 1) * sdh_ref[...])

    def group_dot(grp):
        return jnp.dot(h, w_ref[:, grp * PROJ_TN:(grp + 1) * PROJ_TN], preferred_element_type=jnp.float32)

    def group_epilogue(grp, acc, fn):
        for hd in range(PROJ_TN // HEAD_DIM):
            sl = slice(hd * HEAD_DIM, (hd + 1) * HEAD_DIM)
            r = acc[:, sl] if fn is None else fn(acc[:, sl])
            o_ref[:, grp * PROJ_TN + hd * HEAD_DIM:grp * PROJ_TN + (hd + 1) * HEAD_DIM] = r.astype(o_ref.dtype)
            if grp == 1:
                for b in range(PROJ_TM // MOBA_BLOCK):
                    blk = r[b * MOBA_BLOCK:(b + 1) * MOBA_BLOCK]
                    kmean_ref[0, b:b + 1, sl] = jnp.mean(blk, axis=0, keepdims=True)

    us = jnp.dot(h, w_ref[:, QKV_WIDTH:QKV_WIDTH + SGU_WIDTH], preferred_element_type=jnp.float32)
    vs = jnp.dot(h, w_ref[:, QKV_WIDTH + SGU_WIDTH:], preferred_element_type=jnp.float32)
    acc_q = group_dot(0)

    row = lax.broadcasted_iota(jnp.int32, (SGU_CHUNK, SGU_CHUNK), 0)
    col = lax.broadcasted_iota(jnp.int32, (SGU_CHUNK, SGU_CHUNK), 1)
    for g in range(SGU_GROUPS):
        sl = slice(g * HEAD_DIM, (g + 1) * HEAD_DIM)
        u = _gelu_tanh(us[:, sl])
        v = _gelu_tanh(vs[:, sl])
        mu = jnp.mean(v, axis=-1, keepdims=True)
        vc = v - mu
        var = jnp.mean(vc * vc, axis=-1, keepdims=True)
        vn = (vc * lax.rsqrt(var + EPS) * lng_ref[g:g + 1, :] + lnb_ref[g:g + 1, :]).astype(jnp.bfloat16)
        w = jnp.where(row >= col, ws_ref[g], 0.0).astype(jnp.bfloat16)
        bias = bs_ref[g]
        for c in range(PROJ_TM // SGU_CHUNK):
            rs = slice(c * SGU_CHUNK, (c + 1) * SGU_CHUNK)
            mixed = jnp.dot(w, vn[rs], preferred_element_type=jnp.float32) + bias
            sgu_ref[rs, sl] = (u[rs] * mixed).astype(sgu_ref.dtype)

    group_epilogue(0, acc_q, rope_moba)
    for grp, fn in ((1, rope_moba), (3, rope_diff), (4, rope_diff), (2, None), (5, None)):
        group_epilogue(grp, group_dot(grp), fn)


def _in_proj(x, g, w_in, cm, sm, cd, sdl, sdh, ln_g, ln_b, w_s, b_s):
    tm = PROJ_TM
    row_tab = pl.BlockSpec((tm, HEAD_DIM), lambda i: (i, 0))
    full = lambda shape: pl.BlockSpec(shape, lambda i: (0,) * len(shape))
    return pl.pallas_call(
        _in_proj_kernel,
        grid=(SEQ // tm,),
        in_specs=[
            pl.BlockSpec((tm, D_MODEL), lambda i: (i, 0)),
            pl.BlockSpec((1, D_MODEL), lambda i: (0, 0)),
            pl.BlockSpec((D_MODEL, QKV_WIDTH + 2 * SGU_WIDTH), lambda i: (0, 0), pipeline_mode=pl.Buffered(1)),
            row_tab, row_tab, row_tab, row_tab, row_tab,
            full((SGU_GROUPS, HEAD_DIM)),
            full((SGU_GROUPS, HEAD_DIM)),
            full((SGU_GROUPS, SGU_CHUNK, SGU_CHUNK)),
            full((SGU_GROUPS, SGU_CHUNK, 1)),
        ],
        out_specs=[
            pl.BlockSpec((tm, QKV_WIDTH), lambda i: (i, 0)),
            pl.BlockSpec((1, tm // MOBA_BLOCK, MOBA_WIDTH), lambda i: (i, 0, 0)),
            pl.BlockSpec((tm, SGU_WIDTH), lambda i: (i, 0)),
        ],
        out_shape=[
            jax.ShapeDtypeStruct((SEQ, QKV_WIDTH), jnp.bfloat16),
            jax.ShapeDtypeStruct((SEQ // tm, tm // MOBA_BLOCK, MOBA_WIDTH), jnp.float32),
            jax.ShapeDtypeStruct((SEQ, SGU_WIDTH), jnp.bfloat16),
        ],
        compiler_params=_cparams(("arbitrary",)),
        name="in_proj",
    )(x, g, w_in, cm, sm, cd, sdl, sdh, ln_g, ln_b, w_s, b_s)


class _Stream(NamedTuple):
    qk_own: Callable
    qk_past: Callable
    v_tile: Callable
    own_mask: Callable
    sa_ref: object
    sb_ref: object
    m_ref: object
    acc_ref: object
    c: float


def _scores(q, k):
    return lax.dot_general(q, k, (((1,), (1,)), ((), ())), preferred_element_type=jnp.float32)


def _with_ones(v):
    return jnp.concatenate([v, jnp.ones(v.shape, v.dtype)], axis=1)


def _lane_tile(x, n):
    return jnp.concatenate([x] * n, axis=1)


def _softmax_pv(st, s_ref, j, mask_fn=None):
    s = s_ref[...]
    n_rep = s.shape[1] // HEAD_DIM
    if mask_fn is not None:
        s = mask_fn(s)
        m_new = jnp.max(s, axis=-1, keepdims=True)
        p = jnp.exp2((s - m_new) * st.c).astype(jnp.bfloat16)
        st.acc_ref[...] = jnp.dot(p, st.v_tile(j), preferred_element_type=jnp.float32)
        st.m_ref[...] = jnp.broadcast_to(m_new, st.m_ref.shape)
        return
    m_prev = st.m_ref[...]
    m_new = jnp.maximum(m_prev, jnp.max(s, axis=-1, keepdims=True))
    alpha = jnp.exp2((m_prev - m_new) * st.c)
    p = jnp.exp2((s - _lane_tile(m_new, n_rep)) * st.c).astype(jnp.bfloat16)
    st.acc_ref[...] = (_lane_tile(alpha, 2) * st.acc_ref[...]
                       + jnp.dot(p, st.v_tile(j), preferred_element_type=jnp.float32))
    st.m_ref[...] = m_new


def _flash_pipeline(n_past, streams):
    for st in streams:
        st.qk_own(st.sa_ref)
    for st in streams:
        st.qk_past(0, st.sb_ref)
        _softmax_pv(st, st.sa_ref, n_past, st.own_mask)

    def pair(jj, carry):
        j = 2 * jj
        for st in streams:
            st.qk_past(j + 1, st.sa_ref)
            _softmax_pv(st, st.sb_ref, j)
        for st in streams:
            st.qk_past(j + 2, st.sb_ref)
            _softmax_pv(st, st.sa_ref, j + 1)
        return carry

    lax.fori_loop(0, n_past // 2, pair, 0)

    @pl.when(n_past % 2 == 1)
    def _():
        for st in streams:
            _softmax_pv(st, st.sb_ref, n_past - 1)


def _cast_specs(w, layer):
    rows, cols = w.shape[1], w.shape[2]
    chunk = rows // CAST_CHUNKS
    assert chunk * CAST_CHUNKS == rows and chunk % 16 == 0
    steps_per_group = SEQ // ATTN_TQ

    def chunk_index(h, i):
        return jnp.minimum(h * steps_per_group + i, CAST_CHUNKS - 1)

    return (pl.BlockSpec((None, chunk, cols), lambda h, i: (layer, chunk_index(h, i), 0)),
            pl.BlockSpec((chunk, cols), lambda h, i: (chunk_index(h, i), 0)),
            jax.ShapeDtypeStruct((rows, cols), jnp.bfloat16))


def _cast_chunks(src_refs, dst_refs):
    for src, dst in zip(src_refs, dst_refs, strict=True):
        dst[...] = src[...].astype(dst.dtype)


def _moba_stream(i, q, k_ref, v_ref, kmean, qa_ref, sa_ref, sb_ref, m_ref, acc_ref):
    tq, tk = ATTN_TQ, ATTN_TK
    k_hi = kmean.astype(jnp.bfloat16)
    rest = kmean - k_hi.astype(jnp.float32)
    k_mid = rest.astype(jnp.bfloat16)
    k_lo = (rest - k_mid.astype(jnp.float32)).astype(jnp.bfloat16)
    gate = _scores(k_hi, q) + _scores(k_mid, q) + _scores(k_lo, q)
    blk = lax.broadcasted_iota(jnp.int32, (N_BLOCKS, tq), 0)
    own = 2 * i + jnp.where(lax.broadcasted_iota(jnp.int32, (N_BLOCKS, tq), 1) >= MOBA_BLOCK, 1, 0)
    gate = jnp.where(blk < own, gate, NEG_INF)
    blk_f = blk.astype(jnp.float32)
    sel = jnp.zeros((N_BLOCKS, tq), jnp.float32)
    for _ in range(MOBA_TOPK):
        mx = jnp.max(gate, axis=0, keepdims=True)
        cand = jnp.where((gate == mx) & (gate > NEG_INF), blk_f, float(N_BLOCKS))
        pick = blk_f == jnp.min(cand, axis=0, keepdims=True)
        sel = jnp.where(pick, 1.0, sel)
        gate = jnp.where(pick, NEG_INF, gate)
    sel = jnp.concatenate([sel, jnp.zeros((HEAD_DIM - N_BLOCKS, tq), jnp.float32)], axis=0).T
    qa_ref[:, :HEAD_DIM] = q
    qa_ref[:, HEAD_DIM:] = jnp.where(sel > 0.0, 0.0, NEG_BIG).astype(jnp.bfloat16)
    lane = lax.broadcasted_iota(jnp.int32, (tq, HEAD_DIM), 1)
    picked_first = jnp.sum(jnp.where(lane == 2 * i, sel, 0.0), axis=-1, keepdims=True) > 0.0

    def own_mask(s):
        r = lax.broadcasted_iota(jnp.int32, s.shape, 0)
        col = lax.broadcasted_iota(jnp.int32, s.shape, 1)
        rr = lax.broadcasted_iota(jnp.int32, (tq, 1), 0)
        first_col = jnp.where(picked_first | (rr < MOBA_BLOCK), 0, MOBA_BLOCK)
        return jnp.where(col <= r, jnp.where(col >= first_col, s, NEG_INF), NEG_INF)

    def qk_own(s_ref):
        s_ref[...] = _scores(qa_ref[:, :HEAD_DIM], k_ref[pl.ds(pl.multiple_of(i * tk, tk), tk), :])

    def qk_past(j, s_ref):
        erow = lax.broadcasted_iota(jnp.int32, (tk, HEAD_DIM), 0)
        elane = lax.broadcasted_iota(jnp.int32, (tk, HEAD_DIM), 1)
        onehot = jnp.where(elane == 2 * j + jnp.where(erow >= MOBA_BLOCK, 1, 0), 1.0, 0.0).astype(jnp.bfloat16)
        ka = jnp.concatenate([k_ref[pl.ds(pl.multiple_of(j * tk, tk), tk), :], onehot], axis=1)
        s_ref[...] = _scores(qa_ref[...], ka)

    def v_tile(j):
        return _with_ones(v_ref[pl.ds(pl.multiple_of(j * tk, tk), tk), :])

    return _Stream(qk_own, qk_past, v_tile, own_mask, sa_ref, sb_ref, m_ref, acc_ref,
                   (HEAD_DIM ** -0.5) * LOG2E)


def _moba_kernel(n_cast, q_ref, k_ref, v_ref, kmean_ref, *refs):
    cast_src, (o_ref, *cast_dst) = refs[:n_cast], refs[n_cast:2 * n_cast + 1]
    qa_ref, sa_ref, sb_ref, m_ref, acc_ref = refs[2 * n_cast + 1:]
    _cast_chunks(cast_src, cast_dst)
    i = pl.program_id(1)
    streams = []
    for h in range(MOBA_HEADS_PER_STEP):
        sl = slice(h * HEAD_DIM, (h + 1) * HEAD_DIM)
        streams.append(_moba_stream(i, q_ref[:, sl], k_ref.at[:, sl], v_ref.at[:, sl], kmean_ref[:, sl],
                                    qa_ref.at[h], sa_ref.at[h], sb_ref.at[h], m_ref.at[h], acc_ref.at[h]))
    _flash_pipeline(i, streams)
    for h in range(MOBA_HEADS_PER_STEP):
        acc = acc_ref[h]
        o_ref[:, h * HEAD_DIM:(h + 1) * HEAD_DIM] = (acc[:, :HEAD_DIM] / acc[:, HEAD_DIM:]).astype(o_ref.dtype)


def _moba_attention(qkv, kmean, casts):
    tq, tk, hp = ATTN_TQ, ATTN_TK, MOBA_HEADS_PER_STEP
    w = hp * HEAD_DIM
    groups = MOBA_HEADS // hp
    assert groups * (SEQ // tq) >= CAST_CHUNKS
    cast_in, cast_out, cast_shape = zip(*[_cast_specs(cw, cl) for cw, cl in casts])
    return pl.pallas_call(
        functools.partial(_moba_kernel, len(casts)),
        grid=(groups, SEQ // tq),
        in_specs=[
            pl.BlockSpec((tq, w), lambda h, i: (i, h)),
            pl.BlockSpec((SEQ, w), lambda h, i: (0, groups + h)),
            pl.BlockSpec((SEQ, w), lambda h, i: (0, 2 * groups + h)),
            pl.BlockSpec((N_BLOCKS, w), lambda h, i: (0, h)),
            *cast_in,
        ],
        out_specs=[pl.BlockSpec((tq, w), lambda h, i: (i, h)), *cast_out],
        out_shape=[jax.ShapeDtypeStruct((SEQ, MOBA_WIDTH), jnp.bfloat16), *cast_shape],
        scratch_shapes=[
            pltpu.VMEM((hp, tq, 2 * HEAD_DIM), jnp.bfloat16),
            pltpu.VMEM((hp, tq, tk), jnp.float32),
            pltpu.VMEM((hp, tq, tk), jnp.float32),
            pltpu.VMEM((hp, tq, HEAD_DIM), jnp.float32),
            pltpu.VMEM((hp, tq, 2 * HEAD_DIM), jnp.float32),
        ],
        compiler_params=_cparams(("arbitrary", "arbitrary")),
        name="moba_attention",
    )(qkv, qkv, qkv, kmean, *[cw for cw, _ in casts])


def _diff_stream(i, q, k_ref, v_ref, q2_ref, sa_ref, sb_ref, m_ref, acc_ref):
    tq, tk = ATTN_TQ, ATTN_TK
    q = q * (DIFF_QK_DIM ** -0.5)
    lane = lax.broadcasted_iota(jnp.int32, (tq, HEAD_DIM), 1)
    zero = jnp.zeros_like(q)
    q2_ref[:tq, :] = jnp.where(lane < DIFF_QK_DIM, q, zero)
    q2_ref[tq:, :] = jnp.where(lane >= DIFF_QK_DIM, q, zero)

    def causal_mask(s):
        r = lax.broadcasted_iota(jnp.int32, s.shape, 0)
        r = jnp.where(r >= tq, r - tq, r)
        col = lax.broadcasted_iota(jnp.int32, s.shape, 1)
        return jnp.where(col <= r, s, NEG_INF)

    def qk_past(j, s_ref):
        s_ref[...] = _scores(q2_ref[...], k_ref[pl.ds(pl.multiple_of(j * tk, tk), tk), :])

    def v_tile(j):
        return _with_ones(v_ref[pl.ds(pl.multiple_of(j * tk, tk), tk), :])

    return _Stream(functools.partial(qk_past, i), qk_past, v_tile, causal_mask,
                   sa_ref, sb_ref, m_ref, acc_ref, LOG2E)


def _diff_kernel(lam_init, n_cast, q_ref, k_ref, v_ref, lp_ref, g_ref, *refs):
    cast_src, (o_ref, *cast_dst) = refs[:n_cast], refs[n_cast:2 * n_cast + 1]
    q2_ref, sa_ref, sb_ref, m_ref, acc_ref = refs[2 * n_cast + 1:]
    _cast_chunks(cast_src, cast_dst)
    i = pl.program_id(1)
    tq = ATTN_TQ
    streams = []
    for h in range(DIFF_HEADS_PER_STEP):
        sl = slice(h * HEAD_DIM, (h + 1) * HEAD_DIM)
        streams.append(_diff_stream(i, q_ref[:, sl], k_ref.at[:, sl], v_ref.at[:, sl],
                                    q2_ref.at[h], sa_ref.at[h], sb_ref.at[h], m_ref.at[h], acc_ref.at[h]))
    _flash_pipeline(i, streams)

    lp = lp_ref[...]
    lam = (jnp.exp(jnp.sum(lp[0:1] * lp[1:2], axis=-1, keepdims=True))
           - jnp.exp(jnp.sum(lp[2:3] * lp[3:4], axis=-1, keepdims=True)) + lam_init)
    for h in range(DIFF_HEADS_PER_STEP):
        acc = acc_ref[h]
        o = acc[:, :HEAD_DIM] / acc[:, HEAD_DIM:]
        o = o[:tq] - lam * o[tq:]
        o = _rms_norm_rows(o, g_ref[...]) * (1.0 - lam_init)
        o_ref[:, h * HEAD_DIM:(h + 1) * HEAD_DIM] = o.astype(o_ref.dtype)


def _diff_attention(qkv, lam_params, subln_g, lam_init, casts):
    tq, tk, hp = ATTN_TQ, ATTN_TK, DIFF_HEADS_PER_STEP
    w = hp * HEAD_DIM
    groups = DIFF_HEADS // hp
    base = 3 * MOBA_WIDTH // w
    assert groups * (SEQ // tq) >= CAST_CHUNKS
    cast_in, cast_out, cast_shape = zip(*[_cast_specs(cw, cl) for cw, cl in casts])
    return pl.pallas_call(
        functools.partial(_diff_kernel, lam_init, len(casts)),
        grid=(groups, SEQ // tq),
        in_specs=[
            pl.BlockSpec((tq, w), lambda h, i: (i, base + h)),
            pl.BlockSpec((SEQ, w), lambda h, i: (0, base + groups + h)),
            pl.BlockSpec((SEQ, w), lambda h, i: (0, base + 2 * groups + h)),
            pl.BlockSpec((4, DIFF_QK_DIM), lambda h, i: (0, 0)),
            pl.BlockSpec((1, HEAD_DIM), lambda h, i: (0, 0)),
            *cast_in,
        ],
        out_specs=[pl.BlockSpec((tq, w), lambda h, i: (i, h)), *cast_out],
        out_shape=[jax.ShapeDtypeStruct((SEQ, DIFF_WIDTH), jnp.bfloat16), *cast_shape],
        scratch_shapes=[
            pltpu.VMEM((hp, 2 * tq, HEAD_DIM), jnp.bfloat16),
            pltpu.VMEM((hp, 2 * tq, tk), jnp.float32),
            pltpu.VMEM((hp, 2 * tq, tk), jnp.float32),
            pltpu.VMEM((hp, 2 * tq, HEAD_DIM), jnp.float32),
            pltpu.VMEM((hp, 2 * tq, 2 * HEAD_DIM), jnp.float32),
        ],
        compiler_params=_cparams(("arbitrary", "arbitrary")),
        name="diff_attention",
    )(qkv, qkv, qkv, lam_params, subln_g, *[cw for cw, _ in casts])


def _out_proj_kernel(x_ref, a_ref, b_ref, c_ref, w_ref, g_ref, o_ref, h_ref):
    half = OUT_TM // 2
    for r0 in range(0, OUT_TM, half):
        rows = slice(r0, r0 + half)
        acc = jnp.dot(a_ref[rows, :], w_ref[0:MOBA_WIDTH, :], preferred_element_type=jnp.float32)
        acc += jnp.dot(b_ref[rows, :], w_ref[MOBA_WIDTH:MOBA_WIDTH + DIFF_WIDTH, :],
                       preferred_element_type=jnp.float32)
        acc += jnp.dot(c_ref[rows, :], w_ref[MOBA_WIDTH + DIFF_WIDTH:, :], preferred_element_type=jnp.float32)
        y = x_ref[rows, :] + acc
        o_ref[rows, :] = y
        h_ref[rows, :] = _rms_norm_rows(y, g_ref[...]).astype(h_ref.dtype)


def _out_proj(x, moba_o, diff_o, sgu_o, w_out, g_mlp):
    tm = OUT_TM
    return pl.pallas_call(
        _out_proj_kernel,
        grid=(SEQ // tm,),
        in_specs=[
            pl.BlockSpec((tm, D_MODEL), lambda i: (i, 0)),
            pl.BlockSpec((tm, MOBA_WIDTH), lambda i: (i, 0)),
            pl.BlockSpec((tm, DIFF_WIDTH), lambda i: (i, 0)),
            pl.BlockSpec((tm, SGU_WIDTH), lambda i: (i, 0)),
            pl.BlockSpec((D_MODEL, D_MODEL), lambda i: (0, 0)),
            pl.BlockSpec((1, D_MODEL), lambda i: (0, 0)),
        ],
        out_specs=[
            pl.BlockSpec((tm, D_MODEL), lambda i: (i, 0)),
            pl.BlockSpec((tm, D_MODEL), lambda i: (i, 0)),
        ],
        out_shape=[
            jax.ShapeDtypeStruct((SEQ, D_MODEL), jnp.float32),
            jax.ShapeDtypeStruct((SEQ, D_MODEL), jnp.bfloat16),
        ],
        compiler_params=_cparams(("arbitrary",)),
        name="out_proj",
    )(x, moba_o, diff_o, sgu_o, w_out, g_mlp)


def _mlp_kernel(final, x_ref, h_ref, w1_ref, w2_ref, gf_ref, o_ref):
    j = pl.program_id(1)

    @pl.when(j == 0)
    def _():
        o_ref[...] = x_ref[...]

    a = jnp.dot(h_ref[...], w1_ref[...], preferred_element_type=jnp.float32)
    a = jnp.square(jnp.maximum(a, 0.0)).astype(jnp.bfloat16)
    o_ref[...] += jnp.dot(a, w2_ref[...], preferred_element_type=jnp.float32)

    if final:
        @pl.when(j == pl.num_programs(1) - 1)
        def _():
            o_ref[...] = _rms_norm_rows(o_ref[...], gf_ref[...])


def _mlp(x, h, w1, w2, g_final, final):
    tm, tf = MLP_TM, MLP_TF
    return pl.pallas_call(
        functools.partial(_mlp_kernel, final),
        grid=(SEQ // tm, D_FF // tf),
        in_specs=[
            pl.BlockSpec((tm, D_MODEL), lambda i, j: (i, 0)),
            pl.BlockSpec((tm, D_MODEL), lambda i, j: (i, 0)),
            pl.BlockSpec((D_MODEL, tf), lambda i, j: (0, j)),
            pl.BlockSpec((tf, D_MODEL), lambda i, j: (j, 0)),
            pl.BlockSpec((1, D_MODEL), lambda i, j: (0, 0)),
        ],
        out_specs=pl.BlockSpec((tm, D_MODEL), lambda i, j: (i, 0)),
        out_shape=jax.ShapeDtypeStruct((SEQ, D_MODEL), jnp.float32),
        compiler_params=_cparams(("arbitrary", "arbitrary"), MLP_VMEM_LIMIT_BYTES),
        name="mlp",
    )(x, h, w1, w2, g_final)


def _rope_tables():
    pos = jnp.arange(SEQ, dtype=jnp.float32)[:, None]

    def tab(dim):
        inv = 1.0 / (ROPE_THETA ** (jnp.arange(0, dim, 2, dtype=jnp.float32) / dim))
        ang = pos * inv[None, :]
        return jnp.cos(ang), jnp.sin(ang)

    cm, sm = tab(HEAD_DIM)
    cd, sd = tab(DIFF_QK_DIM)
    zd = jnp.zeros_like(sd)
    return (jnp.concatenate([cm, cm], axis=1), jnp.concatenate([-sm, sm], axis=1),
            jnp.concatenate([cd, cd, cd, cd], axis=1),
            jnp.concatenate([-sd, zd, -sd, zd], axis=1),
            jnp.concatenate([zd, sd, zd, sd], axis=1))


def kernel(x, attn_norm_g, w_in, diff_lambda, diff_subln_g, sgu_ln_g, sgu_ln_b, sgu_w, sgu_b,
           w_out, mlp_norm_g, w_mlp_in, w_mlp_out, final_norm_g):
    assert x.shape == (1, SEQ, D_MODEL)
    bf = jnp.bfloat16
    xs = x.reshape(SEQ, D_MODEL)
    cm, sm, cd, sdl, sdh = _rope_tables()
    gf = final_norm_g.reshape(1, D_MODEL)
    w_in_l = w_in[0].astype(bf)
    for l in range(DEPTH):
        lam_init = 0.8 - 0.6 * math.exp(-0.3 * l)
        g_attn = attn_norm_g[l].reshape(1, D_MODEL)
        qkv, kmean, sgu_o = _in_proj(xs, g_attn, w_in_l, cm, sm, cd, sdl, sdh, sgu_ln_g[l], sgu_ln_b[l],
                                     sgu_w[l], sgu_b[l].reshape(SGU_GROUPS, SGU_CHUNK, 1))
        kmean = kmean.reshape(N_BLOCKS, MOBA_WIDTH)
        moba_o, w1_l, w_out_l = _moba_attention(qkv, kmean, [(w_mlp_in, l), (w_out, l)])
        diff_casts = [(w_mlp_out, l)] + ([(w_in, l + 1)] if l + 1 < DEPTH else [])
        diff_o, w2_l, *w_in_next = _diff_attention(qkv, diff_lambda[l], diff_subln_g[l].reshape(1, HEAD_DIM),
                                                   lam_init, diff_casts)
        xs, h_mlp = _out_proj(xs, moba_o, diff_o, sgu_o, w_out_l, mlp_norm_g[l].reshape(1, D_MODEL))
        xs = _mlp(xs, h_mlp, w1_l, w2_l, gf, l == DEPTH - 1)
        if w_in_next:
            w_in_l = w_in_next[0]
    return xs.reshape(1, SEQ, D_MODEL)
```

```python
import functools
import math
from typing import Callable, NamedTuple

import jax
import jax.numpy as jnp
from jax import lax
from jax.experimental import pallas as pl
from jax.experimental.pallas import tpu as pltpu

D_MODEL = 2048
SEQ = 8192
DEPTH = 4
HEAD_DIM = 128
MOBA_HEADS = 6
DIFF_HEADS = 6
SGU_GROUPS = 4
MOBA_WIDTH = MOBA_HEADS * HEAD_DIM
DIFF_WIDTH = DIFF_HEADS * HEAD_DIM
SGU_WIDTH = SGU_GROUPS * HEAD_DIM
DIFF_QK_DIM = HEAD_DIM // 2
MOBA_BLOCK = 256
MOBA_TOPK = 3
SGU_CHUNK = 128
D_FF = 4 * D_MODEL
ROPE_THETA = 10000.0
EPS = 1e-6
QKV_WIDTH = 3 * MOBA_WIDTH + 3 * DIFF_WIDTH
N_BLOCKS = SEQ // MOBA_BLOCK

VMEM_LIMIT_BYTES = 56 * 1024 * 1024

PROJ_TM = 512
PROJ_TN = 768
ATTN_TQ = 2 * MOBA_BLOCK
ATTN_TK = 2 * MOBA_BLOCK
MOBA_HEADS_PER_STEP = 3
DIFF_HEADS_PER_STEP = 3
DIFF_VMEM_LIMIT_BYTES = 60 * 1024 * 1024
CAST_CHUNKS = 32
OUT_TM = 512
MLP_TM = 512
MLP_TF = 2048
MLP_VMEM_LIMIT_BYTES = 58 * 1024 * 1024

NEG_INF = float("-inf")
NEG_BIG = -1e30
LOG2E = 1.4426950408889634


def _cparams(sem, vmem_limit_bytes=VMEM_LIMIT_BYTES):
    return pltpu.CompilerParams(dimension_semantics=sem, vmem_limit_bytes=vmem_limit_bytes)


def _rms_norm_rows(x, g):
    return x * lax.rsqrt(jnp.mean(x * x, axis=-1, keepdims=True) + EPS) * g


def _gelu_tanh(x):
    c = math.sqrt(2.0 / math.pi)
    return 0.5 * x * (1.0 + jnp.tanh(c * (x + 0.044715 * (x * x * x))))


def _in_proj_kernel(x_ref, g_ref, w_ref, cm_ref, sm_ref, cd_ref, sdl_ref, sdh_ref,
                    lng_ref, lnb_ref, ws_ref, bs_ref, o_ref, kmean_ref, sgu_ref):
    h = _rms_norm_rows(x_ref[...], g_ref[...]).astype(jnp.bfloat16)

    def rope_moba(xh):
        return xh * cm_ref[...] + pltpu.roll(xh, HEAD_DIM // 2, 1) * sm_ref[...]

    def rope_diff(xh):
        return (xh * cd_ref[...]
                + pltpu.roll(xh, HEAD_DIM - DIFF_QK_DIM // 2, 1) * sdl_ref[...]
                + pltpu.roll(xh, DIFF_QK_DIM // 2, 1) * sdh_ref[...])

    def group_dot(grp):
        return jnp.dot(h, w_ref[:, grp * PROJ_TN:(grp + 1) * PROJ_TN], preferred_element_type=jnp.float32)

    def group_epilogue(grp, acc, fn):
        for hd in range(PROJ_TN // HEAD_DIM):
            sl = slice(hd * HEAD_DIM, (hd + 1) * HEAD_DIM)
            r = acc[:, sl] if fn is None else fn(acc[:, sl])
            o_ref[:, grp * PROJ_TN + hd * HEAD_DIM:grp * PROJ_TN + (hd + 1) * HEAD_DIM] = r.astype(o_ref.dtype)
            if grp == 1:
                for b in range(PROJ_TM // MOBA_BLOCK):
                    blk = r[b * MOBA_BLOCK:(b + 1) * MOBA_BLOCK]
                    kmean_ref[0, b:b + 1, sl] = jnp.mean(blk, axis=0, keepdims=True)

    us = jnp.dot(h, w_ref[:, QKV_WIDTH:QKV_WIDTH + SGU_WIDTH], preferred_element_type=jnp.float32)
    vs = jnp.dot(h, w_ref[:, QKV_WIDTH + SGU_WIDTH:], preferred_element_type=jnp.float32)
    acc_q = group_dot(0)

    row = lax.broadcasted_iota(jnp.int32, (SGU_CHUNK, SGU_CHUNK), 0)
    col = lax.broadcasted_iota(jnp.int32, (SGU_CHUNK, SGU_CHUNK), 1)
    for g in range(SGU_GROUPS):
        sl = slice(g * HEAD_DIM, (g + 1) * HEAD_DIM)
        u = _gelu_tanh(us[:, sl])
        v = _gelu_tanh(vs[:, sl])
        mu = jnp.mean(v, axis=-1, keepdims=True)
        vc = v - mu
        var = jnp.mean(vc * vc, axis=-1, keepdims=True)
        vn = (vc * lax.rsqrt(var + EPS) * lng_ref[g:g + 1, :] + lnb_ref[g:g + 1, :]).astype(jnp.bfloat16)
        w = jnp.where(row >= col, ws_ref[g], 0.0).astype(jnp.bfloat16)
        bias = bs_ref[g]
        for c in range(PROJ_TM // SGU_CHUNK):
            rs = slice(c * SGU_CHUNK, (c + 1) * SGU_CHUNK)
            mixed = jnp.dot(w, vn[rs], preferred_element_type=jnp.float32) + bias
            sgu_ref[rs, sl] = (u[rs] * mixed).astype(sgu_ref.dtype)

    group_epilogue(0, acc_q, rope_moba)
    for grp, fn in ((1, rope_moba), (3, rope_diff), (4, rope_diff), (2, None), (5, None)):
        group_epilogue(grp, group_dot(grp), fn)


def _in_proj(x, g, w_in, cm, sm, cd, sdl, sdh, ln_g, ln_b, w_s, b_s):
    tm = PROJ_TM
    row_tab = pl.BlockSpec((tm, HEAD_DIM), lambda i: (i, 0))
    full = lambda shape: pl.BlockSpec(shape, lambda i: (0,) * len(shape))
    return pl.pallas_call(
        _in_proj_kernel,
        grid=(SEQ // tm,),
        in_specs=[
            pl.BlockSpec((tm, D_MODEL), lambda i: (i, 0)),
            pl.BlockSpec((1, D_MODEL), lambda i: (0, 0)),
            pl.BlockSpec((D_MODEL, QKV_WIDTH + 2 * SGU_WIDTH), lambda i: (0, 0), pipeline_mode=pl.Buffered(1)),
            row_tab, row_tab, row_tab, row_tab, row_tab,
            full((SGU_GROUPS, HEAD_DIM)),
            full((SGU_GROUPS, HEAD_DIM)),
            full((SGU_GROUPS, SGU_CHUNK, SGU_CHUNK)),
            full((SGU_GROUPS, SGU_CHUNK, 1)),
        ],
        out_specs=[
            pl.BlockSpec((tm, QKV_WIDTH), lambda i: (i, 0)),
            pl.BlockSpec((1, tm // MOBA_BLOCK, MOBA_WIDTH), lambda i: (i, 0, 0)),
            pl.BlockSpec((tm, SGU_WIDTH), lambda i: (i, 0)),
        ],
        out_shape=[
            jax.ShapeDtypeStruct((SEQ, QKV_WIDTH), jnp.bfloat16),
            jax.ShapeDtypeStruct((SEQ // tm, tm // MOBA_BLOCK, MOBA_WIDTH), jnp.float32),
            jax.ShapeDtypeStruct((SEQ, SGU_WIDTH), jnp.bfloat16),
        ],
        compiler_params=_cparams(("arbitrary",)),
        name="in_proj",
    )(x, g, w_in, cm, sm, cd, sdl, sdh, ln_g, ln_b, w_s, b_s)


class _Stream(NamedTuple):
    qk_own: Callable
    qk_past: Callable
    v_tile: Callable
    own_mask: Callable
    sa_ref: object
    sb_ref: object
    m_ref: object
    acc_ref: object
    c: float


def _scores(q, k):
    return lax.dot_general(q, k, (((1,), (1,)), ((), ())), preferred_element_type=jnp.float32)


def _with_ones(v):
    return jnp.concatenate([v, jnp.ones(v.shape, v.dtype)], axis=1)


def _lane_tile(x, n):
    return jnp.concatenate([x] * n, axis=1)


def _softmax_pv(st, s_ref, j, mask_fn=None):
    s = s_ref[...]
    n_rep = s.shape[1] // HEAD_DIM
    if mask_fn is not None:
        s = mask_fn(s)
        m_new = jnp.max(s, axis=-1, keepdims=True)
        p = jnp.exp2((s - m_new) * st.c).astype(jnp.bfloat16)
        st.acc_ref[...] = jnp.dot(p, st.v_tile(j), preferred_element_type=jnp.float32)
        st.m_ref[...] = jnp.broadcast_to(m_new, st.m_ref.shape)
        return
    m_prev = st.m_ref[...]
    m_new = jnp.maximum(m_prev, jnp.max(s, axis=-1, keepdims=True))
    alpha = jnp.exp2((m_prev - m_new) * st.c)
    p = jnp.exp2((s - _lane_tile(m_new, n_rep)) * st.c).astype(jnp.bfloat16)
    st.acc_ref[...] = (_lane_tile(alpha, 2) * st.acc_ref[...]
                       + jnp.dot(p, st.v_tile(j), preferred_element_type=jnp.float32))
    st.m_ref[...] = m_new


def _flash_pipeline(n_past, streams):
    for st in streams:
        st.qk_own(st.sa_ref)
    for st in streams:
        st.qk_past(0, st.sb_ref)
        _softmax_pv(st, st.sa_ref, n_past, st.own_mask)

    def pair(jj, carry):
        j = 2 * jj
        for st in streams:
            st.qk_past(j + 1, st.sa_ref)
            _softmax_pv(st, st.sb_ref, j)
        for st in streams:
            st.qk_past(j + 2, st.sb_ref)
            _softmax_pv(st, st.sa_ref, j + 1)
        return carry

    lax.fori_loop(0, n_past // 2, pair, 0)

    @pl.when(n_past % 2 == 1)
    def _():
        for st in streams:
            _softmax_pv(st, st.sb_ref, n_past - 1)


def _cast_specs(w, layer):
    rows, cols = w.shape[1], w.shape[2]
    chunk = rows // CAST_CHUNKS
    assert chunk * CAST_CHUNKS == rows and chunk % 16 == 0
    steps_per_group = SEQ // ATTN_TQ

    def chunk_index(h, i):
        return jnp.minimum(h * steps_per_group + i, CAST_CHUNKS - 1)

    return (pl.BlockSpec((None, chunk, cols), lambda h, i: (layer, chunk_index(h, i), 0)),
            pl.BlockSpec((chunk, cols), lambda h, i: (chunk_index(h, i), 0)),
            jax.ShapeDtypeStruct((rows, cols), jnp.bfloat16))


def _cast_chunks(src_refs, dst_refs):
    for src, dst in zip(src_refs, dst_refs, strict=True):
        dst[...] = src[...].astype(dst.dtype)


def _moba_stream(i, q, k_ref, v_ref, kmean, qa_ref, sa_ref, sb_ref, m_ref, acc_ref):
    tq, tk = ATTN_TQ, ATTN_TK
    k_hi = kmean.astype(jnp.bfloat16)
    rest = kmean - k_hi.astype(jnp.float32)
    k_mid = rest.astype(jnp.bfloat16)
    k_lo = (rest - k_mid.astype(jnp.float32)).astype(jnp.bfloat16)
    gate = _scores(k_hi, q) + _scores(k_mid, q) + _scores(k_lo, q)
    blk = lax.broadcasted_iota(jnp.int32, (N_BLOCKS, tq), 0)
    own = 2 * i + jnp.where(lax.broadcasted_iota(jnp.int32, (N_BLOCKS, tq), 1) >= MOBA_BLOCK, 1, 0)
    gate = jnp.where(blk < own, gate, NEG_INF)
    blk_f = blk.astype(jnp.float32)
    sel = jnp.zeros((N_BLOCKS, tq), jnp.float32)
    for _ in range(MOBA_TOPK):
        mx = jnp.max(gate, axis=0, keepdims=True)
        cand = jnp.where((gate == mx) & (gate > NEG_INF), blk_f, float(N_BLOCKS))
        pick = blk_f == jnp.min(cand, axis=0, keepdims=True)
        sel = jnp.where(pick, 1.0, sel)
        gate = jnp.where(pick, NEG_INF, gate)
    sel = jnp.concatenate([sel, jnp.zeros((HEAD_DIM - N_BLOCKS, tq), jnp.float32)], axis=0).T
    qa_ref[:, :HEAD_DIM] = q
    qa_ref[:, HEAD_DIM:] = jnp.where(sel > 0.0, 0.0, NEG_BIG).astype(jnp.bfloat16)
    lane = lax.broadcasted_iota(jnp.int32, (tq, HEAD_DIM), 1)
    picked_first = jnp.sum(jnp.where(lane == 2 * i, sel, 0.0), axis=-1, keepdims=True) > 0.0

    def own_mask(s):
        r = lax.broadcasted_iota(jnp.int32, s.shape, 0)
        col = lax.broadcasted_iota(jnp.int32, s.shape, 1)
        rr = lax.broadcasted_iota(jnp.int32, (tq, 1), 0)
        first_col = jnp.where(picked_first | (rr < MOBA_BLOCK), 0, MOBA_BLOCK)
        return jnp.where(col <= r, jnp.where(col >= first_col, s, NEG_INF), NEG_INF)

    def qk_own(s_ref):
        s_ref[...] = _scores(qa_ref[:, :HEAD_DIM], k_ref[pl.ds(pl.multiple_of(i * tk, tk), tk), :])

    def qk_past(j, s_ref):
        erow = lax.broadcasted_iota(jnp.int32, (tk, HEAD_DIM), 0)
        elane = lax.broadcasted_iota(jnp.int32, (tk, HEAD_DIM), 1)
        onehot = jnp.where(elane == 2 * j + jnp.where(erow >= MOBA_BLOCK, 1, 0), 1.0, 0.0).astype(jnp.bfloat16)
        ka = jnp.concatenate([k_ref[pl.ds(pl.multiple_of(j * tk, tk), tk), :], onehot], axis=1)
        s_ref[...] = _scores(qa_ref[...], ka)

    def v_tile(j):
        return _with_ones(v_ref[pl.ds(pl.multiple_of(j * tk, tk), tk), :])

    return _Stream(qk_own, qk_past, v_tile, own_mask, sa_ref, sb_ref, m_ref, acc_ref,
                   (HEAD_DIM ** -0.5) * LOG2E)


def _moba_kernel(n_cast, q_ref, k_ref, v_ref, kmean_ref, *refs):
    cast_src, (o_ref, *cast_dst) = refs[:n_cast], refs[n_cast:2 * n_cast + 1]
    qa_ref, sa_ref, sb_ref, m_ref, acc_ref = refs[2 * n_cast + 1:]
    _cast_chunks(cast_src, cast_dst)
    i = pl.program_id(1)
    streams = []
    for h in range(MOBA_HEADS_PER_STEP):
        sl = slice(h * HEAD_DIM, (h + 1) * HEAD_DIM)
        streams.append(_moba_stream(i, q_ref[:, sl], k_ref.at[:, sl], v_ref.at[:, sl], kmean_ref[:, sl],
                                    qa_ref.at[h], sa_ref.at[h], sb_ref.at[h], m_ref.at[h], acc_ref.at[h]))
    _flash_pipeline(i, streams)
    for h in range(MOBA_HEADS_PER_STEP):
        acc = acc_ref[h]
        o_ref[:, h * HEAD_DIM:(h + 1) * HEAD_DIM] = (acc[:, :HEAD_DIM] / acc[:, HEAD_DIM:]).astype(o_ref.dtype)


def _moba_attention(qkv, kmean, casts):
    tq, tk, hp = ATTN_TQ, ATTN_TK, MOBA_HEADS_PER_STEP
    w = hp * HEAD_DIM
    groups = MOBA_HEADS // hp
    assert groups * (SEQ // tq) >= CAST_CHUNKS
    cast_in, cast_out, cast_shape = zip(*[_cast_specs(cw, cl) for cw, cl in casts])
    return pl.pallas_call(
        functools.partial(_moba_kernel, len(casts)),
        grid=(groups, SEQ // tq),
        in_specs=[
            pl.BlockSpec((tq, w), lambda h, i: (i, h)),
            pl.BlockSpec((SEQ, w), lambda h, i: (0, groups + h)),
            pl.BlockSpec((SEQ, w), lambda h, i: (0, 2 * groups + h)),
            pl.BlockSpec((N_BLOCKS, w), lambda h, i: (0, h)),
            *cast_in,
        ],
        out_specs=[pl.BlockSpec((tq, w), lambda h, i: (i, h)), *cast_out],
        out_shape=[jax.ShapeDtypeStruct((SEQ, MOBA_WIDTH), jnp.bfloat16), *cast_shape],
        scratch_shapes=[
            pltpu.VMEM((hp, tq, 2 * HEAD_DIM), jnp.bfloat16),
            pltpu.VMEM((hp, tq, tk), jnp.float32),
            pltpu.VMEM((hp, tq, tk), jnp.float32),
            pltpu.VMEM((hp, tq, HEAD_DIM), jnp.float32),
            pltpu.VMEM((hp, tq, 2 * HEAD_DIM), jnp.float32),
        ],
        compiler_params=_cparams(("arbitrary", "arbitrary")),
        name="moba_attention",
    )(qkv, qkv, qkv, kmean, *[cw for cw, _ in casts])


def _diff_stream(i, q, k_ref, v_ref, q2_ref, sa_ref, sb_ref, m_ref, acc_ref):
    tq, tk = ATTN_TQ, ATTN_TK
    q = q * (DIFF_QK_DIM ** -0.5)
    lane = lax.broadcasted_iota(jnp.int32, (tq, HEAD_DIM), 1)
    zero = jnp.zeros_like(q)
    q2_ref[:tq, :] = jnp.where(lane < DIFF_QK_DIM, q, zero)
    q2_ref[tq:, :] = jnp.where(lane >= DIFF_QK_DIM, q, zero)

    def causal_mask(s):
        r = lax.broadcasted_iota(jnp.int32, s.shape, 0)
        r = jnp.where(r >= tq, r - tq, r)
        col = lax.broadcasted_iota(jnp.int32, s.shape, 1)
        return jnp.where(col <= r, s, NEG_INF)

    def qk_past(j, s_ref):
        s_ref[...] = _scores(q2_ref[...], k_ref[pl.ds(pl.multiple_of(j * tk, tk), tk), :])

    def v_tile(j):
        return _with_ones(v_ref[pl.ds(pl.multiple_of(j * tk, tk), tk), :])

    return _Stream(functools.partial(qk_past, i), qk_past, v_tile, causal_mask,
                   sa_ref, sb_ref, m_ref, acc_ref, LOG2E)


def _diff_kernel(lam_init, n_cast, q_ref, k_ref, v_ref, lp_ref, g_ref, *refs):
    cast_src, (o_ref, *cast_dst) = refs[:n_cast], refs[n_cast:2 * n_cast + 1]
    q2_ref, sa_ref, sb_ref, m_ref, acc_ref = refs[2 * n_cast + 1:]
    _cast_chunks(cast_src, cast_dst)
    i = pl.program_id(1)
    tq = ATTN_TQ
    streams = []
    for h in range(DIFF_HEADS_PER_STEP):
        sl = slice(h * HEAD_DIM, (h + 1) * HEAD_DIM)
        streams.append(_diff_stream(i, q_ref[:, sl], k_ref.at[:, sl], v_ref.at[:, sl],
                                    q2_ref.at[h], sa_ref.at[h], sb_ref.at[h], m_ref.at[h], acc_ref.at[h]))
    _flash_pipeline(i, streams)

    lp = lp_ref[...]
    lam = (jnp.exp(jnp.sum(lp[0:1] * lp[1:2], axis=-1, keepdims=True))
           - jnp.exp(jnp.sum(lp[2:3] * lp[3:4], axis=-1, keepdims=True)) + lam_init)
    for h in range(DIFF_HEADS_PER_STEP):
        acc = acc_ref[h]
        o = acc[:, :HEAD_DIM] / acc[:, HEAD_DIM:]
        o = o[:tq] - lam * o[tq:]
        o = _rms_norm_rows(o, g_ref[...]) * (1.0 - lam_init)
        o_ref[:, h * HEAD_DIM:(h + 1) * HEAD_DIM] = o.astype(o_ref.dtype)


def _diff_attention(qkv, lam_params, subln_g, lam_init, casts):
    tq, tk, hp = ATTN_TQ, ATTN_TK, DIFF_HEADS_PER_STEP
    w = hp * HEAD_DIM
    groups = DIFF_HEADS // hp
    base = 3 * MOBA_WIDTH // w
    assert groups * (SEQ // tq) >= CAST_CHUNKS
    cast_in, cast_out, cast_shape = zip(*[_cast_specs(cw, cl) for cw, cl in casts])
    return pl.pallas_call(
        functools.partial(_diff_kernel, lam_init, len(casts)),
        grid=(groups, SEQ // tq),
        in_specs=[
            pl.BlockSpec((tq, w), lambda h, i: (i, base + h)),
            pl.BlockSpec((SEQ, w), lambda h, i: (0, base + groups + h)),
            pl.BlockSpec((SEQ, w), lambda h, i: (0, base + 2 * groups + h)),
            pl.BlockSpec((4, DIFF_QK_DIM), lambda h, i: (0, 0)),
            pl.BlockSpec((1, HEAD_DIM), lambda h, i: (0, 0)),
            *cast_in,
        ],
        out_specs=[pl.BlockSpec((tq, w), lambda h, i: (i, h)), *cast_out],
        out_shape=[jax.ShapeDtypeStruct((SEQ, DIFF_WIDTH), jnp.bfloat16), *cast_shape],
        scratch_shapes=[
            pltpu.VMEM((hp, 2 * tq, HEAD_DIM), jnp.bfloat16),
            pltpu.VMEM((hp, 2 * tq, tk), jnp.float32),
            pltpu.VMEM((hp, 2 * tq, tk), jnp.float32),
            pltpu.VMEM((hp, 2 * tq, HEAD_DIM), jnp.float32),
            pltpu.VMEM((hp, 2 * tq, 2 * HEAD_DIM), jnp.float32),
        ],
        compiler_params=_cparams(("arbitrary", "arbitrary"), DIFF_VMEM_LIMIT_BYTES),
        name="diff_attention",
    )(qkv, qkv, qkv, lam_params, subln_g, *[cw for cw, _ in casts])


def _out_proj_kernel(x_ref, a_ref, b_ref, c_ref, w_ref, g_ref, o_ref, h_ref):
    half = OUT_TM // 2
    for r0 in range(0, OUT_TM, half):
        rows = slice(r0, r0 + half)
        acc = jnp.dot(a_ref[rows, :], w_ref[0:MOBA_WIDTH, :], preferred_element_type=jnp.float32)
        acc += jnp.dot(b_ref[rows, :], w_ref[MOBA_WIDTH:MOBA_WIDTH + DIFF_WIDTH, :],
                       preferred_element_type=jnp.float32)
        acc += jnp.dot(c_ref[rows, :], w_ref[MOBA_WIDTH + DIFF_WIDTH:, :], preferred_element_type=jnp.float32)
        y = x_ref[rows, :] + acc
        o_ref[rows, :] = y
        h_ref[rows, :] = _rms_norm_rows(y, g_ref[...]).astype(h_ref.dtype)


def _out_proj(x, moba_o, diff_o, sgu_o, w_out, g_mlp):
    tm = OUT_TM
    return pl.pallas_call(
        _out_proj_kernel,
        grid=(SEQ // tm,),
        in_specs=[
            pl.BlockSpec((tm, D_MODEL), lambda i: (i, 0)),
            pl.BlockSpec((tm, MOBA_WIDTH), lambda i: (i, 0)),
            pl.BlockSpec((tm, DIFF_WIDTH), lambda i: (i, 0)),
            pl.BlockSpec((tm, SGU_WIDTH), lambda i: (i, 0)),
            pl.BlockSpec((D_MODEL, D_MODEL), lambda i: (0, 0)),
            pl.BlockSpec((1, D_MODEL), lambda i: (0, 0)),
        ],
        out_specs=[
            pl.BlockSpec((tm, D_MODEL), lambda i: (i, 0)),
            pl.BlockSpec((tm, D_MODEL), lambda i: (i, 0)),
        ],
        out_shape=[
            jax.ShapeDtypeStruct((SEQ, D_MODEL), jnp.float32),
            jax.ShapeDtypeStruct((SEQ, D_MODEL), jnp.bfloat16),
        ],
        compiler_params=_cparams(("arbitrary",)),
        name="out_proj",
    )(x, moba_o, diff_o, sgu_o, w_out, g_mlp)


def _mlp_kernel(final, x_ref, h_ref, w1_ref, w2_ref, gf_ref, o_ref):
    j = pl.program_id(1)

    @pl.when(j == 0)
    def _():
        o_ref[...] = x_ref[...]

    a = jnp.dot(h_ref[...], w1_ref[...], preferred_element_type=jnp.float32)
    a = jnp.square(jnp.maximum(a, 0.0)).astype(jnp.bfloat16)
    o_ref[...] += jnp.dot(a, w2_ref[...], preferred_element_type=jnp.float32)

    if final:
        @pl.when(j == pl.num_programs(1) - 1)
        def _():
            o_ref[...] = _rms_norm_rows(o_ref[...], gf_ref[...])


def _mlp(x, h, w1, w2, g_final, final):
    tm, tf = MLP_TM, MLP_TF
    return pl.pallas_call(
        functools.partial(_mlp_kernel, final),
        grid=(SEQ // tm, D_FF // tf),
        in_specs=[
            pl.BlockSpec((tm, D_MODEL), lambda i, j: (i, 0)),
            pl.BlockSpec((tm, D_MODEL), lambda i, j: (i, 0)),
            pl.BlockSpec((D_MODEL, tf), lambda i, j: (0, j)),
            pl.BlockSpec((tf, D_MODEL), lambda i, j: (j, 0)),
            pl.BlockSpec((1, D_MODEL), lambda i, j: (0, 0)),
        ],
        out_specs=pl.BlockSpec((tm, D_MODEL), lambda i, j: (i, 0)),
        out_shape=jax.ShapeDtypeStruct((SEQ, D_MODEL), jnp.float32),
        compiler_params=_cparams(("arbitrary", "arbitrary"), MLP_VMEM_LIMIT_BYTES),
        name="mlp",
    )(x, h, w1, w2, g_final)


def _rope_tables():
    pos = jnp.arange(SEQ, dtype=jnp.float32)[:, None]

    def tab(dim):
        inv = 1.0 / (ROPE_THETA ** (jnp.arange(0, dim, 2, dtype=jnp.float32) / dim))
        ang = pos * inv[None, :]
        return jnp.cos(ang), jnp.sin(ang)

    cm, sm, cd, sd = lax.optimization_barrier((*tab(HEAD_DIM), *tab(DIFF_QK_DIM)))
    zd = jnp.zeros_like(sd)
    return (jnp.concatenate([cm, cm], axis=1), jnp.concatenate([-sm, sm], axis=1),
            jnp.concatenate([cd, cd, cd, cd], axis=1),
            jnp.concatenate([-sd, zd, -sd, zd], axis=1),
            jnp.concatenate([zd, sd, zd, sd], axis=1))


def kernel(x, attn_norm_g, w_in, diff_lambda, diff_subln_g, sgu_ln_g, sgu_ln_b, sgu_w, sgu_b,
           w_out, mlp_norm_g, w_mlp_in, w_mlp_out, final_norm_g):
    assert x.shape == (1, SEQ, D_MODEL)
    bf = jnp.bfloat16
    xs = x.reshape(SEQ, D_MODEL)
    cm, sm, cd, sdl, sdh = _rope_tables()
    gf = final_norm_g.reshape(1, D_MODEL)
    w_in_l = w_in[0].astype(bf)
    for l in range(DEPTH):
        lam_init = 0.8 - 0.6 * math.exp(-0.3 * l)
        g_attn = attn_norm_g[l].reshape(1, D_MODEL)
        qkv, kmean, sgu_o = _in_proj(xs, g_attn, w_in_l, cm, sm, cd, sdl, sdh, sgu_ln_g[l], sgu_ln_b[l],
                                     sgu_w[l], sgu_b[l].reshape(SGU_GROUPS, SGU_CHUNK, 1))
        kmean = kmean.reshape(N_BLOCKS, MOBA_WIDTH)
        moba_o, w1_l, w_out_l = _moba_attention(qkv, kmean, [(w_mlp_in, l), (w_out, l)])
        diff_casts = [(w_mlp_out, l)] + ([(w_in, l + 1)] if l + 1 < DEPTH else [])
        diff_o, w2_l, *w_in_next = _diff_attention(qkv, diff_lambda[l], diff_subln_g[l].reshape(1, HEAD_DIM),
                                                   lam_init, diff_casts)
        xs, h_mlp = _out_proj(xs, moba_o, diff_o, sgu_o, w_out_l, mlp_norm_g[l].reshape(1, D_MODEL))
        xs = _mlp(xs, h_mlp, w1_l, w2_l, gf, l == DEPTH - 1)
        if w_in_next:
            w_in_l = w_in_next[0]
    return xs.reshape(1, SEQ, D_MODEL)
```

```python
import functools
import math
from typing import Callable, NamedTuple

import jax
import jax.numpy as jnp
from jax import lax
from jax.experimental import pallas as pl
from jax.experimental.pallas import tpu as pltpu

D_MODEL = 2048
SEQ = 8192
DEPTH = 4
HEAD_DIM = 128
MOBA_HEADS = 6
DIFF_HEADS = 6
SGU_GROUPS = 4
MOBA_WIDTH = MOBA_HEADS * HEAD_DIM
DIFF_WIDTH = DIFF_HEADS * HEAD_DIM
SGU_WIDTH = SGU_GROUPS * HEAD_DIM
DIFF_QK_DIM = HEAD_DIM // 2
MOBA_BLOCK = 256
MOBA_TOPK = 3
SGU_CHUNK = 128
D_FF = 4 * D_MODEL
ROPE_THETA = 10000.0
EPS = 1e-6
QKV_WIDTH = 3 * MOBA_WIDTH + 3 * DIFF_WIDTH
N_BLOCKS = SEQ // MOBA_BLOCK

VMEM_LIMIT_BYTES = 56 * 1024 * 1024

PROJ_TM = 512
PROJ_TN = 768
ATTN_TQ = 2 * MOBA_BLOCK
ATTN_TK = 2 * MOBA_BLOCK
MOBA_HEADS_PER_STEP = 3
DIFF_HEADS_PER_STEP = 2
CAST_CHUNKS = 32
OUT_TM = 512
MLP_TM = 512
MLP_TF = 2048
MLP_VMEM_LIMIT_BYTES = 58 * 1024 * 1024

NEG_INF = float("-inf")
NEG_BIG = -1e30
LOG2E = 1.4426950408889634


def _cparams(sem, vmem_limit_bytes=VMEM_LIMIT_BYTES):
    return pltpu.CompilerParams(dimension_semantics=sem, vmem_limit_bytes=vmem_limit_bytes)


def _rms_norm_rows(x, g):
    return x * lax.rsqrt(jnp.mean(x * x, axis=-1, keepdims=True) + EPS) * g


def _gelu_tanh(x):
    c = math.sqrt(2.0 / math.pi)
    return 0.5 * x * (1.0 + jnp.tanh(c * (x + 0.044715 * (x * x * x))))


def _in_proj_kernel(x_ref, g_ref, w_ref, cm_ref, sm_ref, cd_ref, sdl_ref, sdh_ref,
                    lng_ref, lnb_ref, ws_ref, bs_ref, o_ref, kmean_ref, sgu_ref):
    h = _rms_norm_rows(x_ref[...], g_ref[...]).astype(jnp.bfloat16)

    def rope_moba(xh):
        return xh * cm_ref[...] + pltpu.roll(xh, HEAD_DIM // 2, 1) * sm_ref[...]

    def rope_diff(xh):
        return (xh * cd_ref[...]
                + pltpu.roll(xh, HEAD_DIM - DIFF_QK_DIM // 2, 1) * sdl_ref[...]
                + pltpu.roll(xh, DIFF_QK_DIM // 2, 1) * sdh_ref[...])

    def group_dot(grp):
        return jnp.dot(h, w_ref[:, grp * PROJ_TN:(grp + 1) * PROJ_TN], preferred_element_type=jnp.float32)

    def group_epilogue(grp, acc, fn):
        for hd in range(PROJ_TN // HEAD_DIM):
            sl = slice(hd * HEAD_DIM, (hd + 1) * HEAD_DIM)
            r = acc[:, sl] if fn is None else fn(acc[:, sl])
            o_ref[:, grp * PROJ_TN + hd * HEAD_DIM:grp * PROJ_TN + (hd + 1) * HEAD_DIM] = r.astype(o_ref.dtype)
            if grp == 1:
                for b in range(PROJ_TM // MOBA_BLOCK):
                    blk = r[b * MOBA_BLOCK:(b + 1) * MOBA_BLOCK]
                    kmean_ref[0, b:b + 1, sl] = jnp.mean(blk, axis=0, keepdims=True)

    us = jnp.dot(h, w_ref[:, QKV_WIDTH:QKV_WIDTH + SGU_WIDTH], preferred_element_type=jnp.float32)
    vs = jnp.dot(h, w_ref[:, QKV_WIDTH + SGU_WIDTH:], preferred_element_type=jnp.float32)
    acc_q = group_dot(0)

    row = lax.broadcasted_iota(jnp.int32, (SGU_CHUNK, SGU_CHUNK), 0)
    col = lax.broadcasted_iota(jnp.int32, (SGU_CHUNK, SGU_CHUNK), 1)
    for g in range(SGU_GROUPS):
        sl = slice(g * HEAD_DIM, (g + 1) * HEAD_DIM)
        u = _gelu_tanh(us[:, sl])
        v = _gelu_tanh(vs[:, sl])
        mu = jnp.mean(v, axis=-1, keepdims=True)
        vc = v - mu
        var = jnp.mean(vc * vc, axis=-1, keepdims=True)
        vn = (vc * lax.rsqrt(var + EPS) * lng_ref[g:g + 1, :] + lnb_ref[g:g + 1, :]).astype(jnp.bfloat16)
        w = jnp.where(row >= col, ws_ref[g], 0.0).astype(jnp.bfloat16)
        bias = bs_ref[g]
        for c in range(PROJ_TM // SGU_CHUNK):
            rs = slice(c * SGU_CHUNK, (c + 1) * SGU_CHUNK)
            mixed = jnp.dot(w, vn[rs], preferred_element_type=jnp.float32) + bias
            sgu_ref[rs, sl] = (u[rs] * mixed).astype(sgu_ref.dtype)

    group_epilogue(0, acc_q, rope_moba)
    for grp, fn in ((1, rope_moba), (3, rope_diff), (4, rope_diff), (2, None), (5, None)):
        group_epilogue(grp, group_dot(grp), fn)


def _in_proj(x, g, w_in, cm, sm, cd, sdl, sdh, ln_g, ln_b, w_s, b_s):
    tm = PROJ_TM
    row_tab = pl.BlockSpec((tm, HEAD_DIM), lambda i: (i, 0))
    full = lambda shape: pl.BlockSpec(shape, lambda i: (0,) * len(shape))
    return pl.pallas_call(
        _in_proj_kernel,
        grid=(SEQ // tm,),
        in_specs=[
            pl.BlockSpec((tm, D_MODEL), lambda i: (i, 0)),
            pl.BlockSpec((1, D_MODEL), lambda i: (0, 0)),
            pl.BlockSpec((D_MODEL, QKV_WIDTH + 2 * SGU_WIDTH), lambda i: (0, 0), pipeline_mode=pl.Buffered(1)),
            row_tab, row_tab, row_tab, row_tab, row_tab,
            full((SGU_GROUPS, HEAD_DIM)),
            full((SGU_GROUPS, HEAD_DIM)),
            full((SGU_GROUPS, SGU_CHUNK, SGU_CHUNK)),
            full((SGU_GROUPS, SGU_CHUNK, 1)),
        ],
        out_specs=[
            pl.BlockSpec((tm, QKV_WIDTH), lambda i: (i, 0)),
            pl.BlockSpec((1, tm // MOBA_BLOCK, MOBA_WIDTH), lambda i: (i, 0, 0)),
            pl.BlockSpec((tm, SGU_WIDTH), lambda i: (i, 0)),
        ],
        out_shape=[
            jax.ShapeDtypeStruct((SEQ, QKV_WIDTH), jnp.bfloat16),
            jax.ShapeDtypeStruct((SEQ // tm, tm // MOBA_BLOCK, MOBA_WIDTH), jnp.float32),
            jax.ShapeDtypeStruct((SEQ, SGU_WIDTH), jnp.bfloat16),
        ],
        compiler_params=_cparams(("arbitrary",)),
        name="in_proj",
    )(x, g, w_in, cm, sm, cd, sdl, sdh, ln_g, ln_b, w_s, b_s)


class _Stream(NamedTuple):
    qk_own: Callable
    qk_past: Callable
    v_tile: Callable
    own_mask: Callable
    sa_ref: object
    sb_ref: object
    m_ref: object
    acc_ref: object
    c: float


def _scores(q, k):
    return lax.dot_general(q, k, (((1,), (1,)), ((), ())), preferred_element_type=jnp.float32)


def _with_ones(v):
    return jnp.concatenate([v, jnp.ones(v.shape, v.dtype)], axis=1)


def _lane_tile(x, n):
    return jnp.concatenate([x] * n, axis=1)


def _softmax_pv(st, s_ref, j, mask_fn=None):
    s = s_ref[...]
    n_rep = s.shape[1] // HEAD_DIM
    if mask_fn is not None:
        s = mask_fn(s)
        m_new = jnp.max(s, axis=-1, keepdims=True)
        p = jnp.exp2((s - m_new) * st.c).astype(jnp.bfloat16)
        st.acc_ref[...] = jnp.dot(p, st.v_tile(j), preferred_element_type=jnp.float32)
        st.m_ref[...] = jnp.broadcast_to(m_new, st.m_ref.shape)
        return
    m_prev = st.m_ref[...]
    m_new = jnp.maximum(m_prev, jnp.max(s, axis=-1, keepdims=True))
    alpha = jnp.exp2((m_prev - m_new) * st.c)
    p = jnp.exp2((s - _lane_tile(m_new, n_rep)) * st.c).astype(jnp.bfloat16)
    st.acc_ref[...] = (_lane_tile(alpha, 2) * st.acc_ref[...]
                       + jnp.dot(p, st.v_tile(j), preferred_element_type=jnp.float32))
    st.m_ref[...] = m_new


def _flash_pipeline(n_past, streams):
    for st in streams:
        st.qk_own(st.sa_ref)
    for st in streams:
        st.qk_past(0, st.sb_ref)
        _softmax_pv(st, st.sa_ref, n_past, st.own_mask)

    def pair(jj, carry):
        j = 2 * jj
        for st in streams:
            st.qk_past(j + 1, st.sa_ref)
            _softmax_pv(st, st.sb_ref, j)
        for st in streams:
            st.qk_past(j + 2, st.sb_ref)
            _softmax_pv(st, st.sa_ref, j + 1)
        return carry

    lax.fori_loop(0, n_past // 2, pair, 0)

    @pl.when(n_past % 2 == 1)
    def _():
        for st in streams:
            _softmax_pv(st, st.sb_ref, n_past - 1)


def _cast_specs(w, layer):
    rows, cols = w.shape[1], w.shape[2]
    chunk = rows // CAST_CHUNKS
    assert chunk * CAST_CHUNKS == rows and chunk % 16 == 0
    steps_per_group = SEQ // ATTN_TQ

    def chunk_index(h, i):
        return jnp.minimum(h * steps_per_group + i, CAST_CHUNKS - 1)

    return (pl.BlockSpec((None, chunk, cols), lambda h, i: (layer, chunk_index(h, i), 0)),
            pl.BlockSpec((chunk, cols), lambda h, i: (chunk_index(h, i), 0)),
            jax.ShapeDtypeStruct((rows, cols), jnp.bfloat16))


def _cast_chunks(src_refs, dst_refs):
    for src, dst in zip(src_refs, dst_refs, strict=True):
        dst[...] = src[...].astype(dst.dtype)


def _moba_stream(i, q, k_ref, v_ref, kmean, qa_ref, sa_ref, sb_ref, m_ref, acc_ref):
    tq, tk = ATTN_TQ, ATTN_TK
    k_hi = kmean.astype(jnp.bfloat16)
    rest = kmean - k_hi.astype(jnp.float32)
    k_mid = rest.astype(jnp.bfloat16)
    k_lo = (rest - k_mid.astype(jnp.float32)).astype(jnp.bfloat16)
    gate = _scores(k_hi, q) + _scores(k_mid, q) + _scores(k_lo, q)
    blk = lax.broadcasted_iota(jnp.int32, (N_BLOCKS, tq), 0)
    own = 2 * i + jnp.where(lax.broadcasted_iota(jnp.int32, (N_BLOCKS, tq), 1) >= MOBA_BLOCK, 1, 0)
    gate = jnp.where(blk < own, gate, NEG_INF)
    blk_f = blk.astype(jnp.float32)
    sel = jnp.zeros((N_BLOCKS, tq), jnp.float32)
    for _ in range(MOBA_TOPK):
        mx = jnp.max(gate, axis=0, keepdims=True)
        cand = jnp.where((gate == mx) & (gate > NEG_INF), blk_f, float(N_BLOCKS))
        pick = blk_f == jnp.min(cand, axis=0, keepdims=True)
        sel = jnp.where(pick, 1.0, sel)
        gate = jnp.where(pick, NEG_INF, gate)
    sel = jnp.concatenate([sel, jnp.zeros((HEAD_DIM - N_BLOCKS, tq), jnp.float32)], axis=0).T
    qa_ref[:, :HEAD_DIM] = q
    qa_ref[:, HEAD_DIM:] = jnp.where(sel > 0.0, 0.0, NEG_BIG).astype(jnp.bfloat16)
    lane = lax.broadcasted_iota(jnp.int32, (tq, HEAD_DIM), 1)
    picked_first = jnp.sum(jnp.where(lane == 2 * i, sel, 0.0), axis=-1, keepdims=True) > 0.0

    def own_mask(s):
        r = lax.broadcasted_iota(jnp.int32, s.shape, 0)
        col = lax.broadcasted_iota(jnp.int32, s.shape, 1)
        rr = lax.broadcasted_iota(jnp.int32, (tq, 1), 0)
        first_col = jnp.where(picked_first | (rr < MOBA_BLOCK), 0, MOBA_BLOCK)
        return jnp.where(col <= r, jnp.where(col >= first_col, s, NEG_INF), NEG_INF)

    def qk_own(s_ref):
        s_ref[...] = _scores(qa_ref[:, :HEAD_DIM], k_ref[pl.ds(pl.multiple_of(i * tk, tk), tk), :])

    def qk_past(j, s_ref):
        erow = lax.broadcasted_iota(jnp.int32, (tk, HEAD_DIM), 0)
        elane = lax.broadcasted_iota(jnp.int32, (tk, HEAD_DIM), 1)
        onehot = jnp.where(elane == 2 * j + jnp.where(erow >= MOBA_BLOCK, 1, 0), 1.0, 0.0).astype(jnp.bfloat16)
        ka = jnp.concatenate([k_ref[pl.ds(pl.multiple_of(j * tk, tk), tk), :], onehot], axis=1)
        s_ref[...] = _scores(qa_ref[...], ka)

    def v_tile(j):
        return _with_ones(v_ref[pl.ds(pl.multiple_of(j * tk, tk), tk), :])

    return _Stream(qk_own, qk_past, v_tile, own_mask, sa_ref, sb_ref, m_ref, acc_ref,
                   (HEAD_DIM ** -0.5) * LOG2E)


def _moba_kernel(n_cast, q_ref, k_ref, v_ref, kmean_ref, *refs):
    cast_src, (o_ref, *cast_dst) = refs[:n_cast], refs[n_cast:2 * n_cast + 1]
    qa_ref, sa_ref, sb_ref, m_ref, acc_ref = refs[2 * n_cast + 1:]
    _cast_chunks(cast_src, cast_dst)
    i = pl.program_id(1)
    streams = []
    for h in range(MOBA_HEADS_PER_STEP):
        sl = slice(h * HEAD_DIM, (h + 1) * HEAD_DIM)
        streams.append(_moba_stream(i, q_ref[:, sl], k_ref.at[:, sl], v_ref.at[:, sl], kmean_ref[:, sl],
                                    qa_ref.at[h], sa_ref.at[h], sb_ref.at[h], m_ref.at[h], acc_ref.at[h]))
    _flash_pipeline(i, streams)
    for h in range(MOBA_HEADS_PER_STEP):
        acc = acc_ref[h]
        o_ref[:, h * HEAD_DIM:(h + 1) * HEAD_DIM] = (acc[:, :HEAD_DIM] / acc[:, HEAD_DIM:]).astype(o_ref.dtype)


def _moba_attention(qkv, kmean, casts):
    tq, tk, hp = ATTN_TQ, ATTN_TK, MOBA_HEADS_PER_STEP
    w = hp * HEAD_DIM
    groups = MOBA_HEADS // hp
    assert groups * (SEQ // tq) >= CAST_CHUNKS
    cast_in, cast_out, cast_shape = zip(*[_cast_specs(cw, cl) for cw, cl in casts])
    return pl.pallas_call(
        functools.partial(_moba_kernel, len(casts)),
        grid=(groups, SEQ // tq),
        in_specs=[
            pl.BlockSpec((tq, w), lambda h, i: (i, h)),
            pl.BlockSpec((SEQ, w), lambda h, i: (0, groups + h)),
            pl.BlockSpec((SEQ, w), lambda h, i: (0, 2 * groups + h)),
            pl.BlockSpec((N_BLOCKS, w), lambda h, i: (0, h)),
            *cast_in,
        ],
        out_specs=[pl.BlockSpec((tq, w), lambda h, i: (i, h)), *cast_out],
        out_shape=[jax.ShapeDtypeStruct((SEQ, MOBA_WIDTH), jnp.bfloat16), *cast_shape],
        scratch_shapes=[
            pltpu.VMEM((hp, tq, 2 * HEAD_DIM), jnp.bfloat16),
            pltpu.VMEM((hp, tq, tk), jnp.float32),
            pltpu.VMEM((hp, tq, tk), jnp.float32),
            pltpu.VMEM((hp, tq, HEAD_DIM), jnp.float32),
            pltpu.VMEM((hp, tq, 2 * HEAD_DIM), jnp.float32),
        ],
        compiler_params=_cparams(("arbitrary", "arbitrary")),
        name="moba_attention",
    )(qkv, qkv, qkv, kmean, *[cw for cw, _ in casts])


def _diff_stream(i, q, k_ref, v_ref, q2_ref, sa_ref, sb_ref, m_ref, acc_ref):
    tq, tk = ATTN_TQ, ATTN_TK
    q = q * (DIFF_QK_DIM ** -0.5)
    lane = lax.broadcasted_iota(jnp.int32, (tq, HEAD_DIM), 1)
    zero = jnp.zeros_like(q)
    q2_ref[:tq, :] = jnp.where(lane < DIFF_QK_DIM, q, zero)
    q2_ref[tq:, :] = jnp.where(lane >= DIFF_QK_DIM, q, zero)

    def causal_mask(s):
        r = lax.broadcasted_iota(jnp.int32, s.shape, 0)
        r = jnp.where(r >= tq, r - tq, r)
        col = lax.broadcasted_iota(jnp.int32, s.shape, 1)
        return jnp.where(col <= r, s, NEG_INF)

    def qk_past(j, s_ref):
        s_ref[...] = _scores(q2_ref[...], k_ref[pl.ds(pl.multiple_of(j * tk, tk), tk), :])

    def v_tile(j):
        return _with_ones(v_ref[pl.ds(pl.multiple_of(j * tk, tk), tk), :])

    return _Stream(functools.partial(qk_past, i), qk_past, v_tile, causal_mask,
                   sa_ref, sb_ref, m_ref, acc_ref, LOG2E)


def _diff_kernel(lam_init, n_cast, q_ref, k_ref, v_ref, lp_ref, g_ref, *refs):
    cast_src, (o_ref, *cast_dst) = refs[:n_cast], refs[n_cast:2 * n_cast + 1]
    q2_ref, sa_ref, sb_ref, m_ref, acc_ref = refs[2 * n_cast + 1:]
    _cast_chunks(cast_src, cast_dst)
    i = pl.program_id(1)
    tq = ATTN_TQ
    streams = []
    for h in range(DIFF_HEADS_PER_STEP):
        sl = slice(h * HEAD_DIM, (h + 1) * HEAD_DIM)
        streams.append(_diff_stream(i, q_ref[:, sl], k_ref.at[:, sl], v_ref.at[:, sl],
                                    q2_ref.at[h], sa_ref.at[h], sb_ref.at[h], m_ref.at[h], acc_ref.at[h]))
    _flash_pipeline(i, streams)

    lp = lp_ref[...]
    lam = (jnp.exp(jnp.sum(lp[0:1] * lp[1:2], axis=-1, keepdims=True))
           - jnp.exp(jnp.sum(lp[2:3] * lp[3:4], axis=-1, keepdims=True)) + lam_init)
    for h in range(DIFF_HEADS_PER_STEP):
        acc = acc_ref[h]
        o = acc[:, :HEAD_DIM] / acc[:, HEAD_DIM:]
        o = o[:tq] - lam * o[tq:]
        o = _rms_norm_rows(o, g_ref[...]) * (1.0 - lam_init)
        o_ref[:, h * HEAD_DIM:(h + 1) * HEAD_DIM] = o.astype(o_ref.dtype)


def _diff_attention(qkv, lam_params, subln_g, lam_init, casts):
    tq, tk, hp = ATTN_TQ, ATTN_TK, DIFF_HEADS_PER_STEP
    w = hp * HEAD_DIM
    groups = DIFF_HEADS // hp
    base = 3 * MOBA_WIDTH // w
    assert groups * (SEQ // tq) >= CAST_CHUNKS
    cast_in, cast_out, cast_shape = zip(*[_cast_specs(cw, cl) for cw, cl in casts])
    return pl.pallas_call(
        functools.partial(_diff_kernel, lam_init, len(casts)),
        grid=(groups, SEQ // tq),
        in_specs=[
            pl.BlockSpec((tq, w), lambda h, i: (i, base + h)),
            pl.BlockSpec((SEQ, w), lambda h, i: (0, base + groups + h)),
            pl.BlockSpec((SEQ, w), lambda h, i: (0, base + 2 * groups + h)),
            pl.BlockSpec((4, DIFF_QK_DIM), lambda h, i: (0, 0)),
            pl.BlockSpec((1, HEAD_DIM), lambda h, i: (0, 0)),
            *cast_in,
        ],
        out_specs=[pl.BlockSpec((tq, w), lambda h, i: (i, h)), *cast_out],
        out_shape=[jax.ShapeDtypeStruct((SEQ, DIFF_WIDTH), jnp.bfloat16), *cast_shape],
        scratch_shapes=[
            pltpu.VMEM((hp, 2 * tq, HEAD_DIM), jnp.bfloat16),
            pltpu.VMEM((hp, 2 * tq, tk), jnp.float32),
            pltpu.VMEM((hp, 2 * tq, tk), jnp.float32),
            pltpu.VMEM((hp, 2 * tq, HEAD_DIM), jnp.float32),
            pltpu.VMEM((hp, 2 * tq, 2 * HEAD_DIM), jnp.float32),
        ],
        compiler_params=_cparams(("arbitrary", "arbitrary")),
        name="diff_attention",
    )(qkv, qkv, qkv, lam_params, subln_g, *[cw for cw, _ in casts])


def _out_proj_kernel(x_ref, a_ref, b_ref, c_ref, w_ref, g_ref, o_ref, h_ref):
    half = OUT_TM // 2
    for r0 in range(0, OUT_TM, half):
        rows = slice(r0, r0 + half)
        acc = jnp.dot(a_ref[rows, :], w_ref[0:MOBA_WIDTH, :], preferred_element_type=jnp.float32)
        acc += jnp.dot(b_ref[rows, :], w_ref[MOBA_WIDTH:MOBA_WIDTH + DIFF_WIDTH, :],
                       preferred_element_type=jnp.float32)
        acc += jnp.dot(c_ref[rows, :], w_ref[MOBA_WIDTH + DIFF_WIDTH:, :], preferred_element_type=jnp.float32)
        y = x_ref[rows, :] + acc
        o_ref[rows, :] = y
        h_ref[rows, :] = _rms_norm_rows(y, g_ref[...]).astype(h_ref.dtype)


def _out_proj(x, moba_o, diff_o, sgu_o, w_out, g_mlp):
    tm = OUT_TM
    return pl.pallas_call(
        _out_proj_kernel,
        grid=(SEQ // tm,),
        in_specs=[
            pl.BlockSpec((tm, D_MODEL), lambda i: (i, 0)),
            pl.BlockSpec((tm, MOBA_WIDTH), lambda i: (i, 0)),
            pl.BlockSpec((tm, DIFF_WIDTH), lambda i: (i, 0)),
            pl.BlockSpec((tm, SGU_WIDTH), lambda i: (i, 0)),
            pl.BlockSpec((D_MODEL, D_MODEL), lambda i: (0, 0)),
            pl.BlockSpec((1, D_MODEL), lambda i: (0, 0)),
        ],
        out_specs=[
            pl.BlockSpec((tm, D_MODEL), lambda i: (i, 0)),
            pl.BlockSpec((tm, D_MODEL), lambda i: (i, 0)),
        ],
        out_shape=[
            jax.ShapeDtypeStruct((SEQ, D_MODEL), jnp.float32),
            jax.ShapeDtypeStruct((SEQ, D_MODEL), jnp.bfloat16),
        ],
        compiler_params=_cparams(("arbitrary",)),
        name="out_proj",
    )(x, moba_o, diff_o, sgu_o, w_out, g_mlp)


def _mlp_kernel(final, x_ref, h_ref, w1_ref, w2_ref, gf_ref, o_ref):
    j = pl.program_id(1)

    @pl.when(j == 0)
    def _():
        o_ref[...] = x_ref[...]

    a = jnp.dot(h_ref[...], w1_ref[...], preferred_element_type=jnp.float32)
    a = jnp.square(jnp.maximum(a, 0.0)).astype(jnp.bfloat16)
    o_ref[...] += jnp.dot(a, w2_ref[...], preferred_element_type=jnp.float32)

    if final:
        @pl.when(j == pl.num_programs(1) - 1)
        def _():
            o_ref[...] = _rms_norm_rows(o_ref[...], gf_ref[...])


def _mlp(x, h, w1, w2, g_final, final):
    tm, tf = MLP_TM, MLP_TF
    return pl.pallas_call(
        functools.partial(_mlp_kernel, final),
        grid=(SEQ // tm, D_FF // tf),
        in_specs=[
            pl.BlockSpec((tm, D_MODEL), lambda i, j: (i, 0)),
            pl.BlockSpec((tm, D_MODEL), lambda i, j: (i, 0)),
            pl.BlockSpec((D_MODEL, tf), lambda i, j: (0, j)),
            pl.BlockSpec((tf, D_MODEL), lambda i, j: (j, 0)),
            pl.BlockSpec((1, D_MODEL), lambda i, j: (0, 0)),
        ],
        out_specs=pl.BlockSpec((tm, D_MODEL), lambda i, j: (i, 0)),
        out_shape=jax.ShapeDtypeStruct((SEQ, D_MODEL), jnp.float32),
        compiler_params=_cparams(("arbitrary", "arbitrary"), MLP_VMEM_LIMIT_BYTES),
        name="mlp",
    )(x, h, w1, w2, g_final)


def _rope_tables():
    pos = jnp.arange(SEQ, dtype=jnp.float32)[:, None]
    lane = jnp.arange(HEAD_DIM)

    def angles(dim):
        inv = 1.0 / (ROPE_THETA ** (jnp.arange(0, dim, 2, dtype=jnp.float32) / dim))
        return pos * jnp.tile(inv, 2 * HEAD_DIM // dim)[None, :]

    ang_m, ang_d = angles(HEAD_DIM), angles(DIFF_QK_DIM)
    sin_m, sin_d = jnp.sin(ang_m), jnp.sin(ang_d)
    first_m = (lane % HEAD_DIM < HEAD_DIM // 2)[None, :]
    first_d = (lane % DIFF_QK_DIM < DIFF_QK_DIM // 2)[None, :]
    return (jnp.cos(ang_m), jnp.where(first_m, -sin_m, sin_m),
            jnp.cos(ang_d),
            jnp.where(first_d, -sin_d, 0.0),
            jnp.where(first_d, 0.0, sin_d))


def kernel(x, attn_norm_g, w_in, diff_lambda, diff_subln_g, sgu_ln_g, sgu_ln_b, sgu_w, sgu_b,
           w_out, mlp_norm_g, w_mlp_in, w_mlp_out, final_norm_g):
    assert x.shape == (1, SEQ, D_MODEL)
    bf = jnp.bfloat16
    xs = x.reshape(SEQ, D_MODEL)
    cm, sm, cd, sdl, sdh = _rope_tables()
    gf = final_norm_g.reshape(1, D_MODEL)
    w_in_l = w_in[0].astype(bf)
    for l in range(DEPTH):
        lam_init = 0.8 - 0.6 * math.exp(-0.3 * l)
        g_attn = attn_norm_g[l].reshape(1, D_MODEL)
        qkv, kmean, sgu_o = _in_proj(xs, g_attn, w_in_l, cm, sm, cd, sdl, sdh, sgu_ln_g[l], sgu_ln_b[l],
                                     sgu_w[l], sgu_b[l].reshape(SGU_GROUPS, SGU_CHUNK, 1))
        kmean = kmean.reshape(N_BLOCKS, MOBA_WIDTH)
        moba_o, w1_l, w_out_l = _moba_attention(qkv, kmean, [(w_mlp_in, l), (w_out, l)])
        diff_casts = [(w_mlp_out, l)] + ([(w_in, l + 1)] if l + 1 < DEPTH else [])
        diff_o, w2_l, *w_in_next = _diff_attention(qkv, diff_lambda[l], diff_subln_g[l].reshape(1, HEAD_DIM),
                                                   lam_init, diff_casts)
        xs, h_mlp = _out_proj(xs, moba_o, diff_o, sgu_o, w_out_l, mlp_norm_g[l].reshape(1, D_MODEL))
        xs = _mlp(xs, h_mlp, w1_l, w2_l, gf, l == DEPTH - 1)
        if w_in_next:
            w_in_l = w_in_next[0]
    return xs.reshape(1, SEQ, D_MODEL)
```

```python
import functools
import math
from typing import Callable, NamedTuple

import jax
import jax.numpy as jnp
from jax import lax
from jax.experimental import pallas as pl
from jax.experimental.pallas import tpu as pltpu

D_MODEL = 2048
SEQ = 8192
DEPTH = 4
HEAD_DIM = 128
MOBA_HEADS = 6
DIFF_HEADS = 6
SGU_GROUPS = 4
MOBA_WIDTH = MOBA_HEADS * HEAD_DIM
DIFF_WIDTH = DIFF_HEADS * HEAD_DIM
SGU_WIDTH = SGU_GROUPS * HEAD_DIM
DIFF_QK_DIM = HEAD_DIM // 2
MOBA_BLOCK = 256
MOBA_TOPK = 3
SGU_CHUNK = 128
D_FF = 4 * D_MODEL
ROPE_THETA = 10000.0
EPS = 1e-6
QKV_WIDTH = 3 * MOBA_WIDTH + 3 * DIFF_WIDTH
N_BLOCKS = SEQ // MOBA_BLOCK

VMEM_LIMIT_BYTES = 56 * 1024 * 1024

PROJ_TM = 512
PROJ_TN = 768
ATTN_TQ = 2 * MOBA_BLOCK
ATTN_TK = 2 * MOBA_BLOCK
MOBA_HEADS_PER_STEP = 3
DIFF_HEADS_PER_STEP = 2
ATTN_TILES_PER_STEP = 2
CAST_CHUNKS = 16
OUT_TM = 512
MLP_TM = 512
MLP_TF = 2048
MLP_VMEM_LIMIT_BYTES = 58 * 1024 * 1024

NEG_INF = float("-inf")
NEG_BIG = -1e30
LOG2E = 1.4426950408889634


def _cparams(sem, vmem_limit_bytes=VMEM_LIMIT_BYTES):
    return pltpu.CompilerParams(dimension_semantics=sem, vmem_limit_bytes=vmem_limit_bytes)


def _rms_norm_rows(x, g):
    return x * lax.rsqrt(jnp.mean(x * x, axis=-1, keepdims=True) + EPS) * g


def _gelu_tanh(x):
    c = math.sqrt(2.0 / math.pi)
    return 0.5 * x * (1.0 + jnp.tanh(c * (x + 0.044715 * (x * x * x))))


def _in_proj_kernel(x_ref, g_ref, w_ref, cm_ref, sm_ref, cd_ref, sdl_ref, sdh_ref,
                    lng_ref, lnb_ref, ws_ref, bs_ref, o_ref, kmean_ref, sgu_ref):
    h = _rms_norm_rows(x_ref[...], g_ref[...]).astype(jnp.bfloat16)

    def rope_moba(xh):
        return xh * cm_ref[...] + pltpu.roll(xh, HEAD_DIM // 2, 1) * sm_ref[...]

    def rope_diff(xh):
        return (xh * cd_ref[...]
                + pltpu.roll(xh, HEAD_DIM - DIFF_QK_DIM // 2, 1) * sdl_ref[...]
                + pltpu.roll(xh, DIFF_QK_DIM // 2, 1) * sdh_ref[...])

    def group_dot(grp):
        return jnp.dot(h, w_ref[:, grp * PROJ_TN:(grp + 1) * PROJ_TN], preferred_element_type=jnp.float32)

    def group_epilogue(grp, acc, fn):
        for hd in range(PROJ_TN // HEAD_DIM):
            sl = slice(hd * HEAD_DIM, (hd + 1) * HEAD_DIM)
            r = acc[:, sl] if fn is None else fn(acc[:, sl])
            o_ref[:, grp * PROJ_TN + hd * HEAD_DIM:grp * PROJ_TN + (hd + 1) * HEAD_DIM] = r.astype(o_ref.dtype)
            if grp == 1:
                for b in range(PROJ_TM // MOBA_BLOCK):
                    blk = r[b * MOBA_BLOCK:(b + 1) * MOBA_BLOCK]
                    kmean_ref[0, b:b + 1, sl] = jnp.mean(blk, axis=0, keepdims=True)

    us = jnp.dot(h, w_ref[:, QKV_WIDTH:QKV_WIDTH + SGU_WIDTH], preferred_element_type=jnp.float32)
    vs = jnp.dot(h, w_ref[:, QKV_WIDTH + SGU_WIDTH:], preferred_element_type=jnp.float32)
    acc_q = group_dot(0)

    row = lax.broadcasted_iota(jnp.int32, (SGU_CHUNK, SGU_CHUNK), 0)
    col = lax.broadcasted_iota(jnp.int32, (SGU_CHUNK, SGU_CHUNK), 1)
    for g in range(SGU_GROUPS):
        sl = slice(g * HEAD_DIM, (g + 1) * HEAD_DIM)
        u = _gelu_tanh(us[:, sl])
        v = _gelu_tanh(vs[:, sl])
        mu = jnp.mean(v, axis=-1, keepdims=True)
        vc = v - mu
        var = jnp.mean(vc * vc, axis=-1, keepdims=True)
        vn = (vc * lax.rsqrt(var + EPS) * lng_ref[g:g + 1, :] + lnb_ref[g:g + 1, :]).astype(jnp.bfloat16)
        w = jnp.where(row >= col, ws_ref[g], 0.0).astype(jnp.bfloat16)
        bias = bs_ref[g]
        for c in range(PROJ_TM // SGU_CHUNK):
            rs = slice(c * SGU_CHUNK, (c + 1) * SGU_CHUNK)
            mixed = jnp.dot(w, vn[rs], preferred_element_type=jnp.float32) + bias
            sgu_ref[rs, sl] = (u[rs] * mixed).astype(sgu_ref.dtype)

    group_epilogue(0, acc_q, rope_moba)
    for grp, fn in ((1, rope_moba), (3, rope_diff), (4, rope_diff), (2, None), (5, None)):
        group_epilogue(grp, group_dot(grp), fn)


def _in_proj(x, g, w_in, cm, sm, cd, sdl, sdh, ln_g, ln_b, w_s, b_s):
    tm = PROJ_TM
    row_tab = pl.BlockSpec((tm, HEAD_DIM), lambda i: (i, 0))
    full = lambda shape: pl.BlockSpec(shape, lambda i: (0,) * len(shape))
    return pl.pallas_call(
        _in_proj_kernel,
        grid=(SEQ // tm,),
        in_specs=[
            pl.BlockSpec((tm, D_MODEL), lambda i: (i, 0)),
            pl.BlockSpec((1, D_MODEL), lambda i: (0, 0)),
            pl.BlockSpec((D_MODEL, QKV_WIDTH + 2 * SGU_WIDTH), lambda i: (0, 0), pipeline_mode=pl.Buffered(1)),
            row_tab, row_tab, row_tab, row_tab, row_tab,
            full((SGU_GROUPS, HEAD_DIM)),
            full((SGU_GROUPS, HEAD_DIM)),
            full((SGU_GROUPS, SGU_CHUNK, SGU_CHUNK)),
            full((SGU_GROUPS, SGU_CHUNK, 1)),
        ],
        out_specs=[
            pl.BlockSpec((tm, QKV_WIDTH), lambda i: (i, 0)),
            pl.BlockSpec((1, tm // MOBA_BLOCK, MOBA_WIDTH), lambda i: (i, 0, 0)),
            pl.BlockSpec((tm, SGU_WIDTH), lambda i: (i, 0)),
        ],
        out_shape=[
            jax.ShapeDtypeStruct((SEQ, QKV_WIDTH), jnp.bfloat16),
            jax.ShapeDtypeStruct((SEQ // tm, tm // MOBA_BLOCK, MOBA_WIDTH), jnp.float32),
            jax.ShapeDtypeStruct((SEQ, SGU_WIDTH), jnp.bfloat16),
        ],
        compiler_params=_cparams(("arbitrary",)),
        name="in_proj",
    )(x, g, w_in, cm, sm, cd, sdl, sdh, ln_g, ln_b, w_s, b_s)


class _Stream(NamedTuple):
    qk_own: Callable
    qk_past: Callable
    v_tile: Callable
    own_mask: Callable
    sa_ref: object
    sb_ref: object
    m_ref: object
    acc_ref: object
    c: float


def _scores(q, k):
    return lax.dot_general(q, k, (((1,), (1,)), ((), ())), preferred_element_type=jnp.float32)


def _with_ones(v):
    return jnp.concatenate([v, jnp.ones(v.shape, v.dtype)], axis=1)


def _lane_tile(x, n):
    return jnp.concatenate([x] * n, axis=1)


def _softmax_pv(st, s_ref, j, mask_fn=None):
    s = s_ref[...]
    n_rep = s.shape[1] // HEAD_DIM
    if mask_fn is not None:
        s = mask_fn(s)
        m_new = jnp.max(s, axis=-1, keepdims=True)
        p = jnp.exp2((s - m_new) * st.c).astype(jnp.bfloat16)
        st.acc_ref[...] = jnp.dot(p, st.v_tile(j), preferred_element_type=jnp.float32)
        st.m_ref[...] = jnp.broadcast_to(m_new, st.m_ref.shape)
        return
    m_prev = st.m_ref[...]
    m_new = jnp.maximum(m_prev, jnp.max(s, axis=-1, keepdims=True))
    alpha = jnp.exp2((m_prev - m_new) * st.c)
    p = jnp.exp2((s - _lane_tile(m_new, n_rep)) * st.c).astype(jnp.bfloat16)
    st.acc_ref[...] = (_lane_tile(alpha, 2) * st.acc_ref[...]
                       + jnp.dot(p, st.v_tile(j), preferred_element_type=jnp.float32))
    st.m_ref[...] = m_new


def _flash_pipeline(n_past, streams, odd):
    for st in streams:
        st.qk_own(st.sa_ref)
    for st in streams:
        st.qk_past(0, st.sb_ref)
        _softmax_pv(st, st.sa_ref, n_past, st.own_mask)

    def pair(jj, carry):
        j = 2 * jj
        for st in streams:
            st.qk_past(j + 1, st.sa_ref)
            _softmax_pv(st, st.sb_ref, j)
        for st in streams:
            st.qk_past(j + 2, st.sb_ref)
            _softmax_pv(st, st.sa_ref, j + 1)
        return carry

    lax.fori_loop(0, n_past // 2, pair, 0)

    if odd:
        for st in streams:
            _softmax_pv(st, st.sb_ref, n_past - 1)


def _cast_specs(w, layer):
    rows, cols = w.shape[1], w.shape[2]
    chunk = rows // CAST_CHUNKS
    assert chunk * CAST_CHUNKS == rows and chunk % 16 == 0
    steps_per_group = SEQ // (ATTN_TILES_PER_STEP * ATTN_TQ)

    def chunk_index(h, i):
        return jnp.minimum(h * steps_per_group + i, CAST_CHUNKS - 1)

    return (pl.BlockSpec((None, chunk, cols), lambda h, i: (layer, chunk_index(h, i), 0)),
            pl.BlockSpec((chunk, cols), lambda h, i: (chunk_index(h, i), 0)),
            jax.ShapeDtypeStruct((rows, cols), jnp.bfloat16))


def _cast_chunks(src_refs, dst_refs):
    for src, dst in zip(src_refs, dst_refs, strict=True):
        dst[...] = src[...].astype(dst.dtype)


def _moba_stream(i, q, k_ref, v_ref, kmean, qa_ref, sa_ref, sb_ref, m_ref, acc_ref):
    tq, tk = ATTN_TQ, ATTN_TK
    k_hi = kmean.astype(jnp.bfloat16)
    rest = kmean - k_hi.astype(jnp.float32)
    k_mid = rest.astype(jnp.bfloat16)
    k_lo = (rest - k_mid.astype(jnp.float32)).astype(jnp.bfloat16)
    gate = _scores(k_hi, q) + _scores(k_mid, q) + _scores(k_lo, q)
    blk = lax.broadcasted_iota(jnp.int32, (N_BLOCKS, tq), 0)
    own = 2 * i + jnp.where(lax.broadcasted_iota(jnp.int32, (N_BLOCKS, tq), 1) >= MOBA_BLOCK, 1, 0)
    gate = jnp.where(blk < own, gate, NEG_INF)
    blk_f = blk.astype(jnp.float32)
    sel = jnp.zeros((N_BLOCKS, tq), jnp.float32)
    for _ in range(MOBA_TOPK):
        mx = jnp.max(gate, axis=0, keepdims=True)
        cand = jnp.where((gate == mx) & (gate > NEG_INF), blk_f, float(N_BLOCKS))
        pick = blk_f == jnp.min(cand, axis=0, keepdims=True)
        sel = jnp.where(pick, 1.0, sel)
        gate = jnp.where(pick, NEG_INF, gate)
    sel = jnp.concatenate([sel, jnp.zeros((HEAD_DIM - N_BLOCKS, tq), jnp.float32)], axis=0).T
    qa_ref[:, :HEAD_DIM] = q
    qa_ref[:, HEAD_DIM:] = jnp.where(sel > 0.0, 0.0, NEG_BIG).astype(jnp.bfloat16)
    lane = lax.broadcasted_iota(jnp.int32, (tq, HEAD_DIM), 1)
    picked_first = jnp.sum(jnp.where(lane == 2 * i, sel, 0.0), axis=-1, keepdims=True) > 0.0

    def own_mask(s):
        r = lax.broadcasted_iota(jnp.int32, s.shape, 0)
        col = lax.broadcasted_iota(jnp.int32, s.shape, 1)
        rr = lax.broadcasted_iota(jnp.int32, (tq, 1), 0)
        first_col = jnp.where(picked_first | (rr < MOBA_BLOCK), 0, MOBA_BLOCK)
        return jnp.where(col <= r, jnp.where(col >= first_col, s, NEG_INF), NEG_INF)

    def qk_own(s_ref):
        s_ref[...] = _scores(qa_ref[:, :HEAD_DIM], k_ref[pl.ds(pl.multiple_of(i * tk, tk), tk), :])

    def qk_past(j, s_ref):
        erow = lax.broadcasted_iota(jnp.int32, (tk, HEAD_DIM), 0)
        elane = lax.broadcasted_iota(jnp.int32, (tk, HEAD_DIM), 1)
        onehot = jnp.where(elane == 2 * j + jnp.where(erow >= MOBA_BLOCK, 1, 0), 1.0, 0.0).astype(jnp.bfloat16)
        ka = jnp.concatenate([k_ref[pl.ds(pl.multiple_of(j * tk, tk), tk), :], onehot], axis=1)
        s_ref[...] = _scores(qa_ref[...], ka)

    def v_tile(j):
        return _with_ones(v_ref[pl.ds(pl.multiple_of(j * tk, tk), tk), :])

    return _Stream(qk_own, qk_past, v_tile, own_mask, sa_ref, sb_ref, m_ref, acc_ref,
                   (HEAD_DIM ** -0.5) * LOG2E)


def _moba_kernel(n_cast, q_ref, k_ref, v_ref, kmean_ref, *refs):
    cast_src, (o_ref, *cast_dst) = refs[:n_cast], refs[n_cast:2 * n_cast + 1]
    qa_ref, sa_ref, sb_ref, m_ref, acc_ref = refs[2 * n_cast + 1:]
    _cast_chunks(cast_src, cast_dst)
    for t in range(ATTN_TILES_PER_STEP):
        i = ATTN_TILES_PER_STEP * pl.program_id(1) + t
        rows = slice(t * ATTN_TQ, (t + 1) * ATTN_TQ)
        streams = []
        for h in range(MOBA_HEADS_PER_STEP):
            sl = slice(h * HEAD_DIM, (h + 1) * HEAD_DIM)
            streams.append(_moba_stream(i, q_ref[rows, sl], k_ref.at[:, sl], v_ref.at[:, sl], kmean_ref[:, sl],
                                        qa_ref.at[h], sa_ref.at[h], sb_ref.at[h], m_ref.at[h], acc_ref.at[h]))
        _flash_pipeline(i, streams, odd=t % 2 == 1)
        for h in range(MOBA_HEADS_PER_STEP):
            acc = acc_ref[h]
            o_ref[rows, h * HEAD_DIM:(h + 1) * HEAD_DIM] = (acc[:, :HEAD_DIM] / acc[:, HEAD_DIM:]).astype(o_ref.dtype)


def _moba_attention(qkv, kmean, casts):
    tq, tk, hp = ATTN_TQ, ATTN_TK, MOBA_HEADS_PER_STEP
    w = hp * HEAD_DIM
    groups = MOBA_HEADS // hp
    tiles = ATTN_TILES_PER_STEP
    assert groups * (SEQ // (tiles * tq)) >= CAST_CHUNKS
    cast_in, cast_out, cast_shape = zip(*[_cast_specs(cw, cl) for cw, cl in casts])
    return pl.pallas_call(
        functools.partial(_moba_kernel, len(casts)),
        grid=(groups, SEQ // (tiles * tq)),
        in_specs=[
            pl.BlockSpec((tiles * tq, w), lambda h, i: (i, h)),
            pl.BlockSpec((SEQ, w), lambda h, i: (0, groups + h)),
            pl.BlockSpec((SEQ, w), lambda h, i: (0, 2 * groups + h)),
            pl.BlockSpec((N_BLOCKS, w), lambda h, i: (0, h)),
            *cast_in,
        ],
        out_specs=[pl.BlockSpec((tiles * tq, w), lambda h, i: (i, h)), *cast_out],
        out_shape=[jax.ShapeDtypeStruct((SEQ, MOBA_WIDTH), jnp.bfloat16), *cast_shape],
        scratch_shapes=[
            pltpu.VMEM((hp, tq, 2 * HEAD_DIM), jnp.bfloat16),
            pltpu.VMEM((hp, tq, tk), jnp.float32),
            pltpu.VMEM((hp, tq, tk), jnp.float32),
            pltpu.VMEM((hp, tq, HEAD_DIM), jnp.float32),
            pltpu.VMEM((hp, tq, 2 * HEAD_DIM), jnp.float32),
        ],
        compiler_params=_cparams(("arbitrary", "arbitrary")),
        name="moba_attention",
    )(qkv, qkv, qkv, kmean, *[cw for cw, _ in casts])


def _diff_stream(i, q, k_ref, v_ref, q2_ref, sa_ref, sb_ref, m_ref, acc_ref):
    tq, tk = ATTN_TQ, ATTN_TK
    q = q * (DIFF_QK_DIM ** -0.5)
    lane = lax.broadcasted_iota(jnp.int32, (tq, HEAD_DIM), 1)
    zero = jnp.zeros_like(q)
    q2_ref[:tq, :] = jnp.where(lane < DIFF_QK_DIM, q, zero)
    q2_ref[tq:, :] = jnp.where(lane >= DIFF_QK_DIM, q, zero)

    def causal_mask(s):
        r = lax.broadcasted_iota(jnp.int32, s.shape, 0)
        r = jnp.where(r >= tq, r - tq, r)
        col = lax.broadcasted_iota(jnp.int32, s.shape, 1)
        return jnp.where(col <= r, s, NEG_INF)

    def qk_past(j, s_ref):
        s_ref[...] = _scores(q2_ref[...], k_ref[pl.ds(pl.multiple_of(j * tk, tk), tk), :])

    def v_tile(j):
        return _with_ones(v_ref[pl.ds(pl.multiple_of(j * tk, tk), tk), :])

    return _Stream(functools.partial(qk_past, i), qk_past, v_tile, causal_mask,
                   sa_ref, sb_ref, m_ref, acc_ref, LOG2E)


def _diff_kernel(lam_init, n_cast, q_ref, k_ref, v_ref, lp_ref, g_ref, *refs):
    cast_src, (o_ref, *cast_dst) = refs[:n_cast], refs[n_cast:2 * n_cast + 1]
    q2_ref, sa_ref, sb_ref, m_ref, acc_ref = refs[2 * n_cast + 1:]
    _cast_chunks(cast_src, cast_dst)
    tq = ATTN_TQ
    lp = lp_ref[...]
    lam = (jnp.exp(jnp.sum(lp[0:1] * lp[1:2], axis=-1, keepdims=True))
           - jnp.exp(jnp.sum(lp[2:3] * lp[3:4], axis=-1, keepdims=True)) + lam_init)
    for t in range(ATTN_TILES_PER_STEP):
        i = ATTN_TILES_PER_STEP * pl.program_id(1) + t
        rows = slice(t * tq, (t + 1) * tq)
        streams = []
        for h in range(DIFF_HEADS_PER_STEP):
            sl = slice(h * HEAD_DIM, (h + 1) * HEAD_DIM)
            streams.append(_diff_stream(i, q_ref[rows, sl], k_ref.at[:, sl], v_ref.at[:, sl],
                                        q2_ref.at[h], sa_ref.at[h], sb_ref.at[h], m_ref.at[h], acc_ref.at[h]))
        _flash_pipeline(i, streams, odd=t % 2 == 1)
        for h in range(DIFF_HEADS_PER_STEP):
            acc = acc_ref[h]
            o = acc[:, :HEAD_DIM] / acc[:, HEAD_DIM:]
            o = o[:tq] - lam * o[tq:]
            o = _rms_norm_rows(o, g_ref[...]) * (1.0 - lam_init)
            o_ref[rows, h * HEAD_DIM:(h + 1) * HEAD_DIM] = o.astype(o_ref.dtype)


def _diff_attention(qkv, lam_params, subln_g, lam_init, casts):
    tq, tk, hp = ATTN_TQ, ATTN_TK, DIFF_HEADS_PER_STEP
    w = hp * HEAD_DIM
    groups = DIFF_HEADS // hp
    base = 3 * MOBA_WIDTH // w
    tiles = ATTN_TILES_PER_STEP
    assert groups * (SEQ // (tiles * tq)) >= CAST_CHUNKS
    cast_in, cast_out, cast_shape = zip(*[_cast_specs(cw, cl) for cw, cl in casts])
    return pl.pallas_call(
        functools.partial(_diff_kernel, lam_init, len(casts)),
        grid=(groups, SEQ // (tiles * tq)),
        in_specs=[
            pl.BlockSpec((tiles * tq, w), lambda h, i: (i, base + h)),
            pl.BlockSpec((SEQ, w), lambda h, i: (0, base + groups + h)),
            pl.BlockSpec((SEQ, w), lambda h, i: (0, base + 2 * groups + h)),
            pl.BlockSpec((4, DIFF_QK_DIM), lambda h, i: (0, 0)),
            pl.BlockSpec((1, HEAD_DIM), lambda h, i: (0, 0)),
            *cast_in,
        ],
        out_specs=[pl.BlockSpec((tiles * tq, w), lambda h, i: (i, h)), *cast_out],
        out_shape=[jax.ShapeDtypeStruct((SEQ, DIFF_WIDTH), jnp.bfloat16), *cast_shape],
        scratch_shapes=[
            pltpu.VMEM((hp, 2 * tq, HEAD_DIM), jnp.bfloat16),
            pltpu.VMEM((hp, 2 * tq, tk), jnp.float32),
            pltpu.VMEM((hp, 2 * tq, tk), jnp.float32),
            pltpu.VMEM((hp, 2 * tq, HEAD_DIM), jnp.float32),
            pltpu.VMEM((hp, 2 * tq, 2 * HEAD_DIM), jnp.float32),
        ],
        compiler_params=_cparams(("arbitrary", "arbitrary")),
        name="diff_attention",
    )(qkv, qkv, qkv, lam_params, subln_g, *[cw for cw, _ in casts])


def _out_proj_kernel(x_ref, a_ref, b_ref, c_ref, w_ref, g_ref, o_ref, h_ref):
    half = OUT_TM // 2
    for r0 in range(0, OUT_TM, half):
        rows = slice(r0, r0 + half)
        acc = jnp.dot(a_ref[rows, :], w_ref[0:MOBA_WIDTH, :], preferred_element_type=jnp.float32)
        acc += jnp.dot(b_ref[rows, :], w_ref[MOBA_WIDTH:MOBA_WIDTH + DIFF_WIDTH, :],
                       preferred_element_type=jnp.float32)
        acc += jnp.dot(c_ref[rows, :], w_ref[MOBA_WIDTH + DIFF_WIDTH:, :], preferred_element_type=jnp.float32)
        y = x_ref[rows, :] + acc
        o_ref[rows, :] = y
        h_ref[rows, :] = _rms_norm_rows(y, g_ref[...]).astype(h_ref.dtype)


def _out_proj(x, moba_o, diff_o, sgu_o, w_out, g_mlp):
    tm = OUT_TM
    return pl.pallas_call(
        _out_proj_kernel,
        grid=(SEQ // tm,),
        in_specs=[
            pl.BlockSpec((tm, D_MODEL), lambda i: (i, 0)),
            pl.BlockSpec((tm, MOBA_WIDTH), lambda i: (i, 0)),
            pl.BlockSpec((tm, DIFF_WIDTH), lambda i: (i, 0)),
            pl.BlockSpec((tm, SGU_WIDTH), lambda i: (i, 0)),
            pl.BlockSpec((D_MODEL, D_MODEL), lambda i: (0, 0)),
            pl.BlockSpec((1, D_MODEL), lambda i: (0, 0)),
        ],
        out_specs=[
            pl.BlockSpec((tm, D_MODEL), lambda i: (i, 0)),
            pl.BlockSpec((tm, D_MODEL), lambda i: (i, 0)),
        ],
        out_shape=[
            jax.ShapeDtypeStruct((SEQ, D_MODEL), jnp.float32),
            jax.ShapeDtypeStruct((SEQ, D_MODEL), jnp.bfloat16),
        ],
        compiler_params=_cparams(("arbitrary",)),
        name="out_proj",
    )(x, moba_o, diff_o, sgu_o, w_out, g_mlp)


def _mlp_kernel(final, x_ref, h_ref, w1_ref, w2_ref, gf_ref, o_ref):
    j = pl.program_id(1)

    @pl.when(j == 0)
    def _():
        o_ref[...] = x_ref[...]

    a = jnp.dot(h_ref[...], w1_ref[...], preferred_element_type=jnp.float32)
    a = jnp.square(jnp.maximum(a, 0.0)).astype(jnp.bfloat16)
    o_ref[...] += jnp.dot(a, w2_ref[...], preferred_element_type=jnp.float32)

    if final:
        @pl.when(j == pl.num_programs(1) - 1)
        def _():
            o_ref[...] = _rms_norm_rows(o_ref[...], gf_ref[...])


def _mlp(x, h, w1, w2, g_final, final):
    tm, tf = MLP_TM, MLP_TF
    return pl.pallas_call(
        functools.partial(_mlp_kernel, final),
        grid=(SEQ // tm, D_FF // tf),
        in_specs=[
            pl.BlockSpec((tm, D_MODEL), lambda i, j: (i, 0)),
            pl.BlockSpec((tm, D_MODEL), lambda i, j: (i, 0)),
            pl.BlockSpec((D_MODEL, tf), lambda i, j: (0, j)),
            pl.BlockSpec((tf, D_MODEL), lambda i, j: (j, 0)),
            pl.BlockSpec((1, D_MODEL), lambda i, j: (0, 0)),
        ],
        out_specs=pl.BlockSpec((tm, D_MODEL), lambda i, j: (i, 0)),
        out_shape=jax.ShapeDtypeStruct((SEQ, D_MODEL), jnp.float32),
        compiler_params=_cparams(("arbitrary", "arbitrary"), MLP_VMEM_LIMIT_BYTES),
        name="mlp",
    )(x, h, w1, w2, g_final)


def _rope_tables():
    pos = jnp.arange(SEQ, dtype=jnp.float32)[:, None]
    lane = jnp.arange(HEAD_DIM)

    def angles(dim):
        inv = 1.0 / (ROPE_THETA ** (jnp.arange(0, dim, 2, dtype=jnp.float32) / dim))
        return pos * jnp.tile(inv, 2 * HEAD_DIM // dim)[None, :]

    ang_m, ang_d = angles(HEAD_DIM), angles(DIFF_QK_DIM)
    sin_m, sin_d = jnp.sin(ang_m), jnp.sin(ang_d)
    first_m = (lane % HEAD_DIM < HEAD_DIM // 2)[None, :]
    first_d = (lane % DIFF_QK_DIM < DIFF_QK_DIM // 2)[None, :]
    return (jnp.cos(ang_m), jnp.where(first_m, -sin_m, sin_m),
            jnp.cos(ang_d),
            jnp.where(first_d, -sin_d, 0.0),
            jnp.where(first_d, 0.0, sin_d))


def kernel(x, attn_norm_g, w_in, diff_lambda, diff_subln_g, sgu_ln_g, sgu_ln_b, sgu_w, sgu_b,
           w_out, mlp_norm_g, w_mlp_in, w_mlp_out, final_norm_g):
    assert x.shape == (1, SEQ, D_MODEL)
    bf = jnp.bfloat16
    xs = x.reshape(SEQ, D_MODEL)
    cm, sm, cd, sdl, sdh = _rope_tables()
    gf = final_norm_g.reshape(1, D_MODEL)
    w_in_l = w_in[0].astype(bf)
    for l in range(DEPTH):
        lam_init = 0.8 - 0.6 * math.exp(-0.3 * l)
        g_attn = attn_norm_g[l].reshape(1, D_MODEL)
        qkv, kmean, sgu_o = _in_proj(xs, g_attn, w_in_l, cm, sm, cd, sdl, sdh, sgu_ln_g[l], sgu_ln_b[l],
                                     sgu_w[l], sgu_b[l].reshape(SGU_GROUPS, SGU_CHUNK, 1))
        kmean = kmean.reshape(N_BLOCKS, MOBA_WIDTH)
        moba_o, w1_l, w_out_l = _moba_attention(qkv, kmean, [(w_mlp_in, l), (w_out, l)])
        diff_casts = [(w_mlp_out, l)] + ([(w_in, l + 1)] if l + 1 < DEPTH else [])
        diff_o, w2_l, *w_in_next = _diff_attention(qkv, diff_lambda[l], diff_subln_g[l].reshape(1, HEAD_DIM),
                                                   lam_init, diff_casts)
        xs, h_mlp = _out_proj(xs, moba_o, diff_o, sgu_o, w_out_l, mlp_norm_g[l].reshape(1, D_MODEL))
        xs = _mlp(xs, h_mlp, w1_l, w2_l, gf, l == DEPTH - 1)
        if w_in_next:
            w_in_l = w_in_next[0]
    return xs.reshape(1, SEQ, D_MODEL)
```

```python
import functools
import math
from typing import Callable, NamedTuple

import jax
import jax.numpy as jnp
from jax import lax
from jax.experimental import pallas as pl
from jax.experimental.pallas import tpu as pltpu

D_MODEL = 2048
SEQ = 8192
DEPTH = 4
HEAD_DIM = 128
MOBA_HEADS = 6
DIFF_HEADS = 6
SGU_GROUPS = 4
MOBA_WIDTH = MOBA_HEADS * HEAD_DIM
DIFF_WIDTH = DIFF_HEADS * HEAD_DIM
SGU_WIDTH = SGU_GROUPS * HEAD_DIM
DIFF_QK_DIM = HEAD_DIM // 2
MOBA_BLOCK = 256
MOBA_TOPK = 3
SGU_CHUNK = 128
D_FF = 4 * D_MODEL
ROPE_THETA = 10000.0
EPS = 1e-6
QKV_WIDTH = 3 * MOBA_WIDTH + 3 * DIFF_WIDTH
N_BLOCKS = SEQ // MOBA_BLOCK

VMEM_LIMIT_BYTES = 56 * 1024 * 1024

PROJ_TM = 512
PROJ_TN = 768
ATTN_TQ = 2 * MOBA_BLOCK
ATTN_TK = 2 * MOBA_BLOCK
MOBA_HEADS_PER_STEP = 3
DIFF_HEADS_PER_STEP = 2
ATTN_TILES_PER_STEP = 2
CAST_CHUNKS = 16
OUT_TM = 512
MLP_TM = 512
MLP_TF = 2048
MLP_VMEM_LIMIT_BYTES = 58 * 1024 * 1024

NEG_INF = float("-inf")
NEG_BIG = -1e30
LOG2E = 1.4426950408889634


def _cparams(sem, vmem_limit_bytes=VMEM_LIMIT_BYTES):
    return pltpu.CompilerParams(dimension_semantics=sem, vmem_limit_bytes=vmem_limit_bytes)


def _rms_norm_rows(x, g):
    return x * lax.rsqrt(jnp.mean(x * x, axis=-1, keepdims=True) + EPS) * g


def _gelu_tanh(x):
    c = math.sqrt(2.0 / math.pi)
    return 0.5 * x * (1.0 + jnp.tanh(c * (x + 0.044715 * (x * x * x))))


def _in_proj_kernel(x_ref, g_ref, w_ref, cm_ref, sm_ref, cd_ref, sdl_ref, sdh_ref,
                    lng_ref, lnb_ref, ws_ref, bs_ref, o_ref, kmean_ref, sgu_ref):
    h = _rms_norm_rows(x_ref[...], g_ref[...]).astype(jnp.bfloat16)

    def rope_moba(xh):
        return xh * cm_ref[...] + pltpu.roll(xh, HEAD_DIM // 2, 1) * sm_ref[...]

    def rope_diff(xh):
        return (xh * cd_ref[...]
                + pltpu.roll(xh, HEAD_DIM - DIFF_QK_DIM // 2, 1) * sdl_ref[...]
                + pltpu.roll(xh, DIFF_QK_DIM // 2, 1) * sdh_ref[...])

    def group_dot(grp):
        return jnp.dot(h, w_ref[:, grp * PROJ_TN:(grp + 1) * PROJ_TN], preferred_element_type=jnp.float32)

    def group_epilogue(grp, acc, fn):
        for hd in range(PROJ_TN // HEAD_DIM):
            sl = slice(hd * HEAD_DIM, (hd + 1) * HEAD_DIM)
            r = acc[:, sl] if fn is None else fn(acc[:, sl])
            o_ref[:, grp * PROJ_TN + hd * HEAD_DIM:grp * PROJ_TN + (hd + 1) * HEAD_DIM] = r.astype(o_ref.dtype)
            if grp == 1:
                for b in range(PROJ_TM // MOBA_BLOCK):
                    blk = r[b * MOBA_BLOCK:(b + 1) * MOBA_BLOCK]
                    kmean_ref[0, b:b + 1, sl] = jnp.mean(blk, axis=0, keepdims=True)

    us = jnp.dot(h, w_ref[:, QKV_WIDTH:QKV_WIDTH + SGU_WIDTH], preferred_element_type=jnp.float32)
    vs = jnp.dot(h, w_ref[:, QKV_WIDTH + SGU_WIDTH:], preferred_element_type=jnp.float32)
    acc_q = group_dot(0)

    row = lax.broadcasted_iota(jnp.int32, (SGU_CHUNK, SGU_CHUNK), 0)
    col = lax.broadcasted_iota(jnp.int32, (SGU_CHUNK, SGU_CHUNK), 1)
    for g in range(SGU_GROUPS):
        sl = slice(g * HEAD_DIM, (g + 1) * HEAD_DIM)
        u = _gelu_tanh(us[:, sl])
        v = _gelu_tanh(vs[:, sl])
        mu = jnp.mean(v, axis=-1, keepdims=True)
        vc = v - mu
        var = jnp.mean(vc * vc, axis=-1, keepdims=True)
        vn = (vc * lax.rsqrt(var + EPS) * lng_ref[g:g + 1, :] + lnb_ref[g:g + 1, :]).astype(jnp.bfloat16)
        w = jnp.where(row >= col, ws_ref[g], 0.0).astype(jnp.bfloat16)
        bias = bs_ref[g]
        for c in range(PROJ_TM // SGU_CHUNK):
            rs = slice(c * SGU_CHUNK, (c + 1) * SGU_CHUNK)
            mixed = jnp.dot(w, vn[rs], preferred_element_type=jnp.float32) + bias
            sgu_ref[rs, sl] = (u[rs] * mixed).astype(sgu_ref.dtype)

    group_epilogue(0, acc_q, rope_moba)
    for grp, fn in ((1, rope_moba), (3, rope_diff), (4, rope_diff), (2, None), (5, None)):
        group_epilogue(grp, group_dot(grp), fn)


def _in_proj(x, g, w_in, cm, sm, cd, sdl, sdh, ln_g, ln_b, w_s, b_s):
    tm = PROJ_TM
    row_tab = pl.BlockSpec((tm, HEAD_DIM), lambda i: (i, 0))
    full = lambda shape: pl.BlockSpec(shape, lambda i: (0,) * len(shape))
    return pl.pallas_call(
        _in_proj_kernel,
        grid=(SEQ // tm,),
        in_specs=[
            pl.BlockSpec((tm, D_MODEL), lambda i: (i, 0)),
            pl.BlockSpec((1, D_MODEL), lambda i: (0, 0)),
            pl.BlockSpec((D_MODEL, QKV_WIDTH + 2 * SGU_WIDTH), lambda i: (0, 0), pipeline_mode=pl.Buffered(1)),
            row_tab, row_tab, row_tab, row_tab, row_tab,
            full((SGU_GROUPS, HEAD_DIM)),
            full((SGU_GROUPS, HEAD_DIM)),
            full((SGU_GROUPS, SGU_CHUNK, SGU_CHUNK)),
            full((SGU_GROUPS, SGU_CHUNK, 1)),
        ],
        out_specs=[
            pl.BlockSpec((tm, QKV_WIDTH), lambda i: (i, 0)),
            pl.BlockSpec((1, tm // MOBA_BLOCK, MOBA_WIDTH), lambda i: (i, 0, 0)),
            pl.BlockSpec((tm, SGU_WIDTH), lambda i: (i, 0)),
        ],
        out_shape=[
            jax.ShapeDtypeStruct((SEQ, QKV_WIDTH), jnp.bfloat16),
            jax.ShapeDtypeStruct((SEQ // tm, tm // MOBA_BLOCK, MOBA_WIDTH), jnp.float32),
            jax.ShapeDtypeStruct((SEQ, SGU_WIDTH), jnp.bfloat16),
        ],
        compiler_params=_cparams(("arbitrary",)),
        name="in_proj",
    )(x, g, w_in, cm, sm, cd, sdl, sdh, ln_g, ln_b, w_s, b_s)


class _Stream(NamedTuple):
    qk_own: Callable
    qk_past: Callable
    v_tile: Callable
    own_mask: Callable
    sa_ref: object
    sb_ref: object
    m_ref: object
    acc_ref: object
    c: float


def _scores(q, k):
    return lax.dot_general(q, k, (((1,), (1,)), ((), ())), preferred_element_type=jnp.float32)


def _with_ones(v):
    return jnp.concatenate([v, jnp.ones(v.shape, v.dtype)], axis=1)


def _lane_tile(x, n):
    return jnp.concatenate([x] * n, axis=1)


def _softmax_pv(st, s_ref, j, mask_fn=None):
    s = s_ref[...]
    n_rep = s.shape[1] // HEAD_DIM
    if mask_fn is not None:
        s = mask_fn(s)
        m_new = jnp.max(s, axis=-1, keepdims=True)
        p = jnp.exp2((s - m_new) * st.c).astype(jnp.bfloat16)
        st.acc_ref[...] = jnp.dot(p, st.v_tile(j), preferred_element_type=jnp.float32)
        st.m_ref[...] = jnp.broadcast_to(m_new, st.m_ref.shape)
        return
    m_prev = st.m_ref[...]
    m_new = jnp.maximum(m_prev, jnp.max(s, axis=-1, keepdims=True))
    alpha = jnp.exp2((m_prev - m_new) * st.c)
    p = jnp.exp2((s - _lane_tile(m_new, n_rep)) * st.c).astype(jnp.bfloat16)
    st.acc_ref[...] = (_lane_tile(alpha, 2) * st.acc_ref[...]
                       + jnp.dot(p, st.v_tile(j), preferred_element_type=jnp.float32))
    st.m_ref[...] = m_new


def _flash_pipeline(n_past, streams, odd):
    for st in streams:
        st.qk_own(st.sa_ref)
    for st in streams:
        st.qk_past(0, st.sb_ref)
        _softmax_pv(st, st.sa_ref, n_past, st.own_mask)

    def pair(jj, carry):
        j = 2 * jj
        for st in streams:
            st.qk_past(j + 1, st.sa_ref)
            _softmax_pv(st, st.sb_ref, j)
        for st in streams:
            st.qk_past(j + 2, st.sb_ref)
            _softmax_pv(st, st.sa_ref, j + 1)
        return carry

    lax.fori_loop(0, n_past // 2, pair, 0)

    if odd:
        for st in streams:
            _softmax_pv(st, st.sb_ref, n_past - 1)


def _cast_specs(w, layer):
    rows, cols = w.shape[1], w.shape[2]
    chunk = rows // CAST_CHUNKS
    assert chunk * CAST_CHUNKS == rows and chunk % 16 == 0
    steps_per_group = SEQ // (ATTN_TILES_PER_STEP * ATTN_TQ)

    def chunk_index(h, i):
        return jnp.minimum(h * steps_per_group + i, CAST_CHUNKS - 1)

    return (pl.BlockSpec((None, chunk, cols), lambda h, i: (layer, chunk_index(h, i), 0)),
            pl.BlockSpec((chunk, cols), lambda h, i: (chunk_index(h, i), 0)),
            jax.ShapeDtypeStruct((rows, cols), jnp.bfloat16))


def _cast_chunks(src_refs, dst_refs):
    for src, dst in zip(src_refs, dst_refs, strict=True):
        dst[...] = src[...].astype(dst.dtype)


def _moba_stream(i, q, k_ref, v_ref, kmean, qa_ref, sa_ref, sb_ref, m_ref, acc_ref):
    tq, tk = ATTN_TQ, ATTN_TK
    k_hi = kmean.astype(jnp.bfloat16)
    rest = kmean - k_hi.astype(jnp.float32)
    k_mid = rest.astype(jnp.bfloat16)
    k_lo = (rest - k_mid.astype(jnp.float32)).astype(jnp.bfloat16)
    gate = _scores(k_hi, q) + _scores(k_mid, q) + _scores(k_lo, q)
    blk = lax.broadcasted_iota(jnp.int32, (N_BLOCKS, tq), 0)
    own = 2 * i + jnp.where(lax.broadcasted_iota(jnp.int32, (N_BLOCKS, tq), 1) >= MOBA_BLOCK, 1, 0)
    gate = jnp.where(blk < own, gate, NEG_INF)
    blk_f = blk.astype(jnp.float32)
    sel = jnp.zeros((N_BLOCKS, tq), jnp.float32)
    for _ in range(MOBA_TOPK):
        mx = jnp.max(gate, axis=0, keepdims=True)
        cand = jnp.where((gate == mx) & (gate > NEG_INF), blk_f, float(N_BLOCKS))
        pick = blk_f == jnp.min(cand, axis=0, keepdims=True)
        sel = jnp.where(pick, 1.0, sel)
        gate = jnp.where(pick, NEG_INF, gate)
    sel = jnp.concatenate([sel, jnp.zeros((HEAD_DIM - N_BLOCKS, tq), jnp.float32)], axis=0).T
    qa_ref[:, :HEAD_DIM] = q
    qa_ref[:, HEAD_DIM:] = jnp.where(sel > 0.0, 0.0, NEG_BIG).astype(jnp.bfloat16)
    lane = lax.broadcasted_iota(jnp.int32, (tq, HEAD_DIM), 1)
    picked_first = jnp.sum(jnp.where(lane == 2 * i, sel, 0.0), axis=-1, keepdims=True) > 0.0

    def own_mask(s):
        r = lax.broadcasted_iota(jnp.int32, s.shape, 0)
        col = lax.broadcasted_iota(jnp.int32, s.shape, 1)
        rr = lax.broadcasted_iota(jnp.int32, (tq, 1), 0)
        first_col = jnp.where(picked_first | (rr < MOBA_BLOCK), 0, MOBA_BLOCK)
        return jnp.where(col <= r, jnp.where(col >= first_col, s, NEG_INF), NEG_INF)

    def qk_own(s_ref):
        s_ref[...] = _scores(qa_ref[:, :HEAD_DIM], k_ref[pl.ds(pl.multiple_of(i * tk, tk), tk), :])

    def qk_past(j, s_ref):
        erow = lax.broadcasted_iota(jnp.int32, (tk, HEAD_DIM), 0)
        elane = lax.broadcasted_iota(jnp.int32, (tk, HEAD_DIM), 1)
        onehot = jnp.where(elane == 2 * j + jnp.where(erow >= MOBA_BLOCK, 1, 0), 1.0, 0.0).astype(jnp.bfloat16)
        ka = jnp.concatenate([k_ref[pl.ds(pl.multiple_of(j * tk, tk), tk), :], onehot], axis=1)
        s_ref[...] = _scores(qa_ref[...], ka)

    def v_tile(j):
        return _with_ones(v_ref[pl.ds(pl.multiple_of(j * tk, tk), tk), :])

    return _Stream(qk_own, qk_past, v_tile, own_mask, sa_ref, sb_ref, m_ref, acc_ref,
                   (HEAD_DIM ** -0.5) * LOG2E)


def _moba_kernel(n_cast, q_ref, k_ref, v_ref, kmean_ref, *refs):
    cast_src, (o_ref, *cast_dst) = refs[:n_cast], refs[n_cast:2 * n_cast + 1]
    qa_ref, sa_ref, sb_ref, m_ref, acc_ref = refs[2 * n_cast + 1:]
    _cast_chunks(cast_src, cast_dst)
    for t in range(ATTN_TILES_PER_STEP):
        i = ATTN_TILES_PER_STEP * pl.program_id(1) + t
        rows = slice(t * ATTN_TQ, (t + 1) * ATTN_TQ)
        streams = []
        for h in range(MOBA_HEADS_PER_STEP):
            sl = slice(h * HEAD_DIM, (h + 1) * HEAD_DIM)
            streams.append(_moba_stream(i, q_ref[rows, sl], k_ref.at[:, sl], v_ref.at[:, sl], kmean_ref[:, sl],
                                        qa_ref.at[h], sa_ref.at[h], sb_ref.at[h], m_ref.at[h], acc_ref.at[h]))
        _flash_pipeline(i, streams, odd=t % 2 == 1)
        for h in range(MOBA_HEADS_PER_STEP):
            acc = acc_ref[h]
            o_ref[rows, h * HEAD_DIM:(h + 1) * HEAD_DIM] = (acc[:, :HEAD_DIM] / acc[:, HEAD_DIM:]).astype(o_ref.dtype)


def _moba_attention(qkv, kmean, casts):
    tq, tk, hp = ATTN_TQ, ATTN_TK, MOBA_HEADS_PER_STEP
    w = hp * HEAD_DIM
    groups = MOBA_HEADS // hp
    tiles = ATTN_TILES_PER_STEP
    assert groups * (SEQ // (tiles * tq)) >= CAST_CHUNKS
    cast_in, cast_out, cast_shape = zip(*[_cast_specs(cw, cl) for cw, cl in casts])
    return pl.pallas_call(
        functools.partial(_moba_kernel, len(casts)),
        grid=(groups, SEQ // (tiles * tq)),
        in_specs=[
            pl.BlockSpec((tiles * tq, w), lambda h, i: (i, h)),
            pl.BlockSpec((SEQ, w), lambda h, i: (0, groups + h)),
            pl.BlockSpec((SEQ, w), lambda h, i: (0, 2 * groups + h)),
            pl.BlockSpec((N_BLOCKS, w), lambda h, i: (0, h)),
            *cast_in,
        ],
        out_specs=[pl.BlockSpec((tiles * tq, w), lambda h, i: (i, h)), *cast_out],
        out_shape=[jax.ShapeDtypeStruct((SEQ, MOBA_WIDTH), jnp.bfloat16), *cast_shape],
        scratch_shapes=[
            pltpu.VMEM((hp, tq, 2 * HEAD_DIM), jnp.bfloat16),
            pltpu.VMEM((hp, tq, tk), jnp.float32),
            pltpu.VMEM((hp, tq, tk), jnp.float32),
            pltpu.VMEM((hp, tq, HEAD_DIM), jnp.float32),
            pltpu.VMEM((hp, tq, 2 * HEAD_DIM), jnp.float32),
        ],
        compiler_params=_cparams(("arbitrary", "arbitrary")),
        name="moba_attention",
    )(qkv, qkv, qkv, kmean, *[cw for cw, _ in casts])


def _diff_stream(i, q, k_ref, v_ref, q2_ref, sa_ref, sb_ref, m_ref, acc_ref):
    tq, tk = ATTN_TQ, ATTN_TK
    q = q * (DIFF_QK_DIM ** -0.5)
    lane = lax.broadcasted_iota(jnp.int32, (tq, HEAD_DIM), 1)
    zero = jnp.zeros_like(q)
    q2_ref[:tq, :] = jnp.where(lane < DIFF_QK_DIM, q, zero)
    q2_ref[tq:, :] = jnp.where(lane >= DIFF_QK_DIM, q, zero)

    def causal_mask(s):
        r = lax.broadcasted_iota(jnp.int32, s.shape, 0)
        r = jnp.where(r >= tq, r - tq, r)
        col = lax.broadcasted_iota(jnp.int32, s.shape, 1)
        return jnp.where(col <= r, s, NEG_INF)

    def qk_past(j, s_ref):
        s_ref[...] = _scores(q2_ref[...], k_ref[pl.ds(pl.multiple_of(j * tk, tk), tk), :])

    def v_tile(j):
        return _with_ones(v_ref[pl.ds(pl.multiple_of(j * tk, tk), tk), :])

    return _Stream(functools.partial(qk_past, i), qk_past, v_tile, causal_mask,
                   sa_ref, sb_ref, m_ref, acc_ref, LOG2E)


def _diff_kernel(lam_init, n_cast, q_ref, k_ref, v_ref, lp_ref, g_ref, *refs):
    cast_src, (o_ref, *cast_dst) = refs[:n_cast], refs[n_cast:2 * n_cast + 1]
    q2_ref, sa_ref, sb_ref, m_ref, acc_ref = refs[2 * n_cast + 1:]
    _cast_chunks(cast_src, cast_dst)
    tq = ATTN_TQ
    lp = lp_ref[...]
    lam = (jnp.exp(jnp.sum(lp[0:1] * lp[1:2], axis=-1, keepdims=True))
           - jnp.exp(jnp.sum(lp[2:3] * lp[3:4], axis=-1, keepdims=True)) + lam_init)
    for t in range(ATTN_TILES_PER_STEP):
        i = ATTN_TILES_PER_STEP * pl.program_id(1) + t
        rows = slice(t * tq, (t + 1) * tq)
        streams = []
        for h in range(DIFF_HEADS_PER_STEP):
            sl = slice(h * HEAD_DIM, (h + 1) * HEAD_DIM)
            streams.append(_diff_stream(i, q_ref[rows, sl], k_ref.at[:, sl], v_ref.at[:, sl],
                                        q2_ref.at[h], sa_ref.at[h], sb_ref.at[h], m_ref.at[h], acc_ref.at[h]))
        _flash_pipeline(i, streams, odd=t % 2 == 1)
        for h in range(DIFF_HEADS_PER_STEP):
            acc = acc_ref[h]
            o = acc[:, :HEAD_DIM] / acc[:, HEAD_DIM:]
            o = o[:tq] - lam * o[tq:]
            o = _rms_norm_rows(o, g_ref[...]) * (1.0 - lam_init)
            o_ref[rows, h * HEAD_DIM:(h + 1) * HEAD_DIM] = o.astype(o_ref.dtype)


def _diff_attention(qkv, lam_params, subln_g, lam_init, casts):
    tq, tk, hp = ATTN_TQ, ATTN_TK, DIFF_HEADS_PER_STEP
    w = hp * HEAD_DIM
    groups = DIFF_HEADS // hp
    base = 3 * MOBA_WIDTH // w
    tiles = ATTN_TILES_PER_STEP
    assert groups * (SEQ // (tiles * tq)) >= CAST_CHUNKS
    cast_in, cast_out, cast_shape = zip(*[_cast_specs(cw, cl) for cw, cl in casts])
    return pl.pallas_call(
        functools.partial(_diff_kernel, lam_init, len(casts)),
        grid=(groups, SEQ // (tiles * tq)),
        in_specs=[
            pl.BlockSpec((tiles * tq, w), lambda h, i: (i, base + h)),
            pl.BlockSpec((SEQ, w), lambda h, i: (0, base + groups + h)),
            pl.BlockSpec((SEQ, w), lambda h, i: (0, base + 2 * groups + h)),
            pl.BlockSpec((4, DIFF_QK_DIM), lambda h, i: (0, 0)),
            pl.BlockSpec((1, HEAD_DIM), lambda h, i: (0, 0)),
            *cast_in,
        ],
        out_specs=[pl.BlockSpec((tiles * tq, w), lambda h, i: (i, h)), *cast_out],
        out_shape=[jax.ShapeDtypeStruct((SEQ, DIFF_WIDTH), jnp.bfloat16), *cast_shape],
        scratch_shapes=[
            pltpu.VMEM((hp, 2 * tq, HEAD_DIM), jnp.bfloat16),
            pltpu.VMEM((hp, 2 * tq, tk), jnp.float32),
            pltpu.VMEM((hp, 2 * tq, tk), jnp.float32),
            pltpu.VMEM((hp, 2 * tq, HEAD_DIM), jnp.float32),
            pltpu.VMEM((hp, 2 * tq, 2 * HEAD_DIM), jnp.float32),
        ],
        compiler_params=_cparams(("arbitrary", "arbitrary")),
        name="diff_attention",
    )(qkv, qkv, qkv, lam_params, subln_g, *[cw for cw, _ in casts])


def _out_proj_kernel(x_ref, a_ref, b_ref, c_ref, w_ref, g_ref, o_ref, h_ref):
    half = OUT_TM // 2
    for r0 in range(0, OUT_TM, half):
        rows = slice(r0, r0 + half)
        acc = jnp.dot(a_ref[rows, :], w_ref[0:MOBA_WIDTH, :], preferred_element_type=jnp.float32)
        acc += jnp.dot(b_ref[rows, :], w_ref[MOBA_WIDTH:MOBA_WIDTH + DIFF_WIDTH, :],
                       preferred_element_type=jnp.float32)
        acc += jnp.dot(c_ref[rows, :], w_ref[MOBA_WIDTH + DIFF_WIDTH:, :], preferred_element_type=jnp.float32)
        y = x_ref[rows, :] + acc
        o_ref[rows, :] = y
        h_ref[rows, :] = _rms_norm_rows(y, g_ref[...]).astype(h_ref.dtype)


def _out_proj(x, moba_o, diff_o, sgu_o, w_out, g_mlp):
    tm = OUT_TM
    return pl.pallas_call(
        _out_proj_kernel,
        grid=(SEQ // tm,),
        in_specs=[
            pl.BlockSpec((tm, D_MODEL), lambda i: (i, 0)),
            pl.BlockSpec((tm, MOBA_WIDTH), lambda i: (i, 0)),
            pl.BlockSpec((tm, DIFF_WIDTH), lambda i: (i, 0)),
            pl.BlockSpec((tm, SGU_WIDTH), lambda i: (i, 0)),
            pl.BlockSpec((D_MODEL, D_MODEL), lambda i: (0, 0)),
            pl.BlockSpec((1, D_MODEL), lambda i: (0, 0)),
        ],
        out_specs=[
            pl.BlockSpec((tm, D_MODEL), lambda i: (i, 0)),
            pl.BlockSpec((tm, D_MODEL), lambda i: (i, 0)),
        ],
        out_shape=[
            jax.ShapeDtypeStruct((SEQ, D_MODEL), jnp.float32),
            jax.ShapeDtypeStruct((SEQ, D_MODEL), jnp.bfloat16),
        ],
        compiler_params=_cparams(("arbitrary",)),
        name="out_proj",
    )(x, moba_o, diff_o, sgu_o, w_out, g_mlp)


def _mlp_kernel(final, x_ref, h_ref, w1_ref, w2_ref, gf_ref, o_ref):
    j = pl.program_id(1)

    def hidden_tile(base_ref):
        a = jnp.dot(h_ref[...], w1_ref[...], preferred_element_type=jnp.float32)
        a = jnp.square(jnp.maximum(a, 0.0)).astype(jnp.bfloat16)
        o_ref[...] = base_ref[...] + jnp.dot(a, w2_ref[...], preferred_element_type=jnp.float32)

    @pl.when(j == 0)
    def _():
        hidden_tile(x_ref)

    @pl.when(j > 0)
    def _():
        hidden_tile(o_ref)

    if final:
        @pl.when(j == pl.num_programs(1) - 1)
        def _():
            o_ref[...] = _rms_norm_rows(o_ref[...], gf_ref[...])


def _mlp(x, h, w1, w2, g_final, final):
    tm, tf = MLP_TM, MLP_TF
    return pl.pallas_call(
        functools.partial(_mlp_kernel, final),
        grid=(SEQ // tm, D_FF // tf),
        in_specs=[
            pl.BlockSpec((tm, D_MODEL), lambda i, j: (i, 0)),
            pl.BlockSpec((tm, D_MODEL), lambda i, j: (i, 0)),
            pl.BlockSpec((D_MODEL, tf), lambda i, j: (0, j)),
            pl.BlockSpec((tf, D_MODEL), lambda i, j: (j, 0)),
            pl.BlockSpec((1, D_MODEL), lambda i, j: (0, 0)),
        ],
        out_specs=pl.BlockSpec((tm, D_MODEL), lambda i, j: (i, 0)),
        out_shape=jax.ShapeDtypeStruct((SEQ, D_MODEL), jnp.float32),
        compiler_params=_cparams(("arbitrary", "arbitrary"), MLP_VMEM_LIMIT_BYTES),
        name="mlp",
    )(x, h, w1, w2, g_final)


def _rope_tables():
    pos = jnp.arange(SEQ, dtype=jnp.float32)[:, None]
    lane = jnp.arange(HEAD_DIM)

    def angles(dim):
        inv = 1.0 / (ROPE_THETA ** (jnp.arange(0, dim, 2, dtype=jnp.float32) / dim))
        return pos * jnp.tile(inv, 2 * HEAD_DIM // dim)[None, :]

    ang_m, ang_d = angles(HEAD_DIM), angles(DIFF_QK_DIM)
    sin_m, sin_d = jnp.sin(ang_m), jnp.sin(ang_d)
    first_m = (lane % HEAD_DIM < HEAD_DIM // 2)[None, :]
    first_d = (lane % DIFF_QK_DIM < DIFF_QK_DIM // 2)[None, :]
    return (jnp.cos(ang_m), jnp.where(first_m, -sin_m, sin_m),
            jnp.cos(ang_d),
            jnp.where(first_d, -sin_d, 0.0),
            jnp.where(first_d, 0.0, sin_d))


def kernel(x, attn_norm_g, w_in, diff_lambda, diff_subln_g, sgu_ln_g, sgu_ln_b, sgu_w, sgu_b,
           w_out, mlp_norm_g, w_mlp_in, w_mlp_out, final_norm_g):
    assert x.shape == (1, SEQ, D_MODEL)
    bf = jnp.bfloat16
    xs = x.reshape(SEQ, D_MODEL)
    cm, sm, cd, sdl, sdh = _rope_tables()
    gf = final_norm_g.reshape(1, D_MODEL)
    w_in_l = w_in[0].astype(bf)
    for l in range(DEPTH):
        lam_init = 0.8 - 0.6 * math.exp(-0.3 * l)
        g_attn = attn_norm_g[l].reshape(1, D_MODEL)
        qkv, kmean, sgu_o = _in_proj(xs, g_attn, w_in_l, cm, sm, cd, sdl, sdh, sgu_ln_g[l], sgu_ln_b[l],
                                     sgu_w[l], sgu_b[l].reshape(SGU_GROUPS, SGU_CHUNK, 1))
        kmean = kmean.reshape(N_BLOCKS, MOBA_WIDTH)
        moba_o, w1_l, w_out_l = _moba_attention(qkv, kmean, [(w_mlp_in, l), (w_out, l)])
        diff_casts = [(w_mlp_out, l)] + ([(w_in, l + 1)] if l + 1 < DEPTH else [])
        diff_o, w2_l, *w_in_next = _diff_attention(qkv, diff_lambda[l], diff_subln_g[l].reshape(1, HEAD_DIM),
                                                   lam_init, diff_casts)
        xs, h_mlp = _out_proj(xs, moba_o, diff_o, sgu_o, w_out_l, mlp_norm_g[l].reshape(1, D_MODEL))
        xs = _mlp(xs, h_mlp, w1_l, w2_l, gf, l == DEPTH - 1)
        if w_in_next:
            w_in_l = w_in_next[0]
    return xs.reshape(1, SEQ, D_MODEL)
```

```python
import functools
import math
from typing import Callable, NamedTuple

import jax
import jax.numpy as jnp
from jax import lax
from jax.experimental import pallas as pl
from jax.experimental.pallas import tpu as pltpu

D_MODEL = 2048
SEQ = 8192
DEPTH = 4
HEAD_DIM = 128
MOBA_HEADS = 6
DIFF_HEADS = 6
SGU_GROUPS = 4
MOBA_WIDTH = MOBA_HEADS * HEAD_DIM
DIFF_WIDTH = DIFF_HEADS * HEAD_DIM
SGU_WIDTH = SGU_GROUPS * HEAD_DIM
DIFF_QK_DIM = HEAD_DIM // 2
MOBA_BLOCK = 256
MOBA_TOPK = 3
SGU_CHUNK = 128
D_FF = 4 * D_MODEL
ROPE_THETA = 10000.0
EPS = 1e-6
QKV_WIDTH = 3 * MOBA_WIDTH + 3 * DIFF_WIDTH
N_BLOCKS = SEQ // MOBA_BLOCK

VMEM_LIMIT_BYTES = 56 * 1024 * 1024

PROJ_TM = 512
PROJ_TN = 768
ATTN_TQ = 2 * MOBA_BLOCK
ATTN_TK = 2 * MOBA_BLOCK
MOBA_HEADS_PER_STEP = 3
DIFF_HEADS_PER_STEP = 2
ATTN_TILES_PER_STEP = 2
CAST_CHUNKS = 16
OUT_TM = 512
MLP_TM = 512
MLP_TF = 2048
MLP_VMEM_LIMIT_BYTES = 58 * 1024 * 1024

NEG_INF = float("-inf")
NEG_BIG = -1e30
LOG2E = 1.4426950408889634


def _cparams(sem, vmem_limit_bytes=VMEM_LIMIT_BYTES):
    return pltpu.CompilerParams(dimension_semantics=sem, vmem_limit_bytes=vmem_limit_bytes)


def _rms_norm_rows(x, g):
    return x * lax.rsqrt(jnp.mean(x * x, axis=-1, keepdims=True) + EPS) * g


def _gelu_tanh(x):
    c = math.sqrt(2.0 / math.pi)
    return 0.5 * x * (1.0 + jnp.tanh(c * (x + 0.044715 * (x * x * x))))


def _in_proj_kernel(x_ref, g_ref, w_ref, cm_ref, sm_ref, cd_ref, sdl_ref, sdh_ref,
                    lng_ref, lnb_ref, ws_ref, bs_ref, o_ref, kmean_ref, sgu_ref):
    h = _rms_norm_rows(x_ref[...], g_ref[...]).astype(jnp.bfloat16)

    def rope_moba(xh):
        return xh * cm_ref[...] + pltpu.roll(xh, HEAD_DIM // 2, 1) * sm_ref[...]

    def rope_diff(xh):
        return (xh * cd_ref[...]
                + pltpu.roll(xh, HEAD_DIM - DIFF_QK_DIM // 2, 1) * sdl_ref[...]
                + pltpu.roll(xh, DIFF_QK_DIM // 2, 1) * sdh_ref[...])

    def group_dot(grp):
        return jnp.dot(h, w_ref[:, grp * PROJ_TN:(grp + 1) * PROJ_TN], preferred_element_type=jnp.float32)

    def group_epilogue(grp, acc, fn):
        for hd in range(PROJ_TN // HEAD_DIM):
            sl = slice(hd * HEAD_DIM, (hd + 1) * HEAD_DIM)
            r = acc[:, sl] if fn is None else fn(acc[:, sl])
            o_ref[:, grp * PROJ_TN + hd * HEAD_DIM:grp * PROJ_TN + (hd + 1) * HEAD_DIM] = r.astype(o_ref.dtype)
            if grp == 1:
                for b in range(PROJ_TM // MOBA_BLOCK):
                    blk = r[b * MOBA_BLOCK:(b + 1) * MOBA_BLOCK]
                    kmean_ref[0, b:b + 1, sl] = jnp.mean(blk, axis=0, keepdims=True)

    us = jnp.dot(h, w_ref[:, QKV_WIDTH:QKV_WIDTH + SGU_WIDTH], preferred_element_type=jnp.float32)
    vs = jnp.dot(h, w_ref[:, QKV_WIDTH + SGU_WIDTH:], preferred_element_type=jnp.float32)
    acc_q = group_dot(0)

    row = lax.broadcasted_iota(jnp.int32, (SGU_CHUNK, SGU_CHUNK), 0)
    col = lax.broadcasted_iota(jnp.int32, (SGU_CHUNK, SGU_CHUNK), 1)
    for g in range(SGU_GROUPS):
        sl = slice(g * HEAD_DIM, (g + 1) * HEAD_DIM)
        u = _gelu_tanh(us[:, sl])
        v = _gelu_tanh(vs[:, sl])
        mu = jnp.mean(v, axis=-1, keepdims=True)
        vc = v - mu
        var = jnp.mean(vc * vc, axis=-1, keepdims=True)
        vn = (vc * lax.rsqrt(var + EPS) * lng_ref[g:g + 1, :] + lnb_ref[g:g + 1, :]).astype(jnp.bfloat16)
        w = jnp.where(row >= col, ws_ref[g], 0.0).astype(jnp.bfloat16)
        bias = bs_ref[g]
        for c in range(PROJ_TM // SGU_CHUNK):
            rs = slice(c * SGU_CHUNK, (c + 1) * SGU_CHUNK)
            mixed = jnp.dot(w, vn[rs], preferred_element_type=jnp.float32) + bias
            sgu_ref[rs, sl] = (u[rs] * mixed).astype(sgu_ref.dtype)

    group_epilogue(0, acc_q, rope_moba)
    for grp, fn in ((1, rope_moba), (3, rope_diff), (4, rope_diff), (2, None), (5, None)):
        group_epilogue(grp, group_dot(grp), fn)


def _in_proj(x, g, w_in, cm, sm, cd, sdl, sdh, ln_g, ln_b, w_s, b_s):
    tm = PROJ_TM
    row_tab = pl.BlockSpec((tm, HEAD_DIM), lambda i: (i, 0))
    full = lambda shape: pl.BlockSpec(shape, lambda i: (0,) * len(shape))
    return pl.pallas_call(
        _in_proj_kernel,
        grid=(SEQ // tm,),
        in_specs=[
            pl.BlockSpec((tm, D_MODEL), lambda i: (i, 0)),
            pl.BlockSpec((1, D_MODEL), lambda i: (0, 0)),
            pl.BlockSpec((D_MODEL, QKV_WIDTH + 2 * SGU_WIDTH), lambda i: (0, 0), pipeline_mode=pl.Buffered(1)),
            row_tab, row_tab, row_tab, row_tab, row_tab,
            full((SGU_GROUPS, HEAD_DIM)),
            full((SGU_GROUPS, HEAD_DIM)),
            full((SGU_GROUPS, SGU_CHUNK, SGU_CHUNK)),
            full((SGU_GROUPS, SGU_CHUNK, 1)),
        ],
        out_specs=[
            pl.BlockSpec((tm, QKV_WIDTH), lambda i: (i, 0)),
            pl.BlockSpec((1, tm // MOBA_BLOCK, MOBA_WIDTH), lambda i: (i, 0, 0)),
            pl.BlockSpec((tm, SGU_WIDTH), lambda i: (i, 0)),
        ],
        out_shape=[
            jax.ShapeDtypeStruct((SEQ, QKV_WIDTH), jnp.bfloat16),
            jax.ShapeDtypeStruct((SEQ // tm, tm // MOBA_BLOCK, MOBA_WIDTH), jnp.float32),
            jax.ShapeDtypeStruct((SEQ, SGU_WIDTH), jnp.bfloat16),
        ],
        compiler_params=_cparams(("arbitrary",)),
        name="in_proj",
    )(x, g, w_in, cm, sm, cd, sdl, sdh, ln_g, ln_b, w_s, b_s)


class _Stream(NamedTuple):
    qk_own: Callable
    qk_past: Callable
    v_tile: Callable
    own_mask: Callable
    sa_ref: object
    sb_ref: object
    m_ref: object
    acc_ref: object
    c: float


def _scores(q, k):
    return lax.dot_general(q, k, (((1,), (1,)), ((), ())), preferred_element_type=jnp.float32)


def _with_ones(v):
    return jnp.concatenate([v, jnp.ones(v.shape, v.dtype)], axis=1)


def _lane_tile(x, n):
    return jnp.concatenate([x] * n, axis=1)


def _softmax_pv(st, s_ref, j, mask_fn=None):
    s = s_ref[...]
    n_rep = s.shape[1] // HEAD_DIM
    if mask_fn is not None:
        s = mask_fn(s)
        m_new = jnp.max(s, axis=-1, keepdims=True)
        p = jnp.exp2((s - m_new) * st.c).astype(jnp.bfloat16)
        st.acc_ref[...] = jnp.dot(p, st.v_tile(j), preferred_element_type=jnp.float32)
        st.m_ref[...] = jnp.broadcast_to(m_new, st.m_ref.shape)
        return
    m_prev = st.m_ref[...]
    m_new = jnp.maximum(m_prev, jnp.max(s, axis=-1, keepdims=True))
    alpha = jnp.exp2((m_prev - m_new) * st.c)
    p = jnp.exp2((s - _lane_tile(m_new, n_rep)) * st.c).astype(jnp.bfloat16)
    st.acc_ref[...] = (_lane_tile(alpha, 2) * st.acc_ref[...]
                       + jnp.dot(p, st.v_tile(j), preferred_element_type=jnp.float32))
    st.m_ref[...] = m_new


def _flash_pipeline(n_past, streams, odd):
    for st in streams:
        st.qk_own(st.sa_ref)
    for st in streams:
        st.qk_past(0, st.sb_ref)
        _softmax_pv(st, st.sa_ref, n_past, st.own_mask)

    def pair(jj, carry):
        j = 2 * jj
        for st in streams:
            st.qk_past(j + 1, st.sa_ref)
            _softmax_pv(st, st.sb_ref, j)
        for st in streams:
            st.qk_past(j + 2, st.sb_ref)
            _softmax_pv(st, st.sa_ref, j + 1)
        return carry

    lax.fori_loop(0, n_past // 2, pair, 0)

    if odd:
        for st in streams:
            _softmax_pv(st, st.sb_ref, n_past - 1)


def _cast_specs(w, layer):
    rows, cols = w.shape[1], w.shape[2]
    chunk = rows // CAST_CHUNKS
    assert chunk * CAST_CHUNKS == rows and chunk % 16 == 0
    steps_per_group = SEQ // (ATTN_TILES_PER_STEP * ATTN_TQ)

    def chunk_index(h, i):
        return jnp.minimum(h * steps_per_group + i, CAST_CHUNKS - 1)

    return (pl.BlockSpec((None, chunk, cols), lambda h, i: (layer, chunk_index(h, i), 0)),
            pl.BlockSpec((chunk, cols), lambda h, i: (chunk_index(h, i), 0)),
            jax.ShapeDtypeStruct((rows, cols), jnp.bfloat16))


def _cast_chunks(src_refs, dst_refs):
    for src, dst in zip(src_refs, dst_refs, strict=True):
        dst[...] = src[...].astype(dst.dtype)


def _moba_stream(i, q, k_ref, v_ref, kmean, qa_ref, sa_ref, sb_ref, m_ref, acc_ref):
    tq, tk = ATTN_TQ, ATTN_TK
    k_hi = kmean.astype(jnp.bfloat16)
    rest = kmean - k_hi.astype(jnp.float32)
    k_mid = rest.astype(jnp.bfloat16)
    k_lo = (rest - k_mid.astype(jnp.float32)).astype(jnp.bfloat16)
    gate = _scores(k_hi, q) + _scores(k_mid, q) + _scores(k_lo, q)
    blk = lax.broadcasted_iota(jnp.int32, (N_BLOCKS, tq), 0)
    own = 2 * i + jnp.where(lax.broadcasted_iota(jnp.int32, (N_BLOCKS, tq), 1) >= MOBA_BLOCK, 1, 0)
    gate = jnp.where(blk < own, gate, NEG_INF)
    blk_f = blk.astype(jnp.float32)
    sel = jnp.zeros((N_BLOCKS, tq), jnp.float32)
    for _ in range(MOBA_TOPK):
        mx = jnp.max(gate, axis=0, keepdims=True)
        cand = jnp.where((gate == mx) & (gate > NEG_INF), blk_f, float(N_BLOCKS))
        pick = blk_f == jnp.min(cand, axis=0, keepdims=True)
        sel = jnp.where(pick, 1.0, sel)
        gate = jnp.where(pick, NEG_INF, gate)
    sel = jnp.concatenate([sel, jnp.zeros((HEAD_DIM - N_BLOCKS, tq), jnp.float32)], axis=0).T
    qa_ref[:, :HEAD_DIM] = q
    qa_ref[:, HEAD_DIM:] = jnp.where(sel > 0.0, 0.0, NEG_BIG).astype(jnp.bfloat16)
    lane = lax.broadcasted_iota(jnp.int32, (tq, HEAD_DIM), 1)
    picked_first = jnp.sum(jnp.where(lane == 2 * i, sel, 0.0), axis=-1, keepdims=True) > 0.0

    def own_mask(s):
        r = lax.broadcasted_iota(jnp.int32, s.shape, 0)
        col = lax.broadcasted_iota(jnp.int32, s.shape, 1)
        rr = lax.broadcasted_iota(jnp.int32, (tq, 1), 0)
        first_col = jnp.where(picked_first | (rr < MOBA_BLOCK), 0, MOBA_BLOCK)
        return jnp.where(col <= r, jnp.where(col >= first_col, s, NEG_INF), NEG_INF)

    def qk_own(s_ref):
        s_ref[...] = _scores(qa_ref[:, :HEAD_DIM], k_ref[pl.ds(pl.multiple_of(i * tk, tk), tk), :])

    def qk_past(j, s_ref):
        erow = lax.broadcasted_iota(jnp.int32, (tk, HEAD_DIM), 0)
        elane = lax.broadcasted_iota(jnp.int32, (tk, HEAD_DIM), 1)
        onehot = jnp.where(elane == 2 * j + jnp.where(erow >= MOBA_BLOCK, 1, 0), 1.0, 0.0).astype(jnp.bfloat16)
        ka = jnp.concatenate([k_ref[pl.ds(pl.multiple_of(j * tk, tk), tk), :], onehot], axis=1)
        s_ref[...] = _scores(qa_ref[...], ka)

    def v_tile(j):
        return _with_ones(v_ref[pl.ds(pl.multiple_of(j * tk, tk), tk), :])

    return _Stream(qk_own, qk_past, v_tile, own_mask, sa_ref, sb_ref, m_ref, acc_ref,
                   (HEAD_DIM ** -0.5) * LOG2E)


def _moba_kernel(n_cast, q_ref, k_ref, v_ref, kmean_ref, *refs):
    cast_src, (o_ref, *cast_dst) = refs[:n_cast], refs[n_cast:2 * n_cast + 1]
    qa_ref, sa_ref, sb_ref, m_ref, acc_ref = refs[2 * n_cast + 1:]
    _cast_chunks(cast_src, cast_dst)
    for t in range(ATTN_TILES_PER_STEP):
        i = ATTN_TILES_PER_STEP * pl.program_id(1) + t
        rows = slice(t * ATTN_TQ, (t + 1) * ATTN_TQ)
        streams = []
        for h in range(MOBA_HEADS_PER_STEP):
            sl = slice(h * HEAD_DIM, (h + 1) * HEAD_DIM)
            streams.append(_moba_stream(i, q_ref[rows, sl], k_ref.at[:, sl], v_ref.at[:, sl], kmean_ref[:, sl],
                                        qa_ref.at[h], sa_ref.at[h], sb_ref.at[h], m_ref.at[h], acc_ref.at[h]))
        _flash_pipeline(i, streams, odd=t % 2 == 1)
        for h in range(MOBA_HEADS_PER_STEP):
            acc = acc_ref[h]
            o_ref[rows, h * HEAD_DIM:(h + 1) * HEAD_DIM] = (acc[:, :HEAD_DIM] / acc[:, HEAD_DIM:]).astype(o_ref.dtype)


def _moba_attention(qkv, kmean, casts):
    tq, tk, hp = ATTN_TQ, ATTN_TK, MOBA_HEADS_PER_STEP
    w = hp * HEAD_DIM
    groups = MOBA_HEADS // hp
    tiles = ATTN_TILES_PER_STEP
    assert groups * (SEQ // (tiles * tq)) >= CAST_CHUNKS
    cast_in, cast_out, cast_shape = zip(*[_cast_specs(cw, cl) for cw, cl in casts])
    return pl.pallas_call(
        functools.partial(_moba_kernel, len(casts)),
        grid=(groups, SEQ // (tiles * tq)),
        in_specs=[
            pl.BlockSpec((tiles * tq, w), lambda h, i: (i, h)),
            pl.BlockSpec((SEQ, w), lambda h, i: (0, groups + h), pipeline_mode=pl.Buffered(1)),
            pl.BlockSpec((SEQ, w), lambda h, i: (0, 2 * groups + h), pipeline_mode=pl.Buffered(1)),
            pl.BlockSpec((N_BLOCKS, w), lambda h, i: (0, h)),
            *cast_in,
        ],
        out_specs=[pl.BlockSpec((tiles * tq, w), lambda h, i: (i, h)), *cast_out],
        out_shape=[jax.ShapeDtypeStruct((SEQ, MOBA_WIDTH), jnp.bfloat16), *cast_shape],
        scratch_shapes=[
            pltpu.VMEM((hp, tq, 2 * HEAD_DIM), jnp.bfloat16),
            pltpu.VMEM((hp, tq, tk), jnp.float32),
            pltpu.VMEM((hp, tq, tk), jnp.float32),
            pltpu.VMEM((hp, tq, HEAD_DIM), jnp.float32),
            pltpu.VMEM((hp, tq, 2 * HEAD_DIM), jnp.float32),
        ],
        compiler_params=_cparams(("arbitrary", "arbitrary")),
        name="moba_attention",
    )(qkv, qkv, qkv, kmean, *[cw for cw, _ in casts])


def _diff_stream(i, q, k_ref, v_ref, q2_ref, sa_ref, sb_ref, m_ref, acc_ref):
    tq, tk = ATTN_TQ, ATTN_TK
    q = q * (DIFF_QK_DIM ** -0.5)
    lane = lax.broadcasted_iota(jnp.int32, (tq, HEAD_DIM), 1)
    zero = jnp.zeros_like(q)
    q2_ref[:tq, :] = jnp.where(lane < DIFF_QK_DIM, q, zero)
    q2_ref[tq:, :] = jnp.where(lane >= DIFF_QK_DIM, q, zero)

    def causal_mask(s):
        r = lax.broadcasted_iota(jnp.int32, s.shape, 0)
        r = jnp.where(r >= tq, r - tq, r)
        col = lax.broadcasted_iota(jnp.int32, s.shape, 1)
        return jnp.where(col <= r, s, NEG_INF)

    def qk_past(j, s_ref):
        s_ref[...] = _scores(q2_ref[...], k_ref[pl.ds(pl.multiple_of(j * tk, tk), tk), :])

    def v_tile(j):
        return _with_ones(v_ref[pl.ds(pl.multiple_of(j * tk, tk), tk), :])

    return _Stream(functools.partial(qk_past, i), qk_past, v_tile, causal_mask,
                   sa_ref, sb_ref, m_ref, acc_ref, LOG2E)


def _diff_kernel(lam_init, n_cast, q_ref, k_ref, v_ref, lp_ref, g_ref, *refs):
    cast_src, (o_ref, *cast_dst) = refs[:n_cast], refs[n_cast:2 * n_cast + 1]
    q2_ref, sa_ref, sb_ref, m_ref, acc_ref = refs[2 * n_cast + 1:]
    _cast_chunks(cast_src, cast_dst)
    tq = ATTN_TQ
    lp = lp_ref[...]
    lam = (jnp.exp(jnp.sum(lp[0:1] * lp[1:2], axis=-1, keepdims=True))
           - jnp.exp(jnp.sum(lp[2:3] * lp[3:4], axis=-1, keepdims=True)) + lam_init)
    for t in range(ATTN_TILES_PER_STEP):
        i = ATTN_TILES_PER_STEP * pl.program_id(1) + t
        rows = slice(t * tq, (t + 1) * tq)
        streams = []
        for h in range(DIFF_HEADS_PER_STEP):
            sl = slice(h * HEAD_DIM, (h + 1) * HEAD_DIM)
            streams.append(_diff_stream(i, q_ref[rows, sl], k_ref.at[:, sl], v_ref.at[:, sl],
                                        q2_ref.at[h], sa_ref.at[h], sb_ref.at[h], m_ref.at[h], acc_ref.at[h]))
        _flash_pipeline(i, streams, odd=t % 2 == 1)
        for h in range(DIFF_HEADS_PER_STEP):
            acc = acc_ref[h]
            o = acc[:, :HEAD_DIM] / acc[:, HEAD_DIM:]
            o = o[:tq] - lam * o[tq:]
            o = _rms_norm_rows(o, g_ref[...]) * (1.0 - lam_init)
            o_ref[rows, h * HEAD_DIM:(h + 1) * HEAD_DIM] = o.astype(o_ref.dtype)


def _diff_attention(qkv, lam_params, subln_g, lam_init, casts):
    tq, tk, hp = ATTN_TQ, ATTN_TK, DIFF_HEADS_PER_STEP
    w = hp * HEAD_DIM
    groups = DIFF_HEADS // hp
    base = 3 * MOBA_WIDTH // w
    tiles = ATTN_TILES_PER_STEP
    assert groups * (SEQ // (tiles * tq)) >= CAST_CHUNKS
    cast_in, cast_out, cast_shape = zip(*[_cast_specs(cw, cl) for cw, cl in casts])
    return pl.pallas_call(
        functools.partial(_diff_kernel, lam_init, len(casts)),
        grid=(groups, SEQ // (tiles * tq)),
        in_specs=[
            pl.BlockSpec((tiles * tq, w), lambda h, i: (i, base + h)),
            pl.BlockSpec((SEQ, w), lambda h, i: (0, base + groups + h), pipeline_mode=pl.Buffered(1)),
            pl.BlockSpec((SEQ, w), lambda h, i: (0, base + 2 * groups + h), pipeline_mode=pl.Buffered(1)),
            pl.BlockSpec((4, DIFF_QK_DIM), lambda h, i: (0, 0)),
            pl.BlockSpec((1, HEAD_DIM), lambda h, i: (0, 0)),
            *cast_in,
        ],
        out_specs=[pl.BlockSpec((tiles * tq, w), lambda h, i: (i, h)), *cast_out],
        out_shape=[jax.ShapeDtypeStruct((SEQ, DIFF_WIDTH), jnp.bfloat16), *cast_shape],
        scratch_shapes=[
            pltpu.VMEM((hp, 2 * tq, HEAD_DIM), jnp.bfloat16),
            pltpu.VMEM((hp, 2 * tq, tk), jnp.float32),
            pltpu.VMEM((hp, 2 * tq, tk), jnp.float32),
            pltpu.VMEM((hp, 2 * tq, HEAD_DIM), jnp.float32),
            pltpu.VMEM((hp, 2 * tq, 2 * HEAD_DIM), jnp.float32),
        ],
        compiler_params=_cparams(("arbitrary", "arbitrary")),
        name="diff_attention",
    )(qkv, qkv, qkv, lam_params, subln_g, *[cw for cw, _ in casts])


def _out_proj_kernel(x_ref, a_ref, b_ref, c_ref, w_ref, g_ref, o_ref, h_ref):
    half = OUT_TM // 2
    for r0 in range(0, OUT_TM, half):
        rows = slice(r0, r0 + half)
        acc = jnp.dot(a_ref[rows, :], w_ref[0:MOBA_WIDTH, :], preferred_element_type=jnp.float32)
        acc += jnp.dot(b_ref[rows, :], w_ref[MOBA_WIDTH:MOBA_WIDTH + DIFF_WIDTH, :],
                       preferred_element_type=jnp.float32)
        acc += jnp.dot(c_ref[rows, :], w_ref[MOBA_WIDTH + DIFF_WIDTH:, :], preferred_element_type=jnp.float32)
        y = x_ref[rows, :] + acc
        o_ref[rows, :] = y
        h_ref[rows, :] = _rms_norm_rows(y, g_ref[...]).astype(h_ref.dtype)


def _out_proj(x, moba_o, diff_o, sgu_o, w_out, g_mlp):
    tm = OUT_TM
    return pl.pallas_call(
        _out_proj_kernel,
        grid=(SEQ // tm,),
        in_specs=[
            pl.BlockSpec((tm, D_MODEL), lambda i: (i, 0)),
            pl.BlockSpec((tm, MOBA_WIDTH), lambda i: (i, 0)),
            pl.BlockSpec((tm, DIFF_WIDTH), lambda i: (i, 0)),
            pl.BlockSpec((tm, SGU_WIDTH), lambda i: (i, 0)),
            pl.BlockSpec((D_MODEL, D_MODEL), lambda i: (0, 0)),
            pl.BlockSpec((1, D_MODEL), lambda i: (0, 0)),
        ],
        out_specs=[
            pl.BlockSpec((tm, D_MODEL), lambda i: (i, 0)),
            pl.BlockSpec((tm, D_MODEL), lambda i: (i, 0)),
        ],
        out_shape=[
            jax.ShapeDtypeStruct((SEQ, D_MODEL), jnp.float32),
            jax.ShapeDtypeStruct((SEQ, D_MODEL), jnp.bfloat16),
        ],
        compiler_params=_cparams(("arbitrary",)),
        name="out_proj",
    )(x, moba_o, diff_o, sgu_o, w_out, g_mlp)


def _mlp_kernel(final, x_ref, h_ref, w1_ref, w2_ref, gf_ref, o_ref):
    j = pl.program_id(1)

    def hidden_tile(base_ref):
        a = jnp.dot(h_ref[...], w1_ref[...], preferred_element_type=jnp.float32)
        a = jnp.square(jnp.maximum(a, 0.0)).astype(jnp.bfloat16)
        o_ref[...] = base_ref[...] + jnp.dot(a, w2_ref[...], preferred_element_type=jnp.float32)

    @pl.when(j == 0)
    def _():
        hidden_tile(x_ref)

    @pl.when(j > 0)
    def _():
        hidden_tile(o_ref)

    if final:
        @pl.when(j == pl.num_programs(1) - 1)
        def _():
            o_ref[...] = _rms_norm_rows(o_ref[...], gf_ref[...])


def _mlp(x, h, w1, w2, g_final, final):
    tm, tf = MLP_TM, MLP_TF
    return pl.pallas_call(
        functools.partial(_mlp_kernel, final),
        grid=(SEQ // tm, D_FF // tf),
        in_specs=[
            pl.BlockSpec((tm, D_MODEL), lambda i, j: (i, 0)),
            pl.BlockSpec((tm, D_MODEL), lambda i, j: (i, 0)),
            pl.BlockSpec((D_MODEL, tf), lambda i, j: (0, j)),
            pl.BlockSpec((tf, D_MODEL), lambda i, j: (j, 0)),
            pl.BlockSpec((1, D_MODEL), lambda i, j: (0, 0)),
        ],
        out_specs=pl.BlockSpec((tm, D_MODEL), lambda i, j: (i, 0)),
        out_shape=jax.ShapeDtypeStruct((SEQ, D_MODEL), jnp.float32),
        compiler_params=_cparams(("arbitrary", "arbitrary"), MLP_VMEM_LIMIT_BYTES),
        name="mlp",
    )(x, h, w1, w2, g_final)


def _rope_tables():
    pos = jnp.arange(SEQ, dtype=jnp.float32)[:, None]
    lane = jnp.arange(HEAD_DIM)

    def angles(dim):
        inv = 1.0 / (ROPE_THETA ** (jnp.arange(0, dim, 2, dtype=jnp.float32) / dim))
        return pos * jnp.tile(inv, 2 * HEAD_DIM // dim)[None, :]

    ang_m, ang_d = angles(HEAD_DIM), angles(DIFF_QK_DIM)
    sin_m, sin_d = jnp.sin(ang_m), jnp.sin(ang_d)
    first_m = (lane % HEAD_DIM < HEAD_DIM // 2)[None, :]
    first_d = (lane % DIFF_QK_DIM < DIFF_QK_DIM // 2)[None, :]
    return (jnp.cos(ang_m), jnp.where(first_m, -sin_m, sin_m),
            jnp.cos(ang_d),
            jnp.where(first_d, -sin_d, 0.0),
            jnp.where(first_d, 0.0, sin_d))


def kernel(x, attn_norm_g, w_in, diff_lambda, diff_subln_g, sgu_ln_g, sgu_ln_b, sgu_w, sgu_b,
           w_out, mlp_norm_g, w_mlp_in, w_mlp_out, final_norm_g):
    assert x.shape == (1, SEQ, D_MODEL)
    bf = jnp.bfloat16
    xs = x.reshape(SEQ, D_MODEL)
    cm, sm, cd, sdl, sdh = _rope_tables()
    gf = final_norm_g.reshape(1, D_MODEL)
    w_in_l = w_in[0].astype(bf)
    for l in range(DEPTH):
        lam_init = 0.8 - 0.6 * math.exp(-0.3 * l)
        g_attn = attn_norm_g[l].reshape(1, D_MODEL)
        qkv, kmean, sgu_o = _in_proj(xs, g_attn, w_in_l, cm, sm, cd, sdl, sdh, sgu_ln_g[l], sgu_ln_b[l],
                                     sgu_w[l], sgu_b[l].reshape(SGU_GROUPS, SGU_CHUNK, 1))
        kmean = kmean.reshape(N_BLOCKS, MOBA_WIDTH)
        moba_o, w1_l, w_out_l = _moba_attention(qkv, kmean, [(w_mlp_in, l), (w_out, l)])
        diff_casts = [(w_mlp_out, l)] + ([(w_in, l + 1)] if l + 1 < DEPTH else [])
        diff_o, w2_l, *w_in_next = _diff_attention(qkv, diff_lambda[l], diff_subln_g[l].reshape(1, HEAD_DIM),
                                                   lam_init, diff_casts)
        xs, h_mlp = _out_proj(xs, moba_o, diff_o, sgu_o, w_out_l, mlp_norm_g[l].reshape(1, D_MODEL))
        xs = _mlp(xs, h_mlp, w1_l, w2_l, gf, l == DEPTH - 1)
        if w_in_next:
            w_in_l = w_in_next[0]
    return xs.reshape(1, SEQ, D_MODEL)
```
